```python
import math
import jax, jax.numpy as jnp
from jax import lax
import numpy as np

D_MODEL = 2048
BATCH = 1
SEQ = 16384
DEPTH = 2

HEAD_DIM = 128
BRANCH_HEADS = 4
BRANCH_WIDTH = BRANCH_HEADS * HEAD_DIM
N_BRANCH = 4
A_KV_HEADS = 2
A_HALF_WINDOW = 128
B_WIN_ROWS = 8
B_WIN_COLS = 16
C_PATTERNS = ((128, 1), (512, 4), (2048, 16))
C_GROUPS = len(C_PATTERNS)
C_KV_HEADS = BRANCH_HEADS
D_KV_HEADS = 2
GRID_W = 64
ROPE_THETA = 10000.0
Q_BLOCK = 128
NORM_EPS = 1e-6
F32 = jnp.float32

IN_SPLITS = (
    BRANCH_WIDTH, A_KV_HEADS * HEAD_DIM, A_KV_HEADS * HEAD_DIM, BRANCH_WIDTH,
    BRANCH_WIDTH, BRANCH_WIDTH, BRANCH_WIDTH, BRANCH_WIDTH,
    C_GROUPS * BRANCH_WIDTH, C_KV_HEADS * HEAD_DIM, C_KV_HEADS * HEAD_DIM, BRANCH_WIDTH,
    BRANCH_WIDTH, D_KV_HEADS * HEAD_DIM, D_KV_HEADS * HEAD_DIM, BRANCH_WIDTH,
    N_BRANCH * D_MODEL,
)
IN_WIDTH = sum(IN_SPLITS)
IN_OFFSETS = tuple(int(o) for o in np.cumsum(IN_SPLITS)[:-1])

kernel_name = 'hybrid_gated_four_mixer_encoder'


def rms_norm(x, g):
    xf = x.astype(F32)
    y = xf * lax.rsqrt(jnp.mean(xf * xf, axis=-1, keepdims=True) + NORM_EPS)
    return (y * g.astype(F32)).astype(x.dtype)


def rope_angles(pos, dim):
    inv_freq = ROPE_THETA ** (-jnp.arange(dim // 2, dtype=F32) / (dim // 2))
    return pos.astype(F32)[:, None] * inv_freq[None, :]


def apply_rope(x, ang):
    half = x.shape[-1] // 2
    cos = jnp.cos(ang)[None, :, None, :]
    sin = jnp.sin(ang)[None, :, None, :]
    xf = x.astype(F32)
    x1, x2 = xf[..., :half], xf[..., half:]
    return jnp.concatenate([x1 * cos - x2 * sin, x2 * cos + x1 * sin], axis=-1).astype(x.dtype)


def axial_rope(x, ang_row, ang_col):
    half = x.shape[-1] // 2
    return jnp.concatenate([apply_rope(x[..., :half], ang_row), apply_rope(x[..., half:], ang_col)], axis=-1)


def banded_attention(q, k, v, half_window, block, sink=None):
    n, length, hk, g, dh = q.shape
    nb = length // block
    span = block + 2 * half_window
    qb = q.reshape(n, nb, block, hk, g, dh)
    pad = ((0, 0), (half_window, half_window), (0, 0), (0, 0))
    kidx = (jnp.arange(nb) * block)[:, None] + jnp.arange(span)[None, :]
    kb = jnp.pad(k, pad)[:, kidx]
    vb = jnp.pad(v, pad)[:, kidx]
    qpos = jnp.arange(length).reshape(nb, block)
    kpos = (kidx - half_window)[:, None, :]
    valid = (jnp.abs(kpos - qpos[:, :, None]) <= half_window) & (kpos >= 0) & (kpos < length)
    s = jnp.einsum('nbqhgd,nbkhd->nbhgqk', qb, kb, preferred_element_type=F32)
    s = jnp.where(valid[None, :, None, None], s, -jnp.inf)
    m = jnp.max(s, axis=-1, keepdims=True)
    if sink is not None:
        sink_b = sink.astype(F32)[None, None, :, :, None, None]
        m = jnp.maximum(m, sink_b)
    p = jnp.exp(s - m)
    denom = jnp.sum(p, axis=-1, keepdims=True)
    if sink is not None:
        denom = denom + jnp.exp(sink_b - m)
    o = jnp.einsum('nbhgqk,nbkhd->nbqhgd', p, vb.astype(F32)) / jnp.moveaxis(denom, 4, 2)
    lse = jnp.moveaxis((m + jnp.log(denom))[..., 0], 4, 2)
    return o.reshape(n, length, hk, g, dh).astype(q.dtype), lse.reshape(n, length, hk, g)


def to_strided(t, dil):
    b, s = t.shape[:2]
    rest = t.shape[2:]
    return jnp.moveaxis(t.reshape(b, s // dil, dil, *rest), 2, 1).reshape(b * dil, s // dil, *rest)


def from_strided(t, dil, b):
    nl = t.shape[1]
    rest = t.shape[2:]
    return jnp.moveaxis(t.reshape(b, dil, nl, *rest), 1, 2).reshape(b, dil * nl, *rest)


def dilated_mixture_attention(q, k, v):
    b, s = q.shape[:2]
    outs, lses = [], []
    for gi, (window, dil) in enumerate(C_PATTERNS):
        sub_len = s // dil
        half = window // (2 * dil)
        block = math.gcd(sub_len, Q_BLOCK)
        o, lse = banded_attention(to_strided(q[:, :, gi], dil)[:, :, :, None], to_strided(k, dil),
                                  to_strided(v, dil), half, block)
        outs.append(from_strided(o[:, :, :, 0], dil, b))
        lses.append(from_strided(lse[:, :, :, 0], dil, b))
    w = jax.nn.softmax(jnp.stack(lses, axis=0), axis=0)
    return jnp.einsum('gbsh,gbshd->bshd', w, jnp.stack(outs, axis=0).astype(F32)).astype(q.dtype)


def neighbourhood_attention(q, k, v, rpb):
    b, s, h, dh = q.shape
    rows = s // GRID_W
    kr = min(B_WIN_ROWS, rows)
    qg = q.reshape(b, rows, GRID_W, h, dh)
    kg = k.reshape(b, rows, GRID_W, h, dh)
    vg = v.reshape(b, rows, GRID_W, h, dh)
    r = jnp.arange(rows)
    row_start = jnp.clip(r - kr // 2, 0, rows - kr)
    ridx = row_start[:, None] + jnp.arange(kr)[None, :]
    kb = kg[:, ridx]
    vb = vg[:, ridx]
    s_ = jnp.einsum('brchd,brjkhd->brhcjk', qg, kb, preferred_element_type=F32)
    c = jnp.arange(GRID_W)
    col_start = jnp.clip(c - B_WIN_COLS // 2, 0, GRID_W - B_WIN_COLS)
    col_ok = (c[None, :] >= col_start[:, None]) & (c[None, :] < col_start[:, None] + B_WIN_COLS)
    dr = ridx - r[:, None] + (B_WIN_ROWS - 1)
    dc = jnp.clip(c[None, :] - c[:, None] + (B_WIN_COLS - 1), 0, 2 * B_WIN_COLS - 2)
    bias = rpb[:, dr[:, None, :, None], dc[None, :, None, :]]
    s_ = s_ + jnp.moveaxis(bias, 0, 1)[None].astype(F32)
    s_ = jnp.where(col_ok[:, None, :], s_, -jnp.inf)
    p = jax.nn.softmax(s_.reshape(b, rows, h, GRID_W, kr * GRID_W), axis=-1).reshape(s_.shape)
    o = jnp.einsum('brhcjk,brjkhd->brchd', p, vb.astype(F32))
    return o.reshape(b, s, h, dh).astype(q.dtype)


def dense_blocked_attention(q, k, v):
    b, s, hk, g, dh = q.shape
    nb = s // Q_BLOCK
    qb = jnp.moveaxis(q.reshape(b, nb, Q_BLOCK, hk, g, dh), 1, 0)
    vf = v.astype(F32)

    def one_block(qi):
        sc = jnp.einsum('bqhgd,bkhd->bhgqk', qi, k, preferred_element_type=F32)
        p = jax.nn.softmax(sc, axis=-1)
        return jnp.einsum('bhgqk,bkhd->bqhgd', p, vf).astype(q.dtype)

    o = lax.map(one_block, qb)
    return jnp.moveaxis(o, 0, 1).reshape(b, s, hk * g, dh)


def hybrid_layer(x, norm_g, w_in, gate_b, a_sink, b_rpb, d_q_norm, d_k_norm, w_branch, w_out,
                 ang_1d, ang_row, ang_col):
    b, s, _ = x.shape
    scale = HEAD_DIM ** -0.5
    h = rms_norm(x, norm_g)
    u = jnp.einsum('bsd,de->bse', h, w_in)
    (aq, ak, av, az, bq, bk, bv, bz, cq, ck, cv, cz, dq, dk, dv, dz, gates) = jnp.split(u, IN_OFFSETS, axis=-1)

    def heads(t):
        return t.reshape(b, s, t.shape[-1] // HEAD_DIM, HEAD_DIM)

    qa = (apply_rope(heads(aq), ang_1d) * scale).reshape(b, s, A_KV_HEADS, BRANCH_HEADS // A_KV_HEADS, HEAD_DIM)
    ya, _ = banded_attention(qa, apply_rope(heads(ak), ang_1d), heads(av), A_HALF_WINDOW, Q_BLOCK,
                             sink=a_sink.reshape(A_KV_HEADS, BRANCH_HEADS // A_KV_HEADS))
    ya = ya.reshape(b, s, BRANCH_WIDTH)
    yb = neighbourhood_attention(heads(bq) * scale, heads(bk), heads(bv), b_rpb).reshape(b, s, BRANCH_WIDTH)
    qc = (apply_rope(heads(cq), ang_1d) * scale).reshape(b, s, C_GROUPS, C_KV_HEADS, HEAD_DIM)
    yc = dilated_mixture_attention(qc, apply_rope(heads(ck), ang_1d), heads(cv)).reshape(b, s, BRANCH_WIDTH)
    qd = axial_rope(rms_norm(heads(dq), d_q_norm), ang_row, ang_col) * scale
    kd = axial_rope(rms_norm(heads(dk), d_k_norm), ang_row, ang_col)
    qd = qd.reshape(b, s, D_KV_HEADS, BRANCH_HEADS // D_KV_HEADS, HEAD_DIM)
    yd = dense_blocked_attention(qd, kd, heads(dv)).reshape(b, s, BRANCH_WIDTH)

    y = jnp.stack([ya, yb, yc, yd], axis=2)
    z = jnp.stack([az, bz, cz, dz], axis=2)
    branch = jnp.einsum('bsnm,nmd->bsnd', y * jax.nn.silu(z), w_branch)
    gate = jax.nn.sigmoid(gates.reshape(b, s, N_BRANCH, D_MODEL) + gate_b)
    merged = jnp.sum(gate * branch, axis=2)
    return x + jnp.einsum('bsd,de->bse', merged, w_out)


def setup_inputs(seed: int = 0) -> dict:
    key = jax.random.key(seed)
    ks = jax.random.split(key, 11)
    nrm = jax.random.normal
    return {
        'x': nrm(ks[0], (BATCH, SEQ, D_MODEL), F32),
        'norm_g': 1.0 + 0.02 * nrm(ks[1], (DEPTH, D_MODEL), F32),
        'w_in': nrm(ks[2], (DEPTH, D_MODEL, IN_WIDTH), F32) * D_MODEL ** -0.5,
        'gate_b': 0.1 * nrm(ks[3], (DEPTH, N_BRANCH, D_MODEL), F32),
        'a_sink': 0.5 * nrm(ks[4], (DEPTH, BRANCH_HEADS), F32),
        'b_rpb': 0.1 * nrm(ks[5], (DEPTH, BRANCH_HEADS, 2 * B_WIN_ROWS - 1, 2 * B_WIN_COLS - 1), F32),
        'd_q_norm': 1.0 + 0.02 * nrm(ks[6], (DEPTH, HEAD_DIM), F32),
        'd_k_norm': 1.0 + 0.02 * nrm(ks[7], (DEPTH, HEAD_DIM), F32),
        'w_branch': nrm(ks[8], (DEPTH, N_BRANCH, BRANCH_WIDTH, D_MODEL), F32) * BRANCH_WIDTH ** -0.5,
        'w_out': nrm(ks[9], (DEPTH, D_MODEL, D_MODEL), F32) * D_MODEL ** -0.5,
        'final_norm_g': 1.0 + 0.02 * nrm(ks[10], (D_MODEL,), F32),
    }


def reference(x, norm_g, w_in, gate_b, a_sink, b_rpb, d_q_norm, d_k_norm, w_branch, w_out, final_norm_g):
    s = x.shape[1]
    t = jnp.arange(s)
    ang_1d = rope_angles(t, HEAD_DIM)
    ang_row = rope_angles(t // GRID_W, HEAD_DIM // 2)
    ang_col = rope_angles(t % GRID_W, HEAD_DIM // 2)
    for layer in range(DEPTH):
        x = hybrid_layer(x, norm_g[layer], w_in[layer], gate_b[layer], a_sink[layer], b_rpb[layer],
                         d_q_norm[layer], d_k_norm[layer], w_branch[layer], w_out[layer],
                         ang_1d, ang_row, ang_col)
    return rms_norm(x, final_norm_g)
```

```python
import functools
import math

import numpy as np
import jax
import jax.numpy as jnp
from jax import lax
from jax.experimental import pallas as pl
from jax.experimental.pallas import tpu as pltpu

F32 = jnp.float32
BF16 = jnp.bfloat16

D_MODEL = 2048
HEAD_DIM = 128
BRANCH_WIDTH = 512
N_BRANCH = 4
A_HALF_WINDOW = 128
B_WIN_ROWS = 8
B_WIN_COLS = 16
C_PATTERNS = ((128, 1), (512, 4), (2048, 16))
GRID_W = 64
ROPE_THETA = 10000.0
NORM_EPS = 1e-6
SCALE = HEAD_DIM ** -0.5
NEG_BIG = -1e30

LANE = 128
VMEM_LIMIT = 56 * 1024 * 1024

_SPLITS = (512, 256, 256, 512, 512, 512, 512, 512, 1536, 512, 512, 512, 512, 256, 256, 512, 8192)
_NAMES = ("aq", "ak", "av", "az", "bq", "bk", "bv", "bz", "cq", "ck", "cv", "cz", "dq", "dk", "dv", "dz", "gates")
_OFF = dict(zip(_NAMES, np.concatenate([[0], np.cumsum(_SPLITS)[:-1]]).tolist()))
_WID = dict(zip(_NAMES, _SPLITS))

ROPE_SEG = ("aq", "ak", "cq", "ck")
NORM_SEG = ("dq", "dk")
PLAIN_SEG = ("az", "bz", "cz", "dz", "av", "dv", "bq", "bk", "bv", "cv")


def _seg_offsets(seg):
    offs, o = {}, 0
    for n in seg:
        offs[n] = o
        o += _WID[n]
    return offs, o


ROPE_OFF, ROPE_W = _seg_offsets(ROPE_SEG)
NORM_OFF, NORM_W = _seg_offsets(NORM_SEG)
PLAIN_OFF, PLAIN_W = _seg_offsets(PLAIN_SEG)


def _params(sem):
    return pltpu.CompilerParams(dimension_semantics=sem, vmem_limit_bytes=VMEM_LIMIT)


def _sigmoid(v):
    return 1.0 / (1.0 + jnp.exp(-v))


def _rmsnorm_kernel(x_ref, g_ref, o_ref):
    x = x_ref[...]
    y = x * lax.rsqrt(jnp.mean(x * x, axis=-1, keepdims=True) + NORM_EPS)
    o_ref[...] = (y * g_ref[...]).astype(o_ref.dtype)


def _rmsnorm(x, g, out_dtype, tm=512):
    s, d = x.shape
    return pl.pallas_call(
        _rmsnorm_kernel,
        out_shape=jax.ShapeDtypeStruct((s, d), out_dtype),
        grid=(s // tm,),
        in_specs=[pl.BlockSpec((tm, d), lambda i: (i, 0)), pl.BlockSpec((1, d), lambda i: (0, 0))],
        out_specs=pl.BlockSpec((tm, d), lambda i: (i, 0)),
        compiler_params=_params(("arbitrary",)),
        name="rmsnorm",
    )(x, g.reshape(1, d))


def _rope_1d(xh, cos, sin_signed):
    return xh * cos + pltpu.roll(xh, HEAD_DIM // 2, 1) * sin_signed


def _rope_axial(xh, cos, sin_lo, sin_hi):
    return xh * cos + pltpu.roll(xh, HEAD_DIM - 32, 1) * sin_lo + pltpu.roll(xh, 32, 1) * sin_hi


def _proj_kernel(*refs, mode):
    h_ref, w_ref = refs[0], refs[1]
    o_ref = refs[-1]
    acc = jnp.dot(h_ref[...], w_ref[...], preferred_element_type=F32)
    tn = acc.shape[1]
    if mode == "scale":
        o_ref[...] = (acc * refs[2][...]).astype(o_ref.dtype)
    elif mode == "sigmoid":
        o_ref[...] = _sigmoid(acc + refs[2][...]).astype(o_ref.dtype)
    elif mode == "rope":
        cs_ref, cos_ref, sin_ref = refs[2:5]
        cos, sin = cos_ref[...], sin_ref[...]
        for c in range(tn // HEAD_DIM):
            sl = slice(c * HEAD_DIM, (c + 1) * HEAD_DIM)
            o_ref[:, sl] = (_rope_1d(acc[:, sl], cos, sin) * cs_ref[:, sl]).astype(o_ref.dtype)
    elif mode == "norm_rope":
        cs_ref, g_ref, cos_ref, slo_ref, shi_ref = refs[2:7]
        cos, slo, shi = cos_ref[...], slo_ref[...], shi_ref[...]
        for c in range(tn // HEAD_DIM):
            sl = slice(c * HEAD_DIM, (c + 1) * HEAD_DIM)
            xh = acc[:, sl]
            y = xh * lax.rsqrt(jnp.mean(xh * xh, axis=-1, keepdims=True) + NORM_EPS) * g_ref[:, sl]
            o_ref[:, sl] = (_rope_axial(y, cos, slo, shi) * cs_ref[:, sl]).astype(o_ref.dtype)
    else:
        raise ValueError(mode)


def _proj(h, w, mode, row_vecs, tables, tm, tn, name):
    s, d = h.shape
    n = w.shape[1]
    in_specs = [pl.BlockSpec((tm, d), lambda i, j: (i, 0)), pl.BlockSpec((d, tn), lambda i, j: (0, j))]
    in_specs += [pl.BlockSpec((1, tn), lambda i, j: (0, j)) for _ in row_vecs]
    in_specs += [pl.BlockSpec((tm, HEAD_DIM), lambda i, j: (i, 0)) for _ in tables]
    return pl.pallas_call(
        functools.partial(_proj_kernel, mode=mode),
        out_shape=jax.ShapeDtypeStruct((s, n), BF16),
        grid=(s // tm, n // tn),
        in_specs=in_specs,
        out_specs=pl.BlockSpec((tm, tn), lambda i, j: (i, j)),
        compiler_params=_params(("arbitrary", "arbitrary")),
        name=name,
    )(h, w, *row_vecs, *tables)


def _banded_kernel(*refs, seq_len, half_window, bq, nsub, groups, has_sink, want_lse):
    if has_sink:
        sink_ref, refs = refs[0], refs[1:]
    q_ref, k_ref, v_ref, o_ref = refs[:4]
    lse_ref = refs[4] if want_lse else None
    col = pl.program_id(0)
    step = pl.program_id(1)
    span = bq + 2 * half_window
    rows = groups * bq

    def body(b, carry):
        q0 = (step * nsub + b) * bq
        start = pl.multiple_of(jnp.clip(q0 - half_window, 0, seq_len - span), 64)
        k = k_ref[pl.ds(start, span), :]
        v = v_ref[pl.ds(start, span), :]
        qoff = pl.multiple_of(b * bq, bq)
        qb = q_ref[pl.ds(qoff, bq), :]
        if groups > 1:
            q2 = jnp.concatenate([qb[:, g * HEAD_DIM:(g + 1) * HEAD_DIM] for g in range(groups)], axis=0)
        else:
            q2 = qb
        s = lax.dot_general(q2, k, (((1,), (1,)), ((), ())), preferred_element_type=F32)
        row = lax.broadcasted_iota(jnp.int32, (rows, span), 0)
        qpos = q0 + (row % bq if groups > 1 else row)
        kpos = start + lax.broadcasted_iota(jnp.int32, (rows, span), 1)
        s = jnp.where(jnp.abs(kpos - qpos) <= half_window, s, NEG_BIG)
        m = jnp.max(s, axis=1, keepdims=True)
        if has_sink:
            r1 = lax.broadcasted_iota(jnp.int32, (rows, 1), 0)
            sink = jnp.full((rows, 1), sink_ref[col * groups], F32)
            for g in range(1, groups):
                sink = jnp.where(r1 >= g * bq, sink_ref[col * groups + g], sink)
            m = jnp.maximum(m, sink)
        p = jnp.exp(s - m)
        denom = jnp.sum(p, axis=1, keepdims=True)
        if has_sink:
            denom = denom + jnp.exp(sink - m)
        o = jnp.dot(p.astype(BF16), v, preferred_element_type=F32) / denom
        for g in range(groups):
            o_ref[pl.ds(qoff, bq), g * HEAD_DIM:(g + 1) * HEAD_DIM] = o[g * bq:(g + 1) * bq].astype(o_ref.dtype)
        if want_lse:
            lse = m + jnp.log(denom)
            for g in range(groups):
                lse_ref[pl.ds(qoff, bq), g * HEAD_DIM:(g + 1) * HEAD_DIM] = jnp.broadcast_to(
                    lse[g * bq:(g + 1) * bq], (bq, HEAD_DIM))
        return carry

    lax.fori_loop(0, nsub, body, 0)


def _banded_attention(q_arr, k_arr, v_arr, q_col, k_col, v_col, *, seq_len, n_cols, o_cols, half_window,
                      groups, sink, want_lse, name, bq=128, max_step=1024):
    nsub = min(max_step, seq_len) // bq
    tq = nsub * bq
    qw = groups * HEAD_DIM
    kern = functools.partial(_banded_kernel, seq_len=seq_len, half_window=half_window, bq=bq, nsub=nsub,
                             groups=groups, has_sink=sink is not None, want_lse=want_lse)
    in_specs = [pl.BlockSpec((tq, qw), lambda c, i: (i, q_col(c))),
                pl.BlockSpec((seq_len, HEAD_DIM), lambda c, i: (0, k_col(c))),
                pl.BlockSpec((seq_len, HEAD_DIM), lambda c, i: (0, v_col(c)))]
    args = [q_arr, k_arr, v_arr]
    if sink is not None:
        in_specs = [pl.BlockSpec(memory_space=pltpu.SMEM)] + in_specs
        args = [sink] + args
    out_shape = [jax.ShapeDtypeStruct((seq_len, o_cols * qw), BF16)]
    out_specs = [pl.BlockSpec((tq, qw), lambda c, i: (i, c))]
    if want_lse:
        out_shape.append(jax.ShapeDtypeStruct((seq_len, o_cols * qw), F32))
        out_specs.append(pl.BlockSpec((tq, qw), lambda c, i: (i, c)))
    return pl.pallas_call(
        kern, out_shape=out_shape, grid=(n_cols, seq_len // tq), in_specs=in_specs, out_specs=out_specs,
        compiler_params=_params(("arbitrary", "arbitrary")), name=name,
    )(*args)


def _nbr_bias_kernel(rpb_ref, o_ref, t_ref):
    h = pl.program_id(0)
    n_dr, n_dc = 2 * B_WIN_ROWS - 1, 2 * B_WIN_COLS - 1
    qc = lax.broadcasted_iota(jnp.int32, (GRID_W, GRID_W), 0)
    kc = lax.broadcasted_iota(jnp.int32, (GRID_W, GRID_W), 1)
    col_start = jnp.clip(qc - B_WIN_COLS // 2, 0, GRID_W - B_WIN_COLS)
    col_ok = (kc >= col_start) & (kc < col_start + B_WIN_COLS)
    dc = kc - qc + (B_WIN_COLS - 1)
    for dr in range(n_dr):
        t = jnp.full((GRID_W, GRID_W), NEG_BIG, F32)
        for d in range(n_dc):
            t = jnp.where(dc == d, rpb_ref[(h * n_dr + dr) * n_dc + d], t)
        t_ref[dr] = jnp.where(col_ok, t, NEG_BIG)
    for o in range(B_WIN_ROWS):
        for j in range(B_WIN_ROWS):
            o_ref[o, :, j * GRID_W:(j + 1) * GRID_W] = t_ref[j - o + B_WIN_ROWS - 1]


def _nbr_bias(rpb):
    n_heads = rpb.shape[0]
    return pl.pallas_call(
        _nbr_bias_kernel,
        out_shape=jax.ShapeDtypeStruct((n_heads, B_WIN_ROWS, GRID_W, B_WIN_ROWS * GRID_W), F32),
        grid=(n_heads,),
        in_specs=[pl.BlockSpec(memory_space=pltpu.SMEM)],
        out_specs=pl.BlockSpec((None, B_WIN_ROWS, GRID_W, B_WIN_ROWS * GRID_W), lambda h: (h, 0, 0, 0)),
        scratch_shapes=[pltpu.VMEM((2 * B_WIN_ROWS - 1, GRID_W, GRID_W), F32)],
        compiler_params=_params(("arbitrary",)),
        name="nbr_bias",
    )(rpb.reshape(-1))


def _nbr_kernel(q_ref, k_ref, v_ref, bias_ref, o_ref, *, n_rows, rows_per_step):
    step = pl.program_id(1)
    win = B_WIN_ROWS * GRID_W

    def body(rl, carry):
        r = step * rows_per_step + rl
        row_start = jnp.clip(r - B_WIN_ROWS // 2, 0, n_rows - B_WIN_ROWS)
        kstart = pl.multiple_of(row_start * GRID_W, GRID_W)
        k = k_ref[pl.ds(kstart, win), :]
        v = v_ref[pl.ds(kstart, win), :]
        qoff = pl.multiple_of(rl * GRID_W, GRID_W)
        q = q_ref[pl.ds(qoff, GRID_W), :]
        s = lax.dot_general(q, k, (((1,), (1,)), ((), ())), preferred_element_type=F32)
        s = s + bias_ref[r - row_start]
        m = jnp.max(s, axis=1, keepdims=True)
        p = jnp.exp(s - m)
        denom = jnp.sum(p, axis=1, keepdims=True)
        o = jnp.dot(p.astype(BF16), v, preferred_element_type=F32) / denom
        o_ref[pl.ds(qoff, GRID_W), :] = o.astype(o_ref.dtype)
        return carry

    lax.fori_loop(0, rows_per_step, body, 0)


def _nbr_attention(p_arr, bias, *, seq_len, rows_per_step=16):
    n_rows = seq_len // GRID_W
    rows_per_step = min(rows_per_step, n_rows)
    tq = rows_per_step * GRID_W
    qc, kc, vc = (PLAIN_OFF[n] // HEAD_DIM for n in ("bq", "bk", "bv"))
    return pl.pallas_call(
        functools.partial(_nbr_kernel, n_rows=n_rows, rows_per_step=rows_per_step),
        out_shape=jax.ShapeDtypeStruct((seq_len, BRANCH_WIDTH), BF16),
        grid=(BRANCH_WIDTH // HEAD_DIM, n_rows // rows_per_step),
        in_specs=[pl.BlockSpec((tq, HEAD_DIM), lambda h, i: (i, qc + h)),
                  pl.BlockSpec((seq_len, HEAD_DIM), lambda h, i: (0, kc + h)),
                  pl.BlockSpec((seq_len, HEAD_DIM), lambda h, i: (0, vc + h)),
                  pl.BlockSpec((None, B_WIN_ROWS, GRID_W, B_WIN_ROWS * GRID_W), lambda h, i: (h, 0, 0, 0))],
        out_specs=pl.BlockSpec((tq, HEAD_DIM), lambda h, i: (i, h)),
        compiler_params=_params(("arbitrary", "arbitrary")),
        name="nbr_attention",
    )(p_arr, p_arr, p_arr, bias)


def _dense_kernel(q_ref, k_ref, v_ref, o_ref, q2_ref, v2_ref, m_ref, acc_ref, *, seq_len, tq, tk):
    @pl.when(pl.program_id(1) == 0)
    def _():
        v2_ref[:, :HEAD_DIM] = v_ref[...]
        v2_ref[:, HEAD_DIM:] = jnp.ones((seq_len, HEAD_DIM), BF16)

    q2_ref[:tq] = q_ref[:, :HEAD_DIM]
    q2_ref[tq:] = q_ref[:, HEAD_DIM:]
    m_ref[...] = jnp.full(m_ref.shape, NEG_BIG, F32)
    acc_ref[...] = jnp.zeros(acc_ref.shape, F32)

    def body(j, carry):
        koff = pl.multiple_of(j * tk, tk)
        k = k_ref[pl.ds(koff, tk), :]
        s = lax.dot_general(q2_ref[...], k, (((1,), (1,)), ((), ())), preferred_element_type=F32)
        m_prev = m_ref[...]
        m_new = jnp.maximum(m_prev, jnp.max(s, axis=1, keepdims=True))
        alpha = jnp.exp(m_prev - m_new)
        p = jnp.exp(s - m_new).astype(BF16)
        acc_ref[...] = acc_ref[...] * alpha + jnp.dot(p, v2_ref[pl.ds(koff, tk), :], preferred_element_type=F32)
        m_ref[...] = m_new
        return carry

    lax.fori_loop(0, seq_len // tk, body, 0)
    acc = acc_ref[...]
    o = acc[:, :HEAD_DIM] / acc[:, HEAD_DIM:]
    o_ref[:, :HEAD_DIM] = o[:tq].astype(o_ref.dtype)
    o_ref[:, HEAD_DIM:] = o[tq:].astype(o_ref.dtype)


def _dense_attention(n_arr, p_arr, *, seq_len, tq=512, tk=512):
    tq, tk = min(tq, seq_len), min(tk, seq_len)
    kc = NORM_OFF["dk"] // HEAD_DIM
    vc = PLAIN_OFF["dv"] // HEAD_DIM
    return pl.pallas_call(
        functools.partial(_dense_kernel, seq_len=seq_len, tq=tq, tk=tk),
        out_shape=jax.ShapeDtypeStruct((seq_len, BRANCH_WIDTH), BF16),
        grid=(2, seq_len // tq),
        in_specs=[pl.BlockSpec((tq, 2 * HEAD_DIM), lambda g, i: (i, g)),
                  pl.BlockSpec((seq_len, HEAD_DIM), lambda g, i: (0, kc + g)),
                  pl.BlockSpec((seq_len, HEAD_DIM), lambda g, i: (0, vc + g))],
        out_specs=pl.BlockSpec((tq, 2 * HEAD_DIM), lambda g, i: (i, g)),
        scratch_shapes=[pltpu.VMEM((2 * tq, HEAD_DIM), BF16), pltpu.VMEM((seq_len, 2 * HEAD_DIM), BF16),
                        pltpu.VMEM((2 * tq, 1), F32), pltpu.VMEM((2 * tq, 2 * HEAD_DIM), F32)],
        compiler_params=_params(("arbitrary", "arbitrary")),
        name="dense_attention",
    )(n_arr, n_arr, p_arr)


def _merge_kernel(ya_ref, yb_ref, c0_ref, c1_ref, c2_ref, l0_ref, l1_ref, l2_ref, yd_ref, z_ref, gate_ref,
                  x_ref, wb_ref, wo_ref, g_ref, *out_refs, final):
    l0, l1, l2 = l0_ref[...], l1_ref[...], l2_ref[...]
    mx = jnp.maximum(jnp.maximum(l0, l1), l2)
    e0, e1, e2 = jnp.exp(l0 - mx), jnp.exp(l1 - mx), jnp.exp(l2 - mx)
    yc = (e0 * c0_ref[...].astype(F32) + e1 * c1_ref[...].astype(F32) + e2 * c2_ref[...].astype(F32)) / (e0 + e1 + e2)
    ys = (ya_ref[...].astype(F32), yb_ref[...].astype(F32), yc, yd_ref[...].astype(F32))
    merged = None
    for n in range(N_BRANCH):
        z = z_ref[:, n * BRANCH_WIDTH:(n + 1) * BRANCH_WIDTH].astype(F32)
        t = (ys[n] * (z * _sigmoid(z))).astype(BF16)
        br = jnp.dot(t, wb_ref[n], preferred_element_type=F32)
        term = gate_ref[:, n * D_MODEL:(n + 1) * D_MODEL].astype(F32) * br
        merged = term if merged is None else merged + term
    x_new = x_ref[...] + jnp.dot(merged.astype(BF16), wo_ref[...], preferred_element_type=F32)
    normed = x_new * lax.rsqrt(jnp.mean(x_new * x_new, axis=-1, keepdims=True) + NORM_EPS) * g_ref[...]
    if final:
        out_refs[0][...] = normed
    else:
        out_refs[0][...] = x_new
        out_refs[1][...] = normed.astype(BF16)


def _merge(ya, yb, cs, ls, yd, p_arr, gates, x, wb, wo, g_next, *, final, tm=256):
    s = x.shape[0]
    row = lambda w: pl.BlockSpec((tm, w), lambda i: (i, 0))
    once = dict(pipeline_mode=pl.Buffered(1))
    in_specs = ([row(BRANCH_WIDTH)] * 2 + [row(BRANCH_WIDTH)] * 3 + [row(BRANCH_WIDTH)] * 3 + [row(BRANCH_WIDTH)]
                + [row(N_BRANCH * BRANCH_WIDTH), row(N_BRANCH * D_MODEL), row(D_MODEL),
                   pl.BlockSpec((N_BRANCH, BRANCH_WIDTH, D_MODEL), lambda i: (0, 0, 0), **once),
                   pl.BlockSpec((D_MODEL, D_MODEL), lambda i: (0, 0), **once),
                   pl.BlockSpec((1, D_MODEL), lambda i: (0, 0))])
    if final:
        out_shape = [jax.ShapeDtypeStruct((s, D_MODEL), F32)]
    else:
        out_shape = [jax.ShapeDtypeStruct((s, D_MODEL), F32), jax.ShapeDtypeStruct((s, D_MODEL), BF16)]
    return pl.pallas_call(
        functools.partial(_merge_kernel, final=final),
        out_shape=out_shape,
        grid=(s // tm,),
        in_specs=in_specs,
        out_specs=[row(D_MODEL)] * len(out_shape),
        compiler_params=_params(("arbitrary",)),
        name="merge",
    )(ya, yb, *cs, *ls, yd, p_arr, gates, x, wb, wo, g_next.reshape(1, D_MODEL))


def _rope_tables(seq_len):
    t = jnp.arange(seq_len)

    def angles(pos, dim):
        inv_freq = ROPE_THETA ** (-jnp.arange(dim // 2, dtype=F32) / (dim // 2))
        return pos.astype(F32)[:, None] * inv_freq[None, :]

    a1 = angles(t, HEAD_DIM)
    cos1 = jnp.concatenate([jnp.cos(a1), jnp.cos(a1)], axis=1)
    sin1 = jnp.concatenate([-jnp.sin(a1), jnp.sin(a1)], axis=1)
    ar = angles(t // GRID_W, HEAD_DIM // 2)
    ac = angles(t % GRID_W, HEAD_DIM // 2)
    zero = jnp.zeros_like(ar)
    cos2 = jnp.concatenate([jnp.cos(ar), jnp.cos(ar), jnp.cos(ac), jnp.cos(ac)], axis=1)
    sin_lo = jnp.concatenate([-jnp.sin(ar), zero, -jnp.sin(ac), zero], axis=1)
    sin_hi = jnp.concatenate([zero, jnp.sin(ar), zero, jnp.sin(ac)], axis=1)
    return (cos1, sin1), (cos2, sin_lo, sin_hi)


def _gather_cols(w, seg):
    return jnp.concatenate([w[:, _OFF[n]:_OFF[n] + _WID[n]] for n in seg], axis=1).astype(BF16)


def _col_scale(seg, scaled):
    return jnp.asarray(np.concatenate(
        [np.full((_WID[n],), SCALE if n in scaled else 1.0, np.float32) for n in seg]).reshape(1, -1))


def _layer(x, h, w_in, gate_b, a_sink, b_rpb, d_q_norm, d_k_norm, w_branch, w_out, g_next, tables, final):
    s = x.shape[0]
    (cos1, sin1), (cos2, sin_lo, sin_hi) = tables
    tm = min(1024, s)

    r_arr = _proj(h, _gather_cols(w_in, ROPE_SEG), "rope", [_col_scale(ROPE_SEG, ("aq", "cq"))], [cos1, sin1],
                  tm=min(512, s), tn=ROPE_W // 2, name="proj_rope")
    gains = jnp.concatenate([jnp.tile(d_q_norm, _WID["dq"] // HEAD_DIM), jnp.tile(d_k_norm, _WID["dk"] // HEAD_DIM)])
    n_arr = _proj(h, _gather_cols(w_in, NORM_SEG), "norm_rope", [_col_scale(NORM_SEG, ("dq",)), gains.reshape(1, -1)],
                  [cos2, sin_lo, sin_hi], tm=tm, tn=NORM_W, name="proj_norm_rope")
    p_arr = _proj(h, _gather_cols(w_in, PLAIN_SEG), "scale", [_col_scale(PLAIN_SEG, ("bq",))], [],
                  tm=tm, tn=512, name="proj_plain")
    gates = _proj(h, _gather_cols(w_in, ("gates",)), "sigmoid", [gate_b.reshape(1, -1)], [],
                  tm=tm, tn=512, name="proj_gates")

    blk = lambda seg_off, name: seg_off[name] // HEAD_DIM
    (ya,) = _banded_attention(
        r_arr, r_arr, p_arr,
        q_col=lambda c: blk(ROPE_OFF, "aq") // 2 + c, k_col=lambda c: blk(ROPE_OFF, "ak") + c,
        v_col=lambda c: blk(PLAIN_OFF, "av") + c,
        seq_len=s, n_cols=2, o_cols=2, half_window=A_HALF_WINDOW, groups=2, sink=a_sink, want_lse=False,
        name="window_attention")

    yb = _nbr_attention(p_arr, _nbr_bias(b_rpb), seq_len=s)

    cs, ls = [], []
    rw, pw = ROPE_W // HEAD_DIM, PLAIN_W // HEAD_DIM
    for gi, (window, dil) in enumerate(C_PATTERNS):
        sub = s // dil
        qb, kb, vb = blk(ROPE_OFF, "cq") + 4 * gi, blk(ROPE_OFF, "ck"), blk(PLAIN_OFF, "cv")
        o, lse = _banded_attention(
            r_arr.reshape(sub, dil * ROPE_W), r_arr.reshape(sub, dil * ROPE_W), p_arr.reshape(sub, dil * PLAIN_W),
            q_col=lambda c, qb=qb: (c // 4) * rw + qb + c % 4, k_col=lambda c, kb=kb: (c // 4) * rw + kb + c % 4,
            v_col=lambda c, vb=vb: (c // 4) * pw + vb + c % 4,
            seq_len=sub, n_cols=4 * dil, o_cols=4 * dil, half_window=window // (2 * dil), groups=1, sink=None,
            want_lse=True, name=f"dilated_attention_{dil}")
        cs.append(o.reshape(s, BRANCH_WIDTH))
        ls.append(lse.reshape(s, BRANCH_WIDTH))

    yd = _dense_attention(n_arr, p_arr, seq_len=s)

    return _merge(ya, yb, cs, ls, yd, p_arr, gates, x, w_branch.astype(BF16), w_out.astype(BF16), g_next,
                  final=final)


def kernel(x, norm_g, w_in, gate_b, a_sink, b_rpb, d_q_norm, d_k_norm, w_branch, w_out, final_norm_g):
    b, s, d = x.shape
    depth = w_in.shape[0]
    tables = _rope_tables(s)
    outs = []
    for bi in range(b):
        xb = x[bi]
        h = _rmsnorm(xb, norm_g[0], BF16)
        for layer in range(depth):
            final = layer == depth - 1
            g_next = final_norm_g if final else norm_g[layer + 1]
            res = _layer(xb, h, w_in[layer], gate_b[layer], a_sink[layer], b_rpb[layer], d_q_norm[layer],
                         d_k_norm[layer], w_branch[layer], w_out[layer], g_next, tables, final)
            if final:
                xb = res[0]
            else:
                xb, h = res
        outs.append(xb)
    return jnp.stack(outs, axis=0)
```

```python
import functools
import math

import numpy as np
import jax
import jax.numpy as jnp
from jax import lax
from jax.experimental import pallas as pl
from jax.experimental.pallas import tpu as pltpu

F32 = jnp.float32
BF16 = jnp.bfloat16

D_MODEL = 2048
HEAD_DIM = 128
BRANCH_WIDTH = 512
N_BRANCH = 4
A_HALF_WINDOW = 128
B_WIN_ROWS = 8
B_WIN_COLS = 16
C_PATTERNS = ((128, 1), (512, 4), (2048, 16))
GRID_W = 64
ROPE_THETA = 10000.0
NORM_EPS = 1e-6
SCALE = HEAD_DIM ** -0.5
NEG_BIG = -1e30

LANE = 128
VMEM_LIMIT = 56 * 1024 * 1024

_SPLITS = (512, 256, 256, 512, 512, 512, 512, 512, 1536, 512, 512, 512, 512, 256, 256, 512, 8192)
_NAMES = ("aq", "ak", "av", "az", "bq", "bk", "bv", "bz", "cq", "ck", "cv", "cz", "dq", "dk", "dv", "dz", "gates")
_OFF = dict(zip(_NAMES, np.concatenate([[0], np.cumsum(_SPLITS)[:-1]]).tolist()))
_WID = dict(zip(_NAMES, _SPLITS))

ROPE_SEG = ("aq", "ak", "cq", "ck")
NORM_SEG = ("dq", "dk")
PLAIN_SEG = ("az", "bz", "cz", "dz", "av", "dv", "bq", "bk", "bv", "cv")


def _seg_offsets(seg):
    offs, o = {}, 0
    for n in seg:
        offs[n] = o
        o += _WID[n]
    return offs, o


ROPE_OFF, ROPE_W = _seg_offsets(ROPE_SEG)
NORM_OFF, NORM_W = _seg_offsets(NORM_SEG)
PLAIN_OFF, PLAIN_W = _seg_offsets(PLAIN_SEG)


def _params(sem):
    return pltpu.CompilerParams(dimension_semantics=sem, vmem_limit_bytes=VMEM_LIMIT)


def _sigmoid(v):
    return 1.0 / (1.0 + jnp.exp(-v))


def _rmsnorm_kernel(x_ref, g_ref, o_ref):
    x = x_ref[...]
    y = x * lax.rsqrt(jnp.mean(x * x, axis=-1, keepdims=True) + NORM_EPS)
    o_ref[...] = (y * g_ref[...]).astype(o_ref.dtype)


def _rmsnorm(x, g, out_dtype, tm=512):
    s, d = x.shape
    return pl.pallas_call(
        _rmsnorm_kernel,
        out_shape=jax.ShapeDtypeStruct((s, d), out_dtype),
        grid=(s // tm,),
        in_specs=[pl.BlockSpec((tm, d), lambda i: (i, 0)), pl.BlockSpec((1, d), lambda i: (0, 0))],
        out_specs=pl.BlockSpec((tm, d), lambda i: (i, 0)),
        compiler_params=_params(("arbitrary",)),
        name="rmsnorm",
    )(x, g.reshape(1, d))


def _rope_1d(xh, cos, sin_signed):
    return xh * cos + pltpu.roll(xh, HEAD_DIM // 2, 1) * sin_signed


def _rope_axial(xh, cos, sin_lo, sin_hi):
    return xh * cos + pltpu.roll(xh, HEAD_DIM - 32, 1) * sin_lo + pltpu.roll(xh, 32, 1) * sin_hi


def _proj_kernel(*refs, mode):
    h_ref, w_ref = refs[0], refs[1]
    o_ref = refs[-1]
    acc = jnp.dot(h_ref[...], w_ref[...], preferred_element_type=F32)
    tn = acc.shape[1]
    if mode == "scale":
        o_ref[...] = (acc * refs[2][...]).astype(o_ref.dtype)
    elif mode == "sigmoid":
        o_ref[...] = _sigmoid(acc + refs[2][...]).astype(o_ref.dtype)
    elif mode == "rope":
        cs_ref, cos_ref, sin_ref = refs[2:5]
        cos, sin = cos_ref[...], sin_ref[...]
        for c in range(tn // HEAD_DIM):
            sl = slice(c * HEAD_DIM, (c + 1) * HEAD_DIM)
            o_ref[:, sl] = (_rope_1d(acc[:, sl], cos, sin) * cs_ref[:, sl]).astype(o_ref.dtype)
    elif mode == "norm_rope":
        cs_ref, g_ref, cos_ref, slo_ref, shi_ref = refs[2:7]
        cos, slo, shi = cos_ref[...], slo_ref[...], shi_ref[...]
        for c in range(tn // HEAD_DIM):
            sl = slice(c * HEAD_DIM, (c + 1) * HEAD_DIM)
            xh = acc[:, sl]
            y = xh * lax.rsqrt(jnp.mean(xh * xh, axis=-1, keepdims=True) + NORM_EPS) * g_ref[:, sl]
            o_ref[:, sl] = (_rope_axial(y, cos, slo, shi) * cs_ref[:, sl]).astype(o_ref.dtype)
    else:
        raise ValueError(mode)


def _proj(h, w, mode, row_vecs, tables, tm, tn, name):
    s, d = h.shape
    n = w.shape[1]
    in_specs = [pl.BlockSpec((tm, d), lambda i, j: (i, 0)), pl.BlockSpec((d, tn), lambda i, j: (0, j))]
    in_specs += [pl.BlockSpec((1, tn), lambda i, j: (0, j)) for _ in row_vecs]
    in_specs += [pl.BlockSpec((tm, HEAD_DIM), lambda i, j: (i, 0)) for _ in tables]
    return pl.pallas_call(
        functools.partial(_proj_kernel, mode=mode),
        out_shape=jax.ShapeDtypeStruct((s, n), BF16),
        grid=(s // tm, n // tn),
        in_specs=in_specs,
        out_specs=pl.BlockSpec((tm, tn), lambda i, j: (i, j)),
        compiler_params=_params(("arbitrary", "arbitrary")),
        name=name,
    )(h, w, *row_vecs, *tables)


WIN_BQ = 128


def _window_kernel(sink_ref, q_ref, k_ref, v_ref, o_ref, *, seq_len, nsub):
    kv_head = pl.program_id(0)
    step = pl.program_id(1)
    bq, hw, groups = WIN_BQ, A_HALF_WINDOW, 2
    span = bq + 2 * hw
    rows = groups * bq

    def body(b, carry):
        q0 = (step * nsub + b) * bq
        start = pl.multiple_of(jnp.clip(q0 - hw, 0, seq_len - span), bq)
        k = k_ref[pl.ds(start, span), :]
        v = v_ref[pl.ds(start, span), :]
        qoff = pl.multiple_of(b * bq, bq)
        qb = q_ref[pl.ds(qoff, bq), :]
        q2 = jnp.concatenate([qb[:, :HEAD_DIM], qb[:, HEAD_DIM:]], axis=0)
        s = lax.dot_general(q2, k, (((1,), (1,)), ((), ())), preferred_element_type=F32)
        row = lax.broadcasted_iota(jnp.int32, (rows, span), 0)
        qpos = q0 + jnp.where(row >= bq, row - bq, row)
        kpos = start + lax.broadcasted_iota(jnp.int32, (rows, span), 1)
        s = jnp.where(jnp.abs(kpos - qpos) <= hw, s, NEG_BIG)
        r1 = lax.broadcasted_iota(jnp.int32, (rows, 1), 0)
        sink = jnp.where(r1 >= bq, sink_ref[kv_head * groups + 1], sink_ref[kv_head * groups])
        m = jnp.maximum(jnp.max(s, axis=1, keepdims=True), sink)
        p = jnp.exp(s - m)
        denom = jnp.sum(p, axis=1, keepdims=True) + jnp.exp(sink - m)
        o = jnp.dot(p.astype(BF16), v, preferred_element_type=F32) / denom
        o_ref[pl.ds(qoff, bq), :HEAD_DIM] = o[:bq].astype(o_ref.dtype)
        o_ref[pl.ds(qoff, bq), HEAD_DIM:] = o[bq:].astype(o_ref.dtype)
        return carry

    lax.fori_loop(0, nsub, body, 0, unroll=4)


def _window_attention(r_arr, p_arr, sink, *, seq_len, max_step=1024):
    nsub = min(max_step, seq_len) // WIN_BQ
    tq = nsub * WIN_BQ
    qc, kc, vc = ROPE_OFF["aq"] // (2 * HEAD_DIM), ROPE_OFF["ak"] // HEAD_DIM, PLAIN_OFF["av"] // HEAD_DIM
    return pl.pallas_call(
        functools.partial(_window_kernel, seq_len=seq_len, nsub=nsub),
        out_shape=jax.ShapeDtypeStruct((seq_len, BRANCH_WIDTH), BF16),
        grid=(2, seq_len // tq),
        in_specs=[pl.BlockSpec(memory_space=pltpu.SMEM),
                  pl.BlockSpec((tq, 2 * HEAD_DIM), lambda c, i: (i, qc + c)),
                  pl.BlockSpec((seq_len, HEAD_DIM), lambda c, i: (0, kc + c)),
                  pl.BlockSpec((seq_len, HEAD_DIM), lambda c, i: (0, vc + c))],
        out_specs=pl.BlockSpec((tq, 2 * HEAD_DIM), lambda c, i: (i, c)),
        compiler_params=_params(("arbitrary", "arbitrary")),
        name="window_attention",
    )(sink, r_arr, r_arr, p_arr)


DIL_BQ = 128
DIL_HALF = 64
DIL_HALO = DIL_HALF * max(d for _, d in C_PATTERNS)


def _dilated_kernel(q0_ref, q1_ref, q2_ref, kp_ref, kc_ref, kn_ref, vp_ref, vc_ref, vn_ref, o_ref,
                    qf_ref, kw_ref, vw_ref, og_ref, lg_ref, *, seq_len, tq):
    t0 = pl.program_id(1) * tq
    halo, bq, half = DIL_HALO, DIL_BQ, DIL_HALF
    span = bq + 2 * half
    for w_ref, prev, cur, nxt in ((kw_ref, kp_ref, kc_ref, kn_ref), (vw_ref, vp_ref, vc_ref, vn_ref)):
        w_ref[:halo] = prev[...].astype(F32)
        w_ref[halo:halo + tq] = cur[...].astype(F32)
        w_ref[halo + tq:] = nxt[...].astype(F32)
    for g, q_ref in enumerate((q0_ref, q1_ref, q2_ref)):
        qf_ref[g] = q_ref[...].astype(F32)

    ii = lax.broadcasted_iota(jnp.int32, (bq, span), 0)
    jj = lax.broadcasted_iota(jnp.int32, (bq, span), 1)
    in_band = jnp.abs(jj - half - ii) <= half

    for g, (_, dil) in enumerate(C_PATTERNS):
        nb = tq // (bq * dil)

        def body(it, carry, g=g, dil=dil, nb=nb):
            q_row = it // nb + dil * bq * (it % nb)
            w_row = halo + q_row - half * dil
            q = qf_ref[g, pl.ds(q_row, bq, stride=dil), :].astype(BF16)
            k = kw_ref[pl.ds(w_row, span, stride=dil), :].astype(BF16)
            v = vw_ref[pl.ds(w_row, span, stride=dil), :].astype(BF16)
            s = lax.dot_general(q, k, (((1,), (1,)), ((), ())), preferred_element_type=F32)
            ktok = t0 + q_row - half * dil + dil * jj
            s = jnp.where(in_band & (ktok >= 0) & (ktok < seq_len), s, NEG_BIG)
            m = jnp.max(s, axis=1, keepdims=True)
            p = jnp.exp(s - m)
            denom = jnp.sum(p, axis=1, keepdims=True)
            o = jnp.dot(p.astype(BF16), v, preferred_element_type=F32) / denom
            og_ref[g, pl.ds(q_row, bq, stride=dil), :] = o
            lg_ref[g, pl.ds(q_row, bq, stride=dil), :] = jnp.broadcast_to(m + jnp.log(denom), (bq, HEAD_DIM))
            return carry

        lax.fori_loop(0, tq // bq, body, 0, unroll=8)

    l0, l1, l2 = lg_ref[0], lg_ref[1], lg_ref[2]
    mx = jnp.maximum(jnp.maximum(l0, l1), l2)
    e0, e1, e2 = jnp.exp(l0 - mx), jnp.exp(l1 - mx), jnp.exp(l2 - mx)
    o_ref[...] = ((e0 * og_ref[0] + e1 * og_ref[1] + e2 * og_ref[2]) / (e0 + e1 + e2)).astype(o_ref.dtype)


def _dilated_attention(r_arr, p_arr, *, seq_len, tq=2048):
    tq = min(tq, seq_len)
    halo = DIL_HALO
    ratio, n_halo = tq // halo, seq_len // halo
    qc, kc, vc = ROPE_OFF["cq"] // HEAD_DIM, ROPE_OFF["ck"] // HEAD_DIM, PLAIN_OFF["cv"] // HEAD_DIM
    n_heads = BRANCH_WIDTH // HEAD_DIM

    def window_specs(col):
        return [pl.BlockSpec((halo, HEAD_DIM), lambda h, i: (jnp.maximum(i * ratio - 1, 0), col + h)),
                pl.BlockSpec((tq, HEAD_DIM), lambda h, i: (i, col + h)),
                pl.BlockSpec((halo, HEAD_DIM), lambda h, i: (jnp.minimum((i + 1) * ratio, n_halo - 1), col + h))]

    q_specs = [pl.BlockSpec((tq, HEAD_DIM), lambda h, i, g=g: (i, qc + n_heads * g + h)) for g in range(3)]
    return pl.pallas_call(
        functools.partial(_dilated_kernel, seq_len=seq_len, tq=tq),
        out_shape=jax.ShapeDtypeStruct((seq_len, BRANCH_WIDTH), BF16),
        grid=(n_heads, seq_len // tq),
        in_specs=q_specs + window_specs(kc) + window_specs(vc),
        out_specs=pl.BlockSpec((tq, HEAD_DIM), lambda h, i: (i, h)),
        scratch_shapes=[pltpu.VMEM((3, tq, HEAD_DIM), F32), pltpu.VMEM((tq + 2 * halo, HEAD_DIM), F32),
                        pltpu.VMEM((tq + 2 * halo, HEAD_DIM), F32), pltpu.VMEM((3, tq, HEAD_DIM), F32),
                        pltpu.VMEM((3, tq, HEAD_DIM), F32)],
        compiler_params=_params(("arbitrary", "arbitrary")),
        name="dilated_attention",
    )(r_arr, r_arr, r_arr, r_arr, r_arr, r_arr, p_arr, p_arr, p_arr)


NBR_ROWS = 4
NBR_UNION = 12
NBR_SUB = 4


def _nbr_bias_kernel(rpb_ref, o_ref, t_ref):
    h = pl.program_id(0)
    n_dr, n_dc = 2 * B_WIN_ROWS - 1, 2 * B_WIN_COLS - 1
    qc = lax.broadcasted_iota(jnp.int32, (GRID_W, GRID_W), 0)
    kc = lax.broadcasted_iota(jnp.int32, (GRID_W, GRID_W), 1)
    col_start = jnp.clip(qc - B_WIN_COLS // 2, 0, GRID_W - B_WIN_COLS)
    col_ok = (kc >= col_start) & (kc < col_start + B_WIN_COLS)
    dc = kc - qc + (B_WIN_COLS - 1)
    for dr in range(n_dr):
        t = jnp.full((GRID_W, GRID_W), NEG_BIG, F32)
        for d in range(n_dc):
            t = jnp.where(dc == d, rpb_ref[(h * n_dr + dr) * n_dc + d], t)
        t_ref[dr] = jnp.where(col_ok, t, NEG_BIG)
    o_ref[...] = jnp.full(o_ref.shape, NEG_BIG, F32)
    half = B_WIN_ROWS // 2
    for case in range(3):
        for a in range(NBR_ROWS):
            win, off = ((0, a), (a, half), (NBR_UNION - B_WIN_ROWS, half + a))[case]
            for j in range(B_WIN_ROWS):
                o_ref[case, a * GRID_W:(a + 1) * GRID_W, (win + j) * GRID_W:(win + j + 1) * GRID_W] = (
                    t_ref[j - off + B_WIN_ROWS - 1])


def _nbr_bias(rpb):
    n_heads = rpb.shape[0]
    shape = (3, NBR_ROWS * GRID_W, NBR_UNION * GRID_W)
    return pl.pallas_call(
        _nbr_bias_kernel,
        out_shape=jax.ShapeDtypeStruct((n_heads,) + shape, F32),
        grid=(n_heads,),
        in_specs=[pl.BlockSpec(memory_space=pltpu.SMEM)],
        out_specs=pl.BlockSpec((None,) + shape, lambda h: (h, 0, 0, 0)),
        scratch_shapes=[pltpu.VMEM((2 * B_WIN_ROWS - 1, GRID_W, GRID_W), F32)],
        compiler_params=_params(("arbitrary",)),
        name="nbr_bias",
    )(rpb.reshape(-1))


def _nbr_kernel(q_ref, k_ref, v_ref, b_first_ref, b_mid_ref, b_last_ref, o_ref, *, n_rows):
    step = pl.program_id(1)
    nq, nk = NBR_ROWS * GRID_W, NBR_UNION * GRID_W
    for sb in range(NBR_SUB):
        bias_ref = b_first_ref if sb == 0 else (b_last_ref if sb == NBR_SUB - 1 else b_mid_ref)
        r0 = (step * NBR_SUB + sb) * NBR_ROWS
        u0 = jnp.clip(r0 - B_WIN_ROWS // 2, 0, n_rows - NBR_UNION)
        kstart = pl.multiple_of(u0 * GRID_W, GRID_W)
        k = k_ref[pl.ds(kstart, nk), :]
        v = v_ref[pl.ds(kstart, nk), :]
        q = q_ref[sb * nq:(sb + 1) * nq, :]
        s = lax.dot_general(q, k, (((1,), (1,)), ((), ())), preferred_element_type=F32) + bias_ref[...]
        m = jnp.max(s, axis=1, keepdims=True)
        p = jnp.exp(s - m)
        denom = jnp.sum(p, axis=1, keepdims=True)
        o = jnp.dot(p.astype(BF16), v, preferred_element_type=F32) / denom
        o_ref[sb * nq:(sb + 1) * nq, :] = o.astype(o_ref.dtype)


def _nbr_attention(p_arr, bias, *, seq_len):
    n_rows = seq_len // GRID_W
    tq = NBR_SUB * NBR_ROWS * GRID_W
    n_steps = seq_len // tq
    qc, kc, vc = (PLAIN_OFF[n] // HEAD_DIM for n in ("bq", "bk", "bv"))
    tile = (None, None, NBR_ROWS * GRID_W, NBR_UNION * GRID_W)
    return pl.pallas_call(
        functools.partial(_nbr_kernel, n_rows=n_rows),
        out_shape=jax.ShapeDtypeStruct((seq_len, BRANCH_WIDTH), BF16),
        grid=(BRANCH_WIDTH // HEAD_DIM, n_steps),
        in_specs=[pl.BlockSpec((tq, HEAD_DIM), lambda h, i: (i, qc + h)),
                  pl.BlockSpec((seq_len, HEAD_DIM), lambda h, i: (0, kc + h)),
                  pl.BlockSpec((seq_len, HEAD_DIM), lambda h, i: (0, vc + h)),
                  pl.BlockSpec(tile, lambda h, i: (h, jnp.where(i == 0, 0, 1), 0, 0)),
                  pl.BlockSpec(tile, lambda h, i: (h, 1, 0, 0)),
                  pl.BlockSpec(tile, lambda h, i: (h, jnp.where(i == n_steps - 1, 2, 1), 0, 0))],
        out_specs=pl.BlockSpec((tq, HEAD_DIM), lambda h, i: (i, h)),
        compiler_params=_params(("arbitrary", "arbitrary")),
        name="nbr_attention",
    )(p_arr, p_arr, p_arr, bias, bias, bias)


DENSE_RB = 256


def _dense_kernel(q_ref, k_ref, v_ref, o_ref, q2_ref, v2_ref, m_ref, acc_ref, *, seq_len, tq, tk):
    @pl.when(pl.program_id(1) == 0)
    def _():
        v2_ref[:, :HEAD_DIM] = v_ref[...]
        v2_ref[:, HEAD_DIM:] = jnp.ones((seq_len, HEAD_DIM), BF16)

    q2_ref[:tq] = q_ref[:, :HEAD_DIM]
    q2_ref[tq:] = q_ref[:, HEAD_DIM:]
    m_ref[...] = jnp.full(m_ref.shape, NEG_BIG, F32)
    acc_ref[...] = jnp.zeros(acc_ref.shape, F32)

    def body(j, carry):
        koff = pl.multiple_of(j * tk, tk)
        k = k_ref[pl.ds(koff, tk), :]
        v2 = v2_ref[pl.ds(koff, tk), :]
        for r in range((2 * tq) // DENSE_RB):
            rows = pl.ds(r * DENSE_RB, DENSE_RB)
            s = lax.dot_general(q2_ref[rows, :], k, (((1,), (1,)), ((), ())), preferred_element_type=F32)
            m_prev = m_ref[rows, :]
            m_new = jnp.maximum(m_prev, jnp.max(s, axis=1, keepdims=True))
            alpha = jnp.exp2(m_prev - m_new)
            p = jnp.exp2(s - m_new).astype(BF16)
            acc_ref[rows, :] = acc_ref[rows, :] * alpha + jnp.dot(p, v2, preferred_element_type=F32)
            m_ref[rows, :] = m_new
        return carry

    lax.fori_loop(0, seq_len // tk, body, 0)
    acc = acc_ref[...]
    o = acc[:, :HEAD_DIM] / acc[:, HEAD_DIM:]
    o_ref[:, :HEAD_DIM] = o[:tq].astype(o_ref.dtype)
    o_ref[:, HEAD_DIM:] = o[tq:].astype(o_ref.dtype)


def _dense_attention(n_arr, p_arr, *, seq_len, tq=2048, tk=1024):
    tq, tk = min(tq, seq_len), min(tk, seq_len)
    kc = NORM_OFF["dk"] // HEAD_DIM
    vc = PLAIN_OFF["dv"] // HEAD_DIM
    return pl.pallas_call(
        functools.partial(_dense_kernel, seq_len=seq_len, tq=tq, tk=tk),
        out_shape=jax.ShapeDtypeStruct((seq_len, BRANCH_WIDTH), BF16),
        grid=(2, seq_len // tq),
        in_specs=[pl.BlockSpec((tq, 2 * HEAD_DIM), lambda g, i: (i, g)),
                  pl.BlockSpec((seq_len, HEAD_DIM), lambda g, i: (0, kc + g)),
                  pl.BlockSpec((seq_len, HEAD_DIM), lambda g, i: (0, vc + g))],
        out_specs=pl.BlockSpec((tq, 2 * HEAD_DIM), lambda g, i: (i, g)),
        scratch_shapes=[pltpu.VMEM((2 * tq, HEAD_DIM), BF16), pltpu.VMEM((seq_len, 2 * HEAD_DIM), BF16),
                        pltpu.VMEM((2 * tq, 1), F32), pltpu.VMEM((2 * tq, 2 * HEAD_DIM), F32)],
        compiler_params=_params(("arbitrary", "arbitrary")),
        name="dense_attention",
    )(n_arr, n_arr, p_arr)


def _merge_kernel(ya_ref, yb_ref, yc_ref, yd_ref, z_ref, gate_ref, x_ref, wb_ref, wo_ref, g_ref, *out_refs, final):
    merged = None
    for n, y_ref in enumerate((ya_ref, yb_ref, yc_ref, yd_ref)):
        z = z_ref[:, n * BRANCH_WIDTH:(n + 1) * BRANCH_WIDTH].astype(F32)
        t = (y_ref[...].astype(F32) * (z * _sigmoid(z))).astype(BF16)
        br = jnp.dot(t, wb_ref[n], preferred_element_type=F32)
        term = gate_ref[:, n * D_MODEL:(n + 1) * D_MODEL].astype(F32) * br
        merged = term if merged is None else merged + term
    x_new = x_ref[...] + jnp.dot(merged.astype(BF16), wo_ref[...], preferred_element_type=F32)
    normed = x_new * lax.rsqrt(jnp.mean(x_new * x_new, axis=-1, keepdims=True) + NORM_EPS) * g_ref[...]
    if final:
        out_refs[0][...] = normed
    else:
        out_refs[0][...] = x_new
        out_refs[1][...] = normed.astype(BF16)


def _merge(ys, p_arr, gates, x, wb, wo, g_next, *, final, tm=256):
    s = x.shape[0]
    row = lambda w: pl.BlockSpec((tm, w), lambda i: (i, 0))
    once = dict(pipeline_mode=pl.Buffered(1))
    in_specs = ([row(BRANCH_WIDTH)] * N_BRANCH
                + [row(N_BRANCH * BRANCH_WIDTH), row(N_BRANCH * D_MODEL), row(D_MODEL),
                   pl.BlockSpec((N_BRANCH, BRANCH_WIDTH, D_MODEL), lambda i: (0, 0, 0), **once),
                   pl.BlockSpec((D_MODEL, D_MODEL), lambda i: (0, 0), **once),
                   pl.BlockSpec((1, D_MODEL), lambda i: (0, 0))])
    if final:
        out_shape = [jax.ShapeDtypeStruct((s, D_MODEL), F32)]
    else:
        out_shape = [jax.ShapeDtypeStruct((s, D_MODEL), F32), jax.ShapeDtypeStruct((s, D_MODEL), BF16)]
    return pl.pallas_call(
        functools.partial(_merge_kernel, final=final),
        out_shape=out_shape,
        grid=(s // tm,),
        in_specs=in_specs,
        out_specs=[row(D_MODEL)] * len(out_shape),
        compiler_params=_params(("arbitrary",)),
        name="merge",
    )(*ys, p_arr, gates, x, wb, wo, g_next.reshape(1, D_MODEL))


def _rope_tables(seq_len):
    t = jnp.arange(seq_len)

    def angles(pos, dim):
        inv_freq = ROPE_THETA ** (-jnp.arange(dim // 2, dtype=F32) / (dim // 2))
        return pos.astype(F32)[:, None] * inv_freq[None, :]

    a1 = angles(t, HEAD_DIM)
    cos1 = jnp.concatenate([jnp.cos(a1), jnp.cos(a1)], axis=1)
    sin1 = jnp.concatenate([-jnp.sin(a1), jnp.sin(a1)], axis=1)
    ar = angles(t // GRID_W, HEAD_DIM // 2)
    ac = angles(t % GRID_W, HEAD_DIM // 2)
    zero = jnp.zeros_like(ar)
    cos2 = jnp.concatenate([jnp.cos(ar), jnp.cos(ar), jnp.cos(ac), jnp.cos(ac)], axis=1)
    sin_lo = jnp.concatenate([-jnp.sin(ar), zero, -jnp.sin(ac), zero], axis=1)
    sin_hi = jnp.concatenate([zero, jnp.sin(ar), zero, jnp.sin(ac)], axis=1)
    return (cos1, sin1), (cos2, sin_lo, sin_hi)


def _gather_cols(w, seg):
    return jnp.concatenate([w[:, _OFF[n]:_OFF[n] + _WID[n]] for n in seg], axis=1).astype(BF16)


def _col_scale(seg, scaled):
    return jnp.asarray(np.concatenate(
        [np.full((_WID[n],), scaled.get(n, 1.0), np.float32) for n in seg]).reshape(1, -1))


def _layer(x, h, w_in, gate_b, a_sink, b_rpb, d_q_norm, d_k_norm, w_branch, w_out, g_next, tables, final):
    s = x.shape[0]
    (cos1, sin1), (cos2, sin_lo, sin_hi) = tables
    tm = min(1024, s)

    r_arr = _proj(h, _gather_cols(w_in, ROPE_SEG), "rope", [_col_scale(ROPE_SEG, {"aq": SCALE, "cq": SCALE})],
                  [cos1, sin1], tm=min(512, s), tn=ROPE_W // 2, name="proj_rope")
    gains = jnp.concatenate([jnp.tile(d_q_norm, _WID["dq"] // HEAD_DIM), jnp.tile(d_k_norm, _WID["dk"] // HEAD_DIM)])
    n_arr = _proj(h, _gather_cols(w_in, NORM_SEG), "norm_rope",
                  [_col_scale(NORM_SEG, {"dq": SCALE * math.log2(math.e)}), gains.reshape(1, -1)],
                  [cos2, sin_lo, sin_hi], tm=tm, tn=NORM_W, name="proj_norm_rope")
    p_arr = _proj(h, _gather_cols(w_in, PLAIN_SEG), "scale", [_col_scale(PLAIN_SEG, {"bq": SCALE})], [],
                  tm=tm, tn=512, name="proj_plain")
    gates = _proj(h, _gather_cols(w_in, ("gates",)), "sigmoid", [gate_b.reshape(1, -1)], [],
                  tm=tm, tn=512, name="proj_gates")

    ya = _window_attention(r_arr, p_arr, a_sink, seq_len=s)
    yb = _nbr_attention(p_arr, _nbr_bias(b_rpb), seq_len=s)
    yc = _dilated_attention(r_arr, p_arr, seq_len=s)
    yd = _dense_attention(n_arr, p_arr, seq_len=s)
    return _merge((ya, yb, yc, yd), p_arr, gates, x, w_branch.astype(BF16), w_out.astype(BF16), g_next, final=final)


def kernel(x, norm_g, w_in, gate_b, a_sink, b_rpb, d_q_norm, d_k_norm, w_branch, w_out, final_norm_g):
    b, s, d = x.shape
    depth = w_in.shape[0]
    tables = _rope_tables(s)
    outs = []
    for bi in range(b):
        xb = x[bi]
        h = _rmsnorm(xb, norm_g[0], BF16)
        for layer in range(depth):
            final = layer == depth - 1
            g_next = final_norm_g if final else norm_g[layer + 1]
            res = _layer(xb, h, w_in[layer], gate_b[layer], a_sink[layer], b_rpb[layer], d_q_norm[layer],
                         d_k_norm[layer], w_branch[layer], w_out[layer], g_next, tables, final)
            if final:
                xb = res[0]
            else:
                xb, h = res
        outs.append(xb)
    return jnp.stack(outs, axis=0)
```

```python
import functools
import math

import numpy as np
import jax
import jax.numpy as jnp
from jax import lax
from jax.experimental import pallas as pl
from jax.experimental.pallas import tpu as pltpu

F32 = jnp.float32
BF16 = jnp.bfloat16

D_MODEL = 2048
HEAD_DIM = 128
BRANCH_WIDTH = 512
N_BRANCH = 4
A_HALF_WINDOW = 128
B_WIN_ROWS = 8
B_WIN_COLS = 16
C_PATTERNS = ((128, 1), (512, 4), (2048, 16))
GRID_W = 64
ROPE_THETA = 10000.0
NORM_EPS = 1e-6
SCALE = HEAD_DIM ** -0.5
NEG_BIG = -1e30

LANE = 128
VMEM_LIMIT = 56 * 1024 * 1024

_SPLITS = (512, 256, 256, 512, 512, 512, 512, 512, 1536, 512, 512, 512, 512, 256, 256, 512, 8192)
_NAMES = ("aq", "ak", "av", "az", "bq", "bk", "bv", "bz", "cq", "ck", "cv", "cz", "dq", "dk", "dv", "dz", "gates")
_OFF = dict(zip(_NAMES, np.concatenate([[0], np.cumsum(_SPLITS)[:-1]]).tolist()))
_WID = dict(zip(_NAMES, _SPLITS))

ROPE_SEG = ("aq", "ak", "cq", "ck")
NORM_SEG = ("dq", "dk")
PLAIN_SEG = ("az", "bz", "cz", "dz", "av", "dv", "bq", "bk", "bv", "cv")


def _seg_offsets(seg):
    offs, o = {}, 0
    for n in seg:
        offs[n] = o
        o += _WID[n]
    return offs, o


ROPE_OFF, ROPE_W = _seg_offsets(ROPE_SEG)
NORM_OFF, NORM_W = _seg_offsets(NORM_SEG)
PLAIN_OFF, PLAIN_W = _seg_offsets(PLAIN_SEG)


def _params(sem):
    return pltpu.CompilerParams(dimension_semantics=sem, vmem_limit_bytes=VMEM_LIMIT)


def _sigmoid(v):
    return 1.0 / (1.0 + jnp.exp(-v))


def _rmsnorm_kernel(x_ref, g_ref, o_ref):
    x = x_ref[...]
    y = x * lax.rsqrt(jnp.mean(x * x, axis=-1, keepdims=True) + NORM_EPS)
    o_ref[...] = (y * g_ref[...]).astype(o_ref.dtype)


def _rmsnorm(x, g, out_dtype, tm=512):
    s, d = x.shape
    return pl.pallas_call(
        _rmsnorm_kernel,
        out_shape=jax.ShapeDtypeStruct((s, d), out_dtype),
        grid=(s // tm,),
        in_specs=[pl.BlockSpec((tm, d), lambda i: (i, 0)), pl.BlockSpec((1, d), lambda i: (0, 0))],
        out_specs=pl.BlockSpec((tm, d), lambda i: (i, 0)),
        compiler_params=_params(("arbitrary",)),
        name="rmsnorm",
    )(x, g.reshape(1, d))


PROJ_RB = 256


def _rope_half(xh, cos, sin_signed):
    return xh * cos + pltpu.roll(xh, HEAD_DIM // 2, 1) * sin_signed


def _proj_kernel(*refs, mode):
    h_ref, w_ref = refs[0], refs[1]
    o_ref = refs[-1]
    tm, tn = o_ref.shape
    for r in range(tm // PROJ_RB):
        rows = pl.ds(r * PROJ_RB, PROJ_RB)
        acc = jnp.dot(h_ref[rows, :], w_ref[...], preferred_element_type=F32)
        if mode == "scale":
            o_ref[rows, :] = (acc * refs[2][...]).astype(o_ref.dtype)
        elif mode == "sigmoid":
            o_ref[rows, :] = _sigmoid(acc + refs[2][...]).astype(o_ref.dtype)
        elif mode == "rope":
            cs_ref, cos_ref, sin_ref = refs[2:5]
            cos, sin = cos_ref[rows, :], sin_ref[rows, :]
            for c in range(tn // HEAD_DIM):
                sl = slice(c * HEAD_DIM, (c + 1) * HEAD_DIM)
                o_ref[rows, sl] = (_rope_half(acc[:, sl], cos, sin) * cs_ref[:, sl]).astype(o_ref.dtype)
        elif mode == "norm_rope":
            cs_ref, g_ref, cos_ref, sin_ref = refs[2:6]
            cos, sin = cos_ref[rows, :], sin_ref[rows, :]
            for c in range(tn // HEAD_DIM):
                sl = slice(c * HEAD_DIM, (c + 1) * HEAD_DIM)
                xh = acc[:, sl]
                y = xh * lax.rsqrt(jnp.mean(xh * xh, axis=-1, keepdims=True) + NORM_EPS) * g_ref[:, sl]
                o_ref[rows, sl] = (_rope_half(y, cos, sin) * cs_ref[:, sl]).astype(o_ref.dtype)
        else:
            raise ValueError(mode)


def _proj(h, w, mode, row_vecs, tables, tm, tn, name):
    s, d = h.shape
    n = w.shape[1]
    tm = min(tm, s)
    in_specs = [pl.BlockSpec((tm, d), lambda i, j: (i, 0)), pl.BlockSpec((d, tn), lambda i, j: (0, j))]
    in_specs += [pl.BlockSpec((1, tn), lambda i, j: (0, j)) for _ in row_vecs]
    in_specs += [pl.BlockSpec((tm, HEAD_DIM), lambda i, j: (i, 0)) for _ in tables]
    return pl.pallas_call(
        functools.partial(_proj_kernel, mode=mode),
        out_shape=jax.ShapeDtypeStruct((s, n), BF16),
        grid=(s // tm, n // tn),
        in_specs=in_specs,
        out_specs=pl.BlockSpec((tm, tn), lambda i, j: (i, j)),
        compiler_params=_params(("arbitrary", "arbitrary")),
        name=name,
    )(h, w, *row_vecs, *tables)


WIN_BQ = 128


def _window_kernel(sink_ref, q_ref, k_ref, v_ref, o_ref, *, seq_len, nsub):
    kv_head = pl.program_id(0)
    step = pl.program_id(1)
    bq, hw, groups = WIN_BQ, A_HALF_WINDOW, 2
    span = bq + 2 * hw
    rows = groups * bq

    def body(b, carry):
        q0 = (step * nsub + b) * bq
        start = pl.multiple_of(jnp.clip(q0 - hw, 0, seq_len - span), bq)
        k = k_ref[pl.ds(start, span), :]
        v = v_ref[pl.ds(start, span), :]
        qoff = pl.multiple_of(b * bq, bq)
        qb = q_ref[pl.ds(qoff, bq), :]
        q2 = jnp.concatenate([qb[:, :HEAD_DIM], qb[:, HEAD_DIM:]], axis=0)
        s = lax.dot_general(q2, k, (((1,), (1,)), ((), ())), preferred_element_type=F32)
        row = lax.broadcasted_iota(jnp.int32, (rows, span), 0)
        qpos = q0 + jnp.where(row >= bq, row - bq, row)
        kpos = start + lax.broadcasted_iota(jnp.int32, (rows, span), 1)
        s = jnp.where(jnp.abs(kpos - qpos) <= hw, s, NEG_BIG)
        r1 = lax.broadcasted_iota(jnp.int32, (rows, 1), 0)
        sink = jnp.where(r1 >= bq, sink_ref[kv_head * groups + 1], sink_ref[kv_head * groups])
        m = jnp.maximum(jnp.max(s, axis=1, keepdims=True), sink)
        p = jnp.exp(s - m)
        denom = jnp.sum(p, axis=1, keepdims=True) + jnp.exp(sink - m)
        o = jnp.dot(p.astype(BF16), v, preferred_element_type=F32) / denom
        o_ref[pl.ds(qoff, bq), :HEAD_DIM] = o[:bq].astype(o_ref.dtype)
        o_ref[pl.ds(qoff, bq), HEAD_DIM:] = o[bq:].astype(o_ref.dtype)
        return carry

    lax.fori_loop(0, nsub, body, 0, unroll=4)


def _window_attention(r_arr, p_arr, sink, *, seq_len, max_step=1024):
    nsub = min(max_step, seq_len) // WIN_BQ
    tq = nsub * WIN_BQ
    qc, kc, vc = ROPE_OFF["aq"] // (2 * HEAD_DIM), ROPE_OFF["ak"] // HEAD_DIM, PLAIN_OFF["av"] // HEAD_DIM
    return pl.pallas_call(
        functools.partial(_window_kernel, seq_len=seq_len, nsub=nsub),
        out_shape=jax.ShapeDtypeStruct((seq_len, BRANCH_WIDTH), BF16),
        grid=(2, seq_len // tq),
        in_specs=[pl.BlockSpec(memory_space=pltpu.SMEM),
                  pl.BlockSpec((tq, 2 * HEAD_DIM), lambda c, i: (i, qc + c)),
                  pl.BlockSpec((seq_len, HEAD_DIM), lambda c, i: (0, kc + c)),
                  pl.BlockSpec((seq_len, HEAD_DIM), lambda c, i: (0, vc + c))],
        out_specs=pl.BlockSpec((tq, 2 * HEAD_DIM), lambda c, i: (i, c)),
        compiler_params=_params(("arbitrary", "arbitrary")),
        name="window_attention",
    )(sink, r_arr, r_arr, p_arr)


DIL_BQ = 128
DIL_HALF = 64
DIL_HALO = DIL_HALF * max(d for _, d in C_PATTERNS)


def _dilated_kernel(q0_ref, q1_ref, q2_ref, kp_ref, kc_ref, kn_ref, vp_ref, vc_ref, vn_ref, o_ref,
                    qf_ref, kw_ref, vw_ref, og_ref, lg_ref, *, seq_len, tq):
    t0 = pl.program_id(1) * tq
    halo, bq, half = DIL_HALO, DIL_BQ, DIL_HALF
    span = bq + 2 * half
    for w_ref, prev, cur, nxt in ((kw_ref, kp_ref, kc_ref, kn_ref), (vw_ref, vp_ref, vc_ref, vn_ref)):
        w_ref[:halo] = prev[...].astype(F32)
        w_ref[halo:halo + tq] = cur[...].astype(F32)
        w_ref[halo + tq:] = nxt[...].astype(F32)
    for g, q_ref in enumerate((q0_ref, q1_ref, q2_ref)):
        qf_ref[g] = q_ref[...].astype(F32)

    ii = lax.broadcasted_iota(jnp.int32, (bq, span), 0)
    jj = lax.broadcasted_iota(jnp.int32, (bq, span), 1)
    in_band = jnp.abs(jj - half - ii) <= half

    for g, (_, dil) in enumerate(C_PATTERNS):
        nb = tq // (bq * dil)

        def body(it, carry, g=g, dil=dil, nb=nb):
            q_row = it // nb + dil * bq * (it % nb)
            w_row = halo + q_row - half * dil
            q = qf_ref[g, pl.ds(q_row, bq, stride=dil), :].astype(BF16)
            k = kw_ref[pl.ds(w_row, span, stride=dil), :].astype(BF16)
            v = vw_ref[pl.ds(w_row, span, stride=dil), :].astype(BF16)
            s = lax.dot_general(q, k, (((1,), (1,)), ((), ())), preferred_element_type=F32)
            ktok = t0 + q_row - half * dil + dil * jj
            s = jnp.where(in_band & (ktok >= 0) & (ktok < seq_len), s, NEG_BIG)
            m = jnp.max(s, axis=1, keepdims=True)
            p = jnp.exp(s - m)
            denom = jnp.sum(p, axis=1, keepdims=True)
            o = jnp.dot(p.astype(BF16), v, preferred_element_type=F32) / denom
            og_ref[g, pl.ds(q_row, bq, stride=dil), :] = o
            lg_ref[g, pl.ds(q_row, bq, stride=dil), :] = jnp.broadcast_to(m + jnp.log(denom), (bq, HEAD_DIM))
            return carry

        lax.fori_loop(0, tq // bq, body, 0, unroll=8)

    l0, l1, l2 = lg_ref[0], lg_ref[1], lg_ref[2]
    mx = jnp.maximum(jnp.maximum(l0, l1), l2)
    e0, e1, e2 = jnp.exp(l0 - mx), jnp.exp(l1 - mx), jnp.exp(l2 - mx)
    o_ref[...] = ((e0 * og_ref[0] + e1 * og_ref[1] + e2 * og_ref[2]) / (e0 + e1 + e2)).astype(o_ref.dtype)


def _dilated_attention(r_arr, p_arr, *, seq_len, tq=2048):
    tq = min(tq, seq_len)
    halo = DIL_HALO
    ratio, n_halo = tq // halo, seq_len // halo
    qc, kc, vc = ROPE_OFF["cq"] // HEAD_DIM, ROPE_OFF["ck"] // HEAD_DIM, PLAIN_OFF["cv"] // HEAD_DIM
    n_heads = BRANCH_WIDTH // HEAD_DIM

    def window_specs(col):
        return [pl.BlockSpec((halo, HEAD_DIM), lambda h, i: (jnp.maximum(i * ratio - 1, 0), col + h)),
                pl.BlockSpec((tq, HEAD_DIM), lambda h, i: (i, col + h)),
                pl.BlockSpec((halo, HEAD_DIM), lambda h, i: (jnp.minimum((i + 1) * ratio, n_halo - 1), col + h))]

    q_specs = [pl.BlockSpec((tq, HEAD_DIM), lambda h, i, g=g: (i, qc + n_heads * g + h)) for g in range(3)]
    return pl.pallas_call(
        functools.partial(_dilated_kernel, seq_len=seq_len, tq=tq),
        out_shape=jax.ShapeDtypeStruct((seq_len, BRANCH_WIDTH), BF16),
        grid=(n_heads, seq_len // tq),
        in_specs=q_specs + window_specs(kc) + window_specs(vc),
        out_specs=pl.BlockSpec((tq, HEAD_DIM), lambda h, i: (i, h)),
        scratch_shapes=[pltpu.VMEM((3, tq, HEAD_DIM), F32), pltpu.VMEM((tq + 2 * halo, HEAD_DIM), F32),
                        pltpu.VMEM((tq + 2 * halo, HEAD_DIM), F32), pltpu.VMEM((3, tq, HEAD_DIM), F32),
                        pltpu.VMEM((3, tq, HEAD_DIM), F32)],
        compiler_params=_params(("arbitrary", "arbitrary")),
        name="dilated_attention",
    )(r_arr, r_arr, r_arr, r_arr, r_arr, r_arr, p_arr, p_arr, p_arr)


NBR_ROWS = 4
NBR_UNION = 12
NBR_SUB = 4


def _nbr_bias_kernel(rpb_ref, o_ref, t_ref):
    h = pl.program_id(0)
    n_dr, n_dc = 2 * B_WIN_ROWS - 1, 2 * B_WIN_COLS - 1
    qc = lax.broadcasted_iota(jnp.int32, (GRID_W, GRID_W), 0)
    kc = lax.broadcasted_iota(jnp.int32, (GRID_W, GRID_W), 1)
    col_start = jnp.clip(qc - B_WIN_COLS // 2, 0, GRID_W - B_WIN_COLS)
    col_ok = (kc >= col_start) & (kc < col_start + B_WIN_COLS)
    dc = kc - qc + (B_WIN_COLS - 1)
    for dr in range(n_dr):
        t = jnp.full((GRID_W, GRID_W), NEG_BIG, F32)
        for d in range(n_dc):
            t = jnp.where(dc == d, rpb_ref[(h * n_dr + dr) * n_dc + d], t)
        t_ref[dr] = jnp.where(col_ok, t, NEG_BIG)
    o_ref[...] = jnp.full(o_ref.shape, NEG_BIG, F32)
    half = B_WIN_ROWS // 2
    for case in range(3):
        for a in range(NBR_ROWS):
            win, off = ((0, a), (a, half), (NBR_UNION - B_WIN_ROWS, half + a))[case]
            for j in range(B_WIN_ROWS):
                o_ref[case, a * GRID_W:(a + 1) * GRID_W, (win + j) * GRID_W:(win + j + 1) * GRID_W] = (
                    t_ref[j - off + B_WIN_ROWS - 1])


def _nbr_bias(rpb):
    n_heads = rpb.shape[0]
    shape = (3, NBR_ROWS * GRID_W, NBR_UNION * GRID_W)
    return pl.pallas_call(
        _nbr_bias_kernel,
        out_shape=jax.ShapeDtypeStruct((n_heads,) + shape, F32),
        grid=(n_heads,),
        in_specs=[pl.BlockSpec(memory_space=pltpu.SMEM)],
        out_specs=pl.BlockSpec((None,) + shape, lambda h: (h, 0, 0, 0)),
        scratch_shapes=[pltpu.VMEM((2 * B_WIN_ROWS - 1, GRID_W, GRID_W), F32)],
        compiler_params=_params(("arbitrary",)),
        name="nbr_bias",
    )(rpb.reshape(-1))


def _nbr_kernel(q_ref, k_ref, v_ref, b_first_ref, b_mid_ref, b_last_ref, o_ref, *, n_rows):
    step = pl.program_id(1)
    nq, nk = NBR_ROWS * GRID_W, NBR_UNION * GRID_W
    for sb in range(NBR_SUB):
        bias_ref = b_first_ref if sb == 0 else (b_last_ref if sb == NBR_SUB - 1 else b_mid_ref)
        r0 = (step * NBR_SUB + sb) * NBR_ROWS
        u0 = jnp.clip(r0 - B_WIN_ROWS // 2, 0, n_rows - NBR_UNION)
        kstart = pl.multiple_of(u0 * GRID_W, GRID_W)
        k = k_ref[pl.ds(kstart, nk), :]
        v = v_ref[pl.ds(kstart, nk), :]
        q = q_ref[sb * nq:(sb + 1) * nq, :]
        s = lax.dot_general(q, k, (((1,), (1,)), ((), ())), preferred_element_type=F32) + bias_ref[...]
        m = jnp.max(s, axis=1, keepdims=True)
        p = jnp.exp(s - m)
        denom = jnp.sum(p, axis=1, keepdims=True)
        o = jnp.dot(p.astype(BF16), v, preferred_element_type=F32) / denom
        o_ref[sb * nq:(sb + 1) * nq, :] = o.astype(o_ref.dtype)


def _nbr_attention(p_arr, bias, *, seq_len):
    n_rows = seq_len // GRID_W
    tq = NBR_SUB * NBR_ROWS * GRID_W
    n_steps = seq_len // tq
    qc, kc, vc = (PLAIN_OFF[n] // HEAD_DIM for n in ("bq", "bk", "bv"))
    tile = (None, None, NBR_ROWS * GRID_W, NBR_UNION * GRID_W)
    return pl.pallas_call(
        functools.partial(_nbr_kernel, n_rows=n_rows),
        out_shape=jax.ShapeDtypeStruct((seq_len, BRANCH_WIDTH), BF16),
        grid=(BRANCH_WIDTH // HEAD_DIM, n_steps),
        in_specs=[pl.BlockSpec((tq, HEAD_DIM), lambda h, i: (i, qc + h)),
                  pl.BlockSpec((seq_len, HEAD_DIM), lambda h, i: (0, kc + h)),
                  pl.BlockSpec((seq_len, HEAD_DIM), lambda h, i: (0, vc + h)),
                  pl.BlockSpec(tile, lambda h, i: (h, jnp.where(i == 0, 0, 1), 0, 0)),
                  pl.BlockSpec(tile, lambda h, i: (h, 1, 0, 0)),
                  pl.BlockSpec(tile, lambda h, i: (h, jnp.where(i == n_steps - 1, 2, 1), 0, 0))],
        out_specs=pl.BlockSpec((tq, HEAD_DIM), lambda h, i: (i, h)),
        compiler_params=_params(("arbitrary", "arbitrary")),
        name="nbr_attention",
    )(p_arr, p_arr, p_arr, bias, bias, bias)


DENSE_RB = 256


def _dense_kernel(q_ref, k_ref, v_ref, o_ref, q2_ref, v2_ref, m_ref, acc_ref, *, seq_len, tq, tk):
    @pl.when(pl.program_id(1) == 0)
    def _():
        v2_ref[:, :HEAD_DIM] = v_ref[...]
        v2_ref[:, HEAD_DIM:] = jnp.ones((seq_len, HEAD_DIM), BF16)

    q2_ref[:tq] = q_ref[:, :HEAD_DIM]
    q2_ref[tq:] = q_ref[:, HEAD_DIM:]
    m_ref[...] = jnp.full(m_ref.shape, NEG_BIG, F32)
    acc_ref[...] = jnp.zeros(acc_ref.shape, F32)

    def body(j, carry):
        koff = pl.multiple_of(j * tk, tk)
        k = k_ref[pl.ds(koff, tk), :]
        v2 = v2_ref[pl.ds(koff, tk), :]
        for r in range((2 * tq) // DENSE_RB):
            rows = pl.ds(r * DENSE_RB, DENSE_RB)
            s = lax.dot_general(q2_ref[rows, :], k, (((1,), (1,)), ((), ())), preferred_element_type=F32)
            m_prev = m_ref[rows, :]
            m_new = jnp.maximum(m_prev, jnp.max(s, axis=1, keepdims=True))
            alpha = jnp.exp2(m_prev - m_new)
            p = jnp.exp2(s - m_new).astype(BF16)
            acc_ref[rows, :] = acc_ref[rows, :] * alpha + jnp.dot(p, v2, preferred_element_type=F32)
            m_ref[rows, :] = m_new
        return carry

    lax.fori_loop(0, seq_len // tk, body, 0)
    acc = acc_ref[...]
    o = acc[:, :HEAD_DIM] / acc[:, HEAD_DIM:]
    o_ref[:, :HEAD_DIM] = o[:tq].astype(o_ref.dtype)
    o_ref[:, HEAD_DIM:] = o[tq:].astype(o_ref.dtype)


def _dense_attention(n_arr, p_arr, *, seq_len, tq=2048, tk=2048):
    tq, tk = min(tq, seq_len), min(tk, seq_len)
    kc = NORM_OFF["dk"] // HEAD_DIM
    vc = PLAIN_OFF["dv"] // HEAD_DIM
    return pl.pallas_call(
        functools.partial(_dense_kernel, seq_len=seq_len, tq=tq, tk=tk),
        out_shape=jax.ShapeDtypeStruct((seq_len, BRANCH_WIDTH), BF16),
        grid=(2, seq_len // tq),
        in_specs=[pl.BlockSpec((tq, 2 * HEAD_DIM), lambda g, i: (i, g)),
                  pl.BlockSpec((seq_len, HEAD_DIM), lambda g, i: (0, kc + g)),
                  pl.BlockSpec((seq_len, HEAD_DIM), lambda g, i: (0, vc + g))],
        out_specs=pl.BlockSpec((tq, 2 * HEAD_DIM), lambda g, i: (i, g)),
        scratch_shapes=[pltpu.VMEM((2 * tq, HEAD_DIM), BF16), pltpu.VMEM((seq_len, 2 * HEAD_DIM), BF16),
                        pltpu.VMEM((2 * tq, 1), F32), pltpu.VMEM((2 * tq, 2 * HEAD_DIM), F32)],
        compiler_params=_params(("arbitrary", "arbitrary")),
        name="dense_attention",
    )(n_arr, n_arr, p_arr)


def _merge_kernel(ya_ref, yb_ref, yc_ref, yd_ref, z_ref, gate_ref, x_ref, wb_ref, wo_ref, g_ref, *out_refs, final):
    merged = None
    for n, y_ref in enumerate((ya_ref, yb_ref, yc_ref, yd_ref)):
        z = z_ref[:, n * BRANCH_WIDTH:(n + 1) * BRANCH_WIDTH].astype(F32)
        t = (y_ref[...].astype(F32) * (z * _sigmoid(z))).astype(BF16)
        br = jnp.dot(t, wb_ref[n], preferred_element_type=F32)
        term = gate_ref[:, n * D_MODEL:(n + 1) * D_MODEL].astype(F32) * br
        merged = term if merged is None else merged + term
    x_new = x_ref[...] + jnp.dot(merged.astype(BF16), wo_ref[...], preferred_element_type=F32)
    normed = x_new * lax.rsqrt(jnp.mean(x_new * x_new, axis=-1, keepdims=True) + NORM_EPS) * g_ref[...]
    if final:
        out_refs[0][...] = normed
    else:
        out_refs[0][...] = x_new
        out_refs[1][...] = normed.astype(BF16)


def _merge(ys, p_arr, gates, x, wb, wo, g_next, *, final, tm=256):
    s = x.shape[0]
    row = lambda w: pl.BlockSpec((tm, w), lambda i: (i, 0))
    once = dict(pipeline_mode=pl.Buffered(1))
    in_specs = ([row(BRANCH_WIDTH)] * N_BRANCH
                + [row(N_BRANCH * BRANCH_WIDTH), row(N_BRANCH * D_MODEL), row(D_MODEL),
                   pl.BlockSpec((N_BRANCH, BRANCH_WIDTH, D_MODEL), lambda i: (0, 0, 0), **once),
                   pl.BlockSpec((D_MODEL, D_MODEL), lambda i: (0, 0), **once),
                   pl.BlockSpec((1, D_MODEL), lambda i: (0, 0))])
    if final:
        out_shape = [jax.ShapeDtypeStruct((s, D_MODEL), F32)]
    else:
        out_shape = [jax.ShapeDtypeStruct((s, D_MODEL), F32), jax.ShapeDtypeStruct((s, D_MODEL), BF16)]
    return pl.pallas_call(
        functools.partial(_merge_kernel, final=final),
        out_shape=out_shape,
        grid=(s // tm,),
        in_specs=in_specs,
        out_specs=[row(D_MODEL)] * len(out_shape),
        compiler_params=_params(("arbitrary",)),
        name="merge",
    )(*ys, p_arr, gates, x, wb, wo, g_next.reshape(1, D_MODEL))


def _rope_tables(seq_len):
    t = jnp.arange(seq_len)

    def angles(pos, dim):
        inv_freq = ROPE_THETA ** (-jnp.arange(dim // 2, dtype=F32) / (dim // 2))
        return pos.astype(F32)[:, None] * inv_freq[None, :]

    a1 = angles(t, HEAD_DIM)
    cos1 = jnp.concatenate([jnp.cos(a1), jnp.cos(a1)], axis=1)
    sin1 = jnp.concatenate([-jnp.sin(a1), jnp.sin(a1)], axis=1)
    ar = angles(t // GRID_W, HEAD_DIM // 2)
    ac = angles(t % GRID_W, HEAD_DIM // 2)
    cos2 = jnp.concatenate([jnp.cos(ar), jnp.cos(ac), jnp.cos(ar), jnp.cos(ac)], axis=1)
    sin2 = jnp.concatenate([-jnp.sin(ar), -jnp.sin(ac), jnp.sin(ar), jnp.sin(ac)], axis=1)
    return (cos1, sin1), (cos2, sin2)


AXIAL_PERM = np.concatenate([np.arange(0, 32), np.arange(64, 96), np.arange(32, 64), np.arange(96, 128)])


def _gather_cols(w, seg, head_perm=None):
    cols = np.concatenate([np.arange(_OFF[n], _OFF[n] + _WID[n]) for n in seg])
    if head_perm is not None:
        cols = cols.reshape(-1, HEAD_DIM)[:, head_perm].reshape(-1)
    if np.all(np.diff(cols) == 1):
        return w[:, cols[0]:cols[-1] + 1].astype(BF16)
    runs = np.split(cols, np.nonzero(np.diff(cols) != 1)[0] + 1)
    return jnp.concatenate([w[:, r[0]:r[-1] + 1] for r in runs], axis=1).astype(BF16)


def _col_scale(seg, scaled):
    return jnp.asarray(np.concatenate(
        [np.full((_WID[n],), scaled.get(n, 1.0), np.float32) for n in seg]).reshape(1, -1))


def _layer(x, h, w_in, gate_b, a_sink, b_rpb, d_q_norm, d_k_norm, w_branch, w_out, g_next, tables, final):
    s = x.shape[0]
    (cos1, sin1), (cos2, sin2) = tables
    tm = 1024

    r_arr = _proj(h, _gather_cols(w_in, ROPE_SEG), "rope", [_col_scale(ROPE_SEG, {"aq": SCALE, "cq": SCALE})],
                  [cos1, sin1], tm=tm, tn=ROPE_W // 2, name="proj_rope")
    gains = jnp.concatenate([jnp.tile(d_q_norm[AXIAL_PERM], _WID["dq"] // HEAD_DIM),
                             jnp.tile(d_k_norm[AXIAL_PERM], _WID["dk"] // HEAD_DIM)])
    n_arr = _proj(h, _gather_cols(w_in, NORM_SEG, AXIAL_PERM), "norm_rope",
                  [_col_scale(NORM_SEG, {"dq": SCALE * math.log2(math.e)}), gains.reshape(1, -1)],
                  [cos2, sin2], tm=tm, tn=NORM_W, name="proj_norm_rope")
    p_arr = _proj(h, _gather_cols(w_in, PLAIN_SEG), "scale", [_col_scale(PLAIN_SEG, {"bq": SCALE})], [],
                  tm=tm, tn=PLAIN_W // 3, name="proj_plain")
    gates = _proj(h, _gather_cols(w_in, ("gates",)), "sigmoid", [gate_b.reshape(1, -1)], [],
                  tm=tm, tn=2048, name="proj_gates")

    ya = _window_attention(r_arr, p_arr, a_sink, seq_len=s)
    yb = _nbr_attention(p_arr, _nbr_bias(b_rpb), seq_len=s)
    yc = _dilated_attention(r_arr, p_arr, seq_len=s)
    yd = _dense_attention(n_arr, p_arr, seq_len=s)
    return _merge((ya, yb, yc, yd), p_arr, gates, x, w_branch.astype(BF16), w_out.astype(BF16), g_next, final=final)


def kernel(x, norm_g, w_in, gate_b, a_sink, b_rpb, d_q_norm, d_k_norm, w_branch, w_out, final_norm_g):
    b, s, d = x.shape
    depth = w_in.shape[0]
    tables = _rope_tables(s)
    outs = []
    for bi in range(b):
        xb = x[bi]
        h = _rmsnorm(xb, norm_g[0], BF16)
        for layer in range(depth):
            final = layer == depth - 1
            g_next = final_norm_g if final else norm_g[layer + 1]
            res = _layer(xb, h, w_in[layer], gate_b[layer], a_sink[layer], b_rpb[layer], d_q_norm[layer],
                         d_k_norm[layer], w_branch[layer], w_out[layer], g_next, tables, final)
            if final:
                xb = res[0]
            else:
                xb, h = res
        outs.append(xb)
    return jnp.stack(outs, axis=0)
```

```python
import functools
import math

import numpy as np
import jax
import jax.numpy as jnp
from jax import lax
from jax.experimental import pallas as pl
from jax.experimental.pallas import tpu as pltpu

F32 = jnp.float32
BF16 = jnp.bfloat16

D_MODEL = 2048
HEAD_DIM = 128
BRANCH_WIDTH = 512
N_BRANCH = 4
A_HALF_WINDOW = 128
B_WIN_ROWS = 8
B_WIN_COLS = 16
C_PATTERNS = ((128, 1), (512, 4), (2048, 16))
GRID_W = 64
ROPE_THETA = 10000.0
NORM_EPS = 1e-6
SCALE = HEAD_DIM ** -0.5
NEG_BIG = -1e30

LANE = 128
VMEM_LIMIT = 56 * 1024 * 1024

_SPLITS = (512, 256, 256, 512, 512, 512, 512, 512, 1536, 512, 512, 512, 512, 256, 256, 512, 8192)
_NAMES = ("aq", "ak", "av", "az", "bq", "bk", "bv", "bz", "cq", "ck", "cv", "cz", "dq", "dk", "dv", "dz", "gates")
_OFF = dict(zip(_NAMES, np.concatenate([[0], np.cumsum(_SPLITS)[:-1]]).tolist()))
_WID = dict(zip(_NAMES, _SPLITS))

ROPE_SEG = ("aq", "ak", "cq", "ck")
NORM_SEG = ("dq", "dk")
PLAIN_SEG = ("az", "bz", "cz", "dz", "av", "dv", "bq", "bk", "bv", "cv")


def _seg_offsets(seg):
    offs, o = {}, 0
    for n in seg:
        offs[n] = o
        o += _WID[n]
    return offs, o


ROPE_OFF, ROPE_W = _seg_offsets(ROPE_SEG)
NORM_OFF, NORM_W = _seg_offsets(NORM_SEG)
PLAIN_OFF, PLAIN_W = _seg_offsets(PLAIN_SEG)


def _params(sem):
    return pltpu.CompilerParams(dimension_semantics=sem, vmem_limit_bytes=VMEM_LIMIT)


def _sigmoid(v):
    return 1.0 / (1.0 + jnp.exp(-v))


def _rmsnorm_kernel(x_ref, g_ref, o_ref):
    x = x_ref[...]
    y = x * lax.rsqrt(jnp.mean(x * x, axis=-1, keepdims=True) + NORM_EPS)
    o_ref[...] = (y * g_ref[...]).astype(o_ref.dtype)


def _rmsnorm(x, g, out_dtype, tm=512):
    s, d = x.shape
    return pl.pallas_call(
        _rmsnorm_kernel,
        out_shape=jax.ShapeDtypeStruct((s, d), out_dtype),
        grid=(s // tm,),
        in_specs=[pl.BlockSpec((tm, d), lambda i: (i, 0)), pl.BlockSpec((1, d), lambda i: (0, 0))],
        out_specs=pl.BlockSpec((tm, d), lambda i: (i, 0)),
        compiler_params=_params(("arbitrary",)),
        name="rmsnorm",
    )(x, g.reshape(1, d))


PROJ_RB = 256


def _rope_half(xh, cos, sin_signed):
    return xh * cos + pltpu.roll(xh, HEAD_DIM // 2, 1) * sin_signed


def _proj_kernel(*refs, mode):
    h_ref, w_ref = refs[0], refs[1]
    o_ref = refs[-1]
    tm, tn = o_ref.shape
    for r in range(tm // PROJ_RB):
        rows = pl.ds(r * PROJ_RB, PROJ_RB)
        acc = jnp.dot(h_ref[rows, :], w_ref[...], preferred_element_type=F32)
        if mode == "scale":
            o_ref[rows, :] = (acc * refs[2][...]).astype(o_ref.dtype)
        elif mode == "sigmoid":
            o_ref[rows, :] = _sigmoid(acc + refs[2][...]).astype(o_ref.dtype)
        elif mode == "rope":
            cs_ref, cos_ref, sin_ref = refs[2:5]
            cos, sin = cos_ref[rows, :], sin_ref[rows, :]
            for c in range(tn // HEAD_DIM):
                sl = slice(c * HEAD_DIM, (c + 1) * HEAD_DIM)
                o_ref[rows, sl] = (_rope_half(acc[:, sl], cos, sin) * cs_ref[:, sl]).astype(o_ref.dtype)
        elif mode == "norm_rope":
            cs_ref, g_ref, cos_ref, sin_ref = refs[2:6]
            cos, sin = cos_ref[rows, :], sin_ref[rows, :]
            for c in range(tn // HEAD_DIM):
                sl = slice(c * HEAD_DIM, (c + 1) * HEAD_DIM)
                xh = acc[:, sl]
                y = xh * lax.rsqrt(jnp.mean(xh * xh, axis=-1, keepdims=True) + NORM_EPS) * g_ref[:, sl]
                o_ref[rows, sl] = (_rope_half(y, cos, sin) * cs_ref[:, sl]).astype(o_ref.dtype)
        else:
            raise ValueError(mode)


def _proj(h, w, mode, row_vecs, tables, tm, tn, name):
    s, d = h.shape
    n = w.shape[1]
    tm = min(tm, s)
    in_specs = [pl.BlockSpec((tm, d), lambda i, j: (i, 0)), pl.BlockSpec((d, tn), lambda i, j: (0, j))]
    in_specs += [pl.BlockSpec((1, tn), lambda i, j: (0, j)) for _ in row_vecs]
    in_specs += [pl.BlockSpec((tm, HEAD_DIM), lambda i, j: (i, 0)) for _ in tables]
    return pl.pallas_call(
        functools.partial(_proj_kernel, mode=mode),
        out_shape=jax.ShapeDtypeStruct((s, n), BF16),
        grid=(s // tm, n // tn),
        in_specs=in_specs,
        out_specs=pl.BlockSpec((tm, tn), lambda i, j: (i, j)),
        compiler_params=_params(("arbitrary", "arbitrary")),
        name=name,
    )(h, w, *row_vecs, *tables)


WIN_BQ = 128


def _window_kernel(sink_ref, q_ref, k_ref, v_ref, o_ref, *, seq_len, nsub):
    kv_head = pl.program_id(0)
    step = pl.program_id(1)
    bq, hw, groups = WIN_BQ, A_HALF_WINDOW, 2
    span = bq + 2 * hw
    rows = groups * bq
    row = lax.broadcasted_iota(jnp.int32, (rows, span), 0)
    col_minus_row = lax.broadcasted_iota(jnp.int32, (rows, span), 1) - jnp.where(row >= bq, row - bq, row)
    r1 = lax.broadcasted_iota(jnp.int32, (rows, 1), 0)
    sink = jnp.where(r1 >= bq, sink_ref[kv_head * groups + 1], sink_ref[kv_head * groups])

    def body(b, carry):
        q0 = (step * nsub + b) * bq
        start = pl.multiple_of(jnp.clip(q0 - hw, 0, seq_len - span), bq)
        k = k_ref[pl.ds(start, span), :]
        v = v_ref[pl.ds(start, span), :]
        qoff = pl.multiple_of(b * bq, bq)
        qb = q_ref[pl.ds(qoff, bq), :]
        q2 = jnp.concatenate([qb[:, :HEAD_DIM], qb[:, HEAD_DIM:]], axis=0)
        s = lax.dot_general(q2, k, (((1,), (1,)), ((), ())), preferred_element_type=F32)
        s = jnp.where(jnp.abs(col_minus_row + (start - q0)) <= hw, s, NEG_BIG)
        m = jnp.maximum(jnp.max(s, axis=1, keepdims=True), sink)
        p = jnp.exp(s - m)
        denom = jnp.sum(p, axis=1, keepdims=True) + jnp.exp(sink - m)
        o = jnp.dot(p.astype(BF16), v, preferred_element_type=F32) / denom
        o_ref[pl.ds(qoff, bq), :HEAD_DIM] = o[:bq].astype(o_ref.dtype)
        o_ref[pl.ds(qoff, bq), HEAD_DIM:] = o[bq:].astype(o_ref.dtype)
        return carry

    lax.fori_loop(0, nsub, body, 0, unroll=4)


def _window_attention(r_arr, p_arr, sink, *, seq_len, max_step=1024):
    nsub = min(max_step, seq_len) // WIN_BQ
    tq = nsub * WIN_BQ
    qc, kc, vc = ROPE_OFF["aq"] // (2 * HEAD_DIM), ROPE_OFF["ak"] // HEAD_DIM, PLAIN_OFF["av"] // HEAD_DIM
    return pl.pallas_call(
        functools.partial(_window_kernel, seq_len=seq_len, nsub=nsub),
        out_shape=jax.ShapeDtypeStruct((seq_len, BRANCH_WIDTH), BF16),
        grid=(2, seq_len // tq),
        in_specs=[pl.BlockSpec(memory_space=pltpu.SMEM),
                  pl.BlockSpec((tq, 2 * HEAD_DIM), lambda c, i: (i, qc + c)),
                  pl.BlockSpec((seq_len, HEAD_DIM), lambda c, i: (0, kc + c)),
                  pl.BlockSpec((seq_len, HEAD_DIM), lambda c, i: (0, vc + c))],
        out_specs=pl.BlockSpec((tq, 2 * HEAD_DIM), lambda c, i: (i, c)),
        compiler_params=_params(("arbitrary", "arbitrary")),
        name="window_attention",
    )(sink, r_arr, r_arr, p_arr)


DIL_BQ = 128
DIL_HALF = 64
DIL_HALO = DIL_HALF * max(d for _, d in C_PATTERNS)


def _dilated_kernel(q0_ref, q1_ref, q2_ref, kp_ref, kc_ref, kn_ref, vp_ref, vc_ref, vn_ref, o_ref,
                    qf_ref, kw_ref, vw_ref, og_ref, lg_ref, *, seq_len, tq):
    t0 = pl.program_id(1) * tq
    halo, bq, half = DIL_HALO, DIL_BQ, DIL_HALF
    span = bq + 2 * half
    for w_ref, prev, cur, nxt in ((kw_ref, kp_ref, kc_ref, kn_ref), (vw_ref, vp_ref, vc_ref, vn_ref)):
        w_ref[:halo] = prev[...].astype(F32)
        w_ref[halo:halo + tq] = cur[...].astype(F32)
        w_ref[halo + tq:] = nxt[...].astype(F32)
    for g, q_ref in enumerate((q0_ref, q1_ref, q2_ref)):
        qf_ref[g] = q_ref[...].astype(F32)

    ii = lax.broadcasted_iota(jnp.int32, (bq, span), 0)
    jj = lax.broadcasted_iota(jnp.int32, (bq, span), 1)
    in_band = jnp.abs(jj - half - ii) <= half

    for g, (_, dil) in enumerate(C_PATTERNS):
        nb = tq // (bq * dil)

        def body(it, carry, g=g, dil=dil, nb=nb):
            q_row = it // nb + dil * bq * (it % nb)
            w_row = halo + q_row - half * dil
            q = qf_ref[g, pl.ds(q_row, bq, stride=dil), :].astype(BF16)
            k = kw_ref[pl.ds(w_row, span, stride=dil), :].astype(BF16)
            v = vw_ref[pl.ds(w_row, span, stride=dil), :].astype(BF16)
            s = lax.dot_general(q, k, (((1,), (1,)), ((), ())), preferred_element_type=F32)
            ktok = t0 + q_row - half * dil + dil * jj
            s = jnp.where(in_band & (ktok >= 0) & (ktok < seq_len), s, NEG_BIG)
            m = jnp.max(s, axis=1, keepdims=True)
            p = jnp.exp(s - m)
            denom = jnp.sum(p, axis=1, keepdims=True)
            o = jnp.dot(p.astype(BF16), v, preferred_element_type=F32) / denom
            og_ref[g, pl.ds(q_row, bq, stride=dil), :] = o
            lg_ref[g, pl.ds(q_row, bq, stride=dil), :] = jnp.broadcast_to(m + jnp.log(denom), (bq, HEAD_DIM))
            return carry

        lax.fori_loop(0, tq // bq, body, 0, unroll=True)

    l0, l1, l2 = lg_ref[0], lg_ref[1], lg_ref[2]
    mx = jnp.maximum(jnp.maximum(l0, l1), l2)
    e0, e1, e2 = jnp.exp(l0 - mx), jnp.exp(l1 - mx), jnp.exp(l2 - mx)
    o_ref[...] = ((e0 * og_ref[0] + e1 * og_ref[1] + e2 * og_ref[2]) / (e0 + e1 + e2)).astype(o_ref.dtype)


def _dilated_attention(r_arr, p_arr, *, seq_len, tq=2048):
    tq = min(tq, seq_len)
    halo = DIL_HALO
    ratio, n_halo = tq // halo, seq_len // halo
    qc, kc, vc = ROPE_OFF["cq"] // HEAD_DIM, ROPE_OFF["ck"] // HEAD_DIM, PLAIN_OFF["cv"] // HEAD_DIM
    n_heads = BRANCH_WIDTH // HEAD_DIM

    def window_specs(col):
        return [pl.BlockSpec((halo, HEAD_DIM), lambda h, i: (jnp.maximum(i * ratio - 1, 0), col + h)),
                pl.BlockSpec((tq, HEAD_DIM), lambda h, i: (i, col + h)),
                pl.BlockSpec((halo, HEAD_DIM), lambda h, i: (jnp.minimum((i + 1) * ratio, n_halo - 1), col + h))]

    q_specs = [pl.BlockSpec((tq, HEAD_DIM), lambda h, i, g=g: (i, qc + n_heads * g + h)) for g in range(3)]
    return pl.pallas_call(
        functools.partial(_dilated_kernel, seq_len=seq_len, tq=tq),
        out_shape=jax.ShapeDtypeStruct((seq_len, BRANCH_WIDTH), BF16),
        grid=(n_heads, seq_len // tq),
        in_specs=q_specs + window_specs(kc) + window_specs(vc),
        out_specs=pl.BlockSpec((tq, HEAD_DIM), lambda h, i: (i, h)),
        scratch_shapes=[pltpu.VMEM((3, tq, HEAD_DIM), F32), pltpu.VMEM((tq + 2 * halo, HEAD_DIM), F32),
                        pltpu.VMEM((tq + 2 * halo, HEAD_DIM), F32), pltpu.VMEM((3, tq, HEAD_DIM), F32),
                        pltpu.VMEM((3, tq, HEAD_DIM), F32)],
        compiler_params=_params(("arbitrary", "arbitrary")),
        name="dilated_attention",
    )(r_arr, r_arr, r_arr, r_arr, r_arr, r_arr, p_arr, p_arr, p_arr)


NBR_ROWS = 4
NBR_UNION = 12
NBR_SUB = 8


def _nbr_bias_kernel(rpb_ref, o_ref, t_ref):
    h = pl.program_id(0)
    n_dr, n_dc = 2 * B_WIN_ROWS - 1, 2 * B_WIN_COLS - 1
    qc = lax.broadcasted_iota(jnp.int32, (GRID_W, GRID_W), 0)
    kc = lax.broadcasted_iota(jnp.int32, (GRID_W, GRID_W), 1)
    col_start = jnp.clip(qc - B_WIN_COLS // 2, 0, GRID_W - B_WIN_COLS)
    col_ok = (kc >= col_start) & (kc < col_start + B_WIN_COLS)
    dc = kc - qc + (B_WIN_COLS - 1)
    for dr in range(n_dr):
        t = jnp.full((GRID_W, GRID_W), NEG_BIG, F32)
        for d in range(n_dc):
            t = jnp.where(dc == d, rpb_ref[(h * n_dr + dr) * n_dc + d], t)
        t_ref[dr] = jnp.where(col_ok, t, NEG_BIG)
    o_ref[...] = jnp.full(o_ref.shape, NEG_BIG, F32)
    half = B_WIN_ROWS // 2
    for case in range(3):
        for a in range(NBR_ROWS):
            win, off = ((0, a), (a, half), (NBR_UNION - B_WIN_ROWS, half + a))[case]
            for j in range(B_WIN_ROWS):
                o_ref[case, a * GRID_W:(a + 1) * GRID_W, (win + j) * GRID_W:(win + j + 1) * GRID_W] = (
                    t_ref[j - off + B_WIN_ROWS - 1])


def _nbr_bias(rpb):
    n_heads = rpb.shape[0]
    shape = (3, NBR_ROWS * GRID_W, NBR_UNION * GRID_W)
    return pl.pallas_call(
        _nbr_bias_kernel,
        out_shape=jax.ShapeDtypeStruct((n_heads,) + shape, F32),
        grid=(n_heads,),
        in_specs=[pl.BlockSpec(memory_space=pltpu.SMEM)],
        out_specs=pl.BlockSpec((None,) + shape, lambda h: (h, 0, 0, 0)),
        scratch_shapes=[pltpu.VMEM((2 * B_WIN_ROWS - 1, GRID_W, GRID_W), F32)],
        compiler_params=_params(("arbitrary",)),
        name="nbr_bias",
    )(rpb.reshape(-1))


def _nbr_kernel(q_ref, k_ref, v_ref, b_first_ref, b_mid_ref, b_last_ref, o_ref, *, n_rows):
    step = pl.program_id(1)
    nq, nk = NBR_ROWS * GRID_W, NBR_UNION * GRID_W
    for sb in range(NBR_SUB):
        bias_ref = b_first_ref if sb == 0 else (b_last_ref if sb == NBR_SUB - 1 else b_mid_ref)
        r0 = (step * NBR_SUB + sb) * NBR_ROWS
        u0 = jnp.clip(r0 - B_WIN_ROWS // 2, 0, n_rows - NBR_UNION)
        kstart = pl.multiple_of(u0 * GRID_W, GRID_W)
        k = k_ref[pl.ds(kstart, nk), :]
        v = v_ref[pl.ds(kstart, nk), :]
        q = q_ref[sb * nq:(sb + 1) * nq, :]
        s = lax.dot_general(q, k, (((1,), (1,)), ((), ())), preferred_element_type=F32) + bias_ref[...]
        m = jnp.max(s, axis=1, keepdims=True)
        p = jnp.exp(s - m)
        denom = jnp.sum(p, axis=1, keepdims=True)
        o = jnp.dot(p.astype(BF16), v, preferred_element_type=F32) / denom
        o_ref[sb * nq:(sb + 1) * nq, :] = o.astype(o_ref.dtype)


def _nbr_attention(p_arr, bias, *, seq_len):
    n_rows = seq_len // GRID_W
    tq = NBR_SUB * NBR_ROWS * GRID_W
    n_steps = seq_len // tq
    qc, kc, vc = (PLAIN_OFF[n] // HEAD_DIM for n in ("bq", "bk", "bv"))
    tile = (None, None, NBR_ROWS * GRID_W, NBR_UNION * GRID_W)
    return pl.pallas_call(
        functools.partial(_nbr_kernel, n_rows=n_rows),
        out_shape=jax.ShapeDtypeStruct((seq_len, BRANCH_WIDTH), BF16),
        grid=(BRANCH_WIDTH // HEAD_DIM, n_steps),
        in_specs=[pl.BlockSpec((tq, HEAD_DIM), lambda h, i: (i, qc + h)),
                  pl.BlockSpec((seq_len, HEAD_DIM), lambda h, i: (0, kc + h)),
                  pl.BlockSpec((seq_len, HEAD_DIM), lambda h, i: (0, vc + h)),
                  pl.BlockSpec(tile, lambda h, i: (h, jnp.where(i == 0, 0, 1), 0, 0)),
                  pl.BlockSpec(tile, lambda h, i: (h, 1, 0, 0)),
                  pl.BlockSpec(tile, lambda h, i: (h, jnp.where(i == n_steps - 1, 2, 1), 0, 0))],
        out_specs=pl.BlockSpec((tq, HEAD_DIM), lambda h, i: (i, h)),
        compiler_params=_params(("arbitrary", "arbitrary")),
        name="nbr_attention",
    )(p_arr, p_arr, p_arr, bias, bias, bias)


DENSE_RB = 256


def _dense_kernel(q_ref, k_ref, v_ref, o_ref, q2_ref, v2_ref, m_ref, acc_ref, s0_ref, *, seq_len, tq, tk):
    @pl.when(pl.program_id(1) == 0)
    def _():
        v2_ref[:, :HEAD_DIM] = v_ref[...]
        v2_ref[:, HEAD_DIM:] = jnp.ones((seq_len, HEAD_DIM), BF16)

    q2_ref[:tq] = q_ref[:, :HEAD_DIM]
    q2_ref[tq:] = q_ref[:, HEAD_DIM:]
    m_ref[...] = jnp.full(m_ref.shape, NEG_BIG, F32)
    acc_ref[...] = jnp.zeros(acc_ref.shape, F32)
    n_chunks = seq_len // tk

    def logits(r, k):
        return lax.dot_general(q2_ref[pl.ds(r * DENSE_RB, DENSE_RB), :], k, (((1,), (1,)), ((), ())),
                               preferred_element_type=F32)

    s0_ref[...] = logits(0, k_ref[pl.ds(0, tk), :])

    def body(j, carry):
        koff = pl.multiple_of(j * tk, tk)
        k = k_ref[pl.ds(koff, tk), :]
        v2 = v2_ref[pl.ds(koff, tk), :]
        for r in range((2 * tq) // DENSE_RB):
            rows = pl.ds(r * DENSE_RB, DENSE_RB)
            s = s0_ref[...] if r == 0 else logits(r, k)
            m_prev = m_ref[rows, :]
            m_new = jnp.maximum(m_prev, jnp.max(s, axis=1, keepdims=True))
            alpha = jnp.exp2(m_prev - m_new)
            p = jnp.exp2(s - m_new).astype(BF16)
            acc_ref[rows, :] = acc_ref[rows, :] * alpha + jnp.dot(p, v2, preferred_element_type=F32)
            m_ref[rows, :] = m_new
        next_off = pl.multiple_of(jnp.minimum(j + 1, n_chunks - 1) * tk, tk)
        s0_ref[...] = logits(0, k_ref[pl.ds(next_off, tk), :])
        return carry

    lax.fori_loop(0, n_chunks, body, 0)
    acc = acc_ref[...]
    o = acc[:, :HEAD_DIM] / acc[:, HEAD_DIM:]
    o_ref[:, :HEAD_DIM] = o[:tq].astype(o_ref.dtype)
    o_ref[:, HEAD_DIM:] = o[tq:].astype(o_ref.dtype)


def _dense_attention(n_arr, p_arr, *, seq_len, tq=2048, tk=2048):
    tq, tk = min(tq, seq_len), min(tk, seq_len)
    kc = NORM_OFF["dk"] // HEAD_DIM
    vc = PLAIN_OFF["dv"] // HEAD_DIM
    return pl.pallas_call(
        functools.partial(_dense_kernel, seq_len=seq_len, tq=tq, tk=tk),
        out_shape=jax.ShapeDtypeStruct((seq_len, BRANCH_WIDTH), BF16),
        grid=(2, seq_len // tq),
        in_specs=[pl.BlockSpec((tq, 2 * HEAD_DIM), lambda g, i: (i, g)),
                  pl.BlockSpec((seq_len, HEAD_DIM), lambda g, i: (0, kc + g), pipeline_mode=pl.Buffered(1)),
                  pl.BlockSpec((seq_len, HEAD_DIM), lambda g, i: (0, vc + g), pipeline_mode=pl.Buffered(1))],
        out_specs=pl.BlockSpec((tq, 2 * HEAD_DIM), lambda g, i: (i, g)),
        scratch_shapes=[pltpu.VMEM((2 * tq, HEAD_DIM), BF16), pltpu.VMEM((seq_len, 2 * HEAD_DIM), BF16),
                        pltpu.VMEM((2 * tq, 1), F32), pltpu.VMEM((2 * tq, 2 * HEAD_DIM), F32),
                        pltpu.VMEM((DENSE_RB, tk), F32)],
        compiler_params=_params(("arbitrary", "arbitrary")),
        name="dense_attention",
    )(n_arr, n_arr, p_arr)


def _merge_kernel(ya_ref, yb_ref, yc_ref, yd_ref, z_ref, gate_ref, x_ref, wb_ref, wo_ref, g_ref, *out_refs, final):
    merged = None
    for n, y_ref in enumerate((ya_ref, yb_ref, yc_ref, yd_ref)):
        z = z_ref[:, n * BRANCH_WIDTH:(n + 1) * BRANCH_WIDTH].astype(F32)
        t = (y_ref[...].astype(F32) * (z * _sigmoid(z))).astype(BF16)
        br = jnp.dot(t, wb_ref[n], preferred_element_type=F32)
        term = gate_ref[:, n * D_MODEL:(n + 1) * D_MODEL].astype(F32) * br
        merged = term if merged is None else merged + term
    x_new = x_ref[...] + jnp.dot(merged.astype(BF16), wo_ref[...], preferred_element_type=F32)
    normed = x_new * lax.rsqrt(jnp.mean(x_new * x_new, axis=-1, keepdims=True) + NORM_EPS) * g_ref[...]
    if final:
        out_refs[0][...] = normed
    else:
        out_refs[0][...] = x_new
        out_refs[1][...] = normed.astype(BF16)


def _merge(ys, p_arr, gates, x, wb, wo, g_next, *, final, tm=256):
    s = x.shape[0]
    row = lambda w: pl.BlockSpec((tm, w), lambda i: (i, 0))
    once = dict(pipeline_mode=pl.Buffered(1))
    in_specs = ([row(BRANCH_WIDTH)] * N_BRANCH
                + [row(N_BRANCH * BRANCH_WIDTH), row(N_BRANCH * D_MODEL), row(D_MODEL),
                   pl.BlockSpec((N_BRANCH, BRANCH_WIDTH, D_MODEL), lambda i: (0, 0, 0), **once),
                   pl.BlockSpec((D_MODEL, D_MODEL), lambda i: (0, 0), **once),
                   pl.BlockSpec((1, D_MODEL), lambda i: (0, 0))])
    if final:
        out_shape = [jax.ShapeDtypeStruct((s, D_MODEL), F32)]
    else:
        out_shape = [jax.ShapeDtypeStruct((s, D_MODEL), F32), jax.ShapeDtypeStruct((s, D_MODEL), BF16)]
    return pl.pallas_call(
        functools.partial(_merge_kernel, final=final),
        out_shape=out_shape,
        grid=(s // tm,),
        in_specs=in_specs,
        out_specs=[row(D_MODEL)] * len(out_shape),
        compiler_params=_params(("arbitrary",)),
        name="merge",
    )(*ys, p_arr, gates, x, wb, wo, g_next.reshape(1, D_MODEL))


@functools.lru_cache(maxsize=None)
def _rope_tables(seq_len):
    t = np.arange(seq_len)

    def angles(pos, dim):
        inv_freq = np.float32(ROPE_THETA) ** (-np.arange(dim // 2, dtype=np.float32) / np.float32(dim // 2))
        return pos.astype(np.float32)[:, None] * inv_freq.astype(np.float32)[None, :]

    a1 = angles(t, HEAD_DIM)
    cos1 = np.concatenate([np.cos(a1), np.cos(a1)], axis=1)
    sin1 = np.concatenate([-np.sin(a1), np.sin(a1)], axis=1)
    ar = angles(t // GRID_W, HEAD_DIM // 2)
    ac = angles(t % GRID_W, HEAD_DIM // 2)
    cos2 = np.concatenate([np.cos(ar), np.cos(ac), np.cos(ar), np.cos(ac)], axis=1)
    sin2 = np.concatenate([-np.sin(ar), -np.sin(ac), np.sin(ar), np.sin(ac)], axis=1)
    assert all(a.dtype == np.float32 for a in (cos1, sin1, cos2, sin2))
    return (cos1, sin1), (cos2, sin2)


AXIAL_PERM = np.concatenate([np.arange(0, 32), np.arange(64, 96), np.arange(32, 64), np.arange(96, 128)])


def _gather_cols(w, seg, head_perm=None):
    cols = np.concatenate([np.arange(_OFF[n], _OFF[n] + _WID[n]) for n in seg])
    if head_perm is not None:
        cols = cols.reshape(-1, HEAD_DIM)[:, head_perm].reshape(-1)
    if np.all(np.diff(cols) == 1):
        return w[:, cols[0]:cols[-1] + 1].astype(BF16)
    runs = np.split(cols, np.nonzero(np.diff(cols) != 1)[0] + 1)
    return jnp.concatenate([w[:, r[0]:r[-1] + 1] for r in runs], axis=1).astype(BF16)


def _col_scale(seg, scaled):
    return jnp.asarray(np.concatenate(
        [np.full((_WID[n],), scaled.get(n, 1.0), np.float32) for n in seg]).reshape(1, -1))


def _layer(x, h, w_in, gate_b, a_sink, b_rpb, d_q_norm, d_k_norm, w_branch, w_out, g_next, tables, final):
    s = x.shape[0]
    (cos1, sin1), (cos2, sin2) = tables
    tm = 1024

    r_arr = _proj(h, _gather_cols(w_in, ROPE_SEG), "rope", [_col_scale(ROPE_SEG, {"aq": SCALE, "cq": SCALE})],
                  [cos1, sin1], tm=tm, tn=ROPE_W // 2, name="proj_rope")
    gains = jnp.concatenate([jnp.tile(d_q_norm[AXIAL_PERM], _WID["dq"] // HEAD_DIM),
                             jnp.tile(d_k_norm[AXIAL_PERM], _WID["dk"] // HEAD_DIM)])
    n_arr = _proj(h, _gather_cols(w_in, NORM_SEG, AXIAL_PERM), "norm_rope",
                  [_col_scale(NORM_SEG, {"dq": SCALE * math.log2(math.e)}), gains.reshape(1, -1)],
                  [cos2, sin2], tm=tm, tn=NORM_W, name="proj_norm_rope")
    p_arr = _proj(h, _gather_cols(w_in, PLAIN_SEG), "scale", [_col_scale(PLAIN_SEG, {"bq": SCALE})], [],
                  tm=tm, tn=PLAIN_W // 3, name="proj_plain")
    gates = _proj(h, _gather_cols(w_in, ("gates",)), "sigmoid", [gate_b.reshape(1, -1)], [],
                  tm=tm, tn=2048, name="proj_gates")

    ya = _window_attention(r_arr, p_arr, a_sink, seq_len=s)
    yb = _nbr_attention(p_arr, _nbr_bias(b_rpb), seq_len=s)
    yc = _dilated_attention(r_arr, p_arr, seq_len=s)
    yd = _dense_attention(n_arr, p_arr, seq_len=s)
    return _merge((ya, yb, yc, yd), p_arr, gates, x, w_branch.astype(BF16), w_out.astype(BF16), g_next, final=final)


def kernel(x, norm_g, w_in, gate_b, a_sink, b_rpb, d_q_norm, d_k_norm, w_branch, w_out, final_norm_g):
    b, s, d = x.shape
    depth = w_in.shape[0]
    tables = tuple(tuple(jnp.asarray(a) for a in grp) for grp in _rope_tables(s))
    outs = []
    for bi in range(b):
        xb = x.reshape(s, d) if b == 1 else x[bi]
        h = _rmsnorm(xb, norm_g[0], BF16)
        for layer in range(depth):
            final = layer == depth - 1
            g_next = final_norm_g if final else norm_g[layer + 1]
            res = _layer(xb, h, w_in[layer], gate_b[layer], a_sink[layer], b_rpb[layer], d_q_norm[layer],
                         d_k_norm[layer], w_branch[layer], w_out[layer], g_next, tables, final)
            if final:
                xb = res[0]
            else:
                xb, h = res
        outs.append(xb)
    return outs[0].reshape(1, s, d) if b == 1 else jnp.stack(outs, axis=0)
```

```python
import functools
import math

import numpy as np
import jax
import jax.numpy as jnp
from jax import lax
from jax.experimental import pallas as pl
from jax.experimental.pallas import tpu as pltpu

F32 = jnp.float32
BF16 = jnp.bfloat16

D_MODEL = 2048
HEAD_DIM = 128
BRANCH_WIDTH = 512
N_BRANCH = 4
A_HALF_WINDOW = 128
B_WIN_ROWS = 8
B_WIN_COLS = 16
C_PATTERNS = ((128, 1), (512, 4), (2048, 16))
GRID_W = 64
ROPE_THETA = 10000.0
NORM_EPS = 1e-6
LOG2E = math.log2(math.e)
SCALE = HEAD_DIM ** -0.5 * LOG2E
NEG_BIG = -1e30

LANE = 128
VMEM_LIMIT = 56 * 1024 * 1024

_SPLITS = (512, 256, 256, 512, 512, 512, 512, 512, 1536, 512, 512, 512, 512, 256, 256, 512, 8192)
_NAMES = ("aq", "ak", "av", "az", "bq", "bk", "bv", "bz", "cq", "ck", "cv", "cz", "dq", "dk", "dv", "dz", "gates")
_OFF = dict(zip(_NAMES, np.concatenate([[0], np.cumsum(_SPLITS)[:-1]]).tolist()))
_WID = dict(zip(_NAMES, _SPLITS))

ROPE_SEG = ("aq", "ak", "cq", "ck")
NORM_SEG = ("dq", "dk")
PLAIN_SEG = ("az", "bz", "cz", "dz", "av", "dv", "bq", "bk", "bv", "cv")


def _seg_offsets(seg):
    offs, o = {}, 0
    for n in seg:
        offs[n] = o
        o += _WID[n]
    return offs, o


ROPE_OFF, ROPE_W = _seg_offsets(ROPE_SEG)
NORM_OFF, NORM_W = _seg_offsets(NORM_SEG)
PLAIN_OFF, PLAIN_W = _seg_offsets(PLAIN_SEG)


def _params(sem):
    return pltpu.CompilerParams(dimension_semantics=sem, vmem_limit_bytes=VMEM_LIMIT)


def _sigmoid(v):
    return 1.0 / (1.0 + jnp.exp(-v))


def _weighted_values(p, v):
    v1 = jnp.concatenate([v, jnp.ones_like(v)], axis=1)
    r = jnp.dot(p.astype(BF16), v1, preferred_element_type=F32)
    return r[:, :HEAD_DIM], r[:, HEAD_DIM:]


def _rmsnorm_kernel(x_ref, g_ref, o_ref):
    x = x_ref[...]
    y = x * lax.rsqrt(jnp.mean(x * x, axis=-1, keepdims=True) + NORM_EPS)
    o_ref[...] = (y * g_ref[...]).astype(o_ref.dtype)


def _rmsnorm(x, g, out_dtype, tm=512):
    s, d = x.shape
    return pl.pallas_call(
        _rmsnorm_kernel,
        out_shape=jax.ShapeDtypeStruct((s, d), out_dtype),
        grid=(s // tm,),
        in_specs=[pl.BlockSpec((tm, d), lambda i: (i, 0)), pl.BlockSpec((1, d), lambda i: (0, 0))],
        out_specs=pl.BlockSpec((tm, d), lambda i: (i, 0)),
        compiler_params=_params(("arbitrary",)),
        name="rmsnorm",
    )(x, g.reshape(1, d))


PROJ_RB = 256


def _rope_half(xh, cos, sin_signed):
    return xh * cos + pltpu.roll(xh, HEAD_DIM // 2, 1) * sin_signed


def _proj_kernel(*refs, mode):
    h_ref, w_ref = refs[0], refs[1]
    o_ref = refs[-1]
    tm, tn = o_ref.shape
    for r in range(tm // PROJ_RB):
        rows = pl.ds(r * PROJ_RB, PROJ_RB)
        acc = jnp.dot(h_ref[rows, :], w_ref[...], preferred_element_type=F32)
        if mode == "scale":
            o_ref[rows, :] = (acc * refs[2][...]).astype(o_ref.dtype)
        elif mode == "rope":
            cs_ref, cos_ref, sin_ref = refs[2:5]
            cos, sin = cos_ref[rows, :], sin_ref[rows, :]
            for c in range(tn // HEAD_DIM):
                sl = slice(c * HEAD_DIM, (c + 1) * HEAD_DIM)
                o_ref[rows, sl] = (_rope_half(acc[:, sl], cos, sin) * cs_ref[:, sl]).astype(o_ref.dtype)
        elif mode == "norm_rope":
            cs_ref, g_ref, cos_ref, sin_ref = refs[2:6]
            cos, sin = cos_ref[rows, :], sin_ref[rows, :]
            for c in range(tn // HEAD_DIM):
                sl = slice(c * HEAD_DIM, (c + 1) * HEAD_DIM)
                xh = acc[:, sl]
                y = xh * lax.rsqrt(jnp.mean(xh * xh, axis=-1, keepdims=True) + NORM_EPS) * g_ref[:, sl]
                o_ref[rows, sl] = (_rope_half(y, cos, sin) * cs_ref[:, sl]).astype(o_ref.dtype)
        else:
            raise ValueError(mode)


def _proj(h, w, mode, row_vecs, tables, tm, tn, name):
    s, d = h.shape
    n = w.shape[1]
    tm = min(tm, s)
    in_specs = [pl.BlockSpec((tm, d), lambda i, j: (i, 0)), pl.BlockSpec((d, tn), lambda i, j: (0, j))]
    in_specs += [pl.BlockSpec((1, tn), lambda i, j: (0, j)) for _ in row_vecs]
    in_specs += [pl.BlockSpec((tm, HEAD_DIM), lambda i, j: (i, 0)) for _ in tables]
    return pl.pallas_call(
        functools.partial(_proj_kernel, mode=mode),
        out_shape=jax.ShapeDtypeStruct((s, n), BF16),
        grid=(s // tm, n // tn),
        in_specs=in_specs,
        out_specs=pl.BlockSpec((tm, tn), lambda i, j: (i, j)),
        compiler_params=_params(("arbitrary", "arbitrary")),
        name=name,
    )(h, w, *row_vecs, *tables)


def _gates_kernel(h_ref, w_ref, b_ref, o_ref, wb_ref):
    @pl.when(pl.program_id(1) == 0)
    def _():
        wb_ref[...] = w_ref[...].astype(BF16)

    for r in range(o_ref.shape[0] // PROJ_RB):
        rows = pl.ds(r * PROJ_RB, PROJ_RB)
        acc = jnp.dot(h_ref[rows, :], wb_ref[...], preferred_element_type=F32)
        o_ref[rows, :] = _sigmoid(acc + b_ref[...]).astype(o_ref.dtype)


def _proj_gates(h, w_in_all, layer, bias, tm=1024, tn=2048):
    s, d = h.shape
    tm = min(tm, s)
    n = _WID["gates"]
    first = _OFF["gates"] // tn
    return pl.pallas_call(
        _gates_kernel,
        out_shape=jax.ShapeDtypeStruct((s, n), BF16),
        grid=(n // tn, s // tm),
        in_specs=[pl.BlockSpec((tm, d), lambda j, i: (i, 0)),
                  pl.BlockSpec((None, d, tn), lambda j, i: (layer, 0, first + j), pipeline_mode=pl.Buffered(1)),
                  pl.BlockSpec((1, tn), lambda j, i: (0, j))],
        out_specs=pl.BlockSpec((tm, tn), lambda j, i: (i, j)),
        scratch_shapes=[pltpu.VMEM((d, tn), BF16)],
        compiler_params=_params(("arbitrary", "arbitrary")),
        name="proj_gates",
    )(h, w_in_all, bias)


WIN_BQ = 128


def _window_kernel(sink_ref, q_ref, k_ref, v_ref, o_ref, *, seq_len, nsub):
    kv_head = pl.program_id(0)
    step = pl.program_id(1)
    bq, hw, groups = WIN_BQ, A_HALF_WINDOW, 2
    span = bq + 2 * hw
    rows = groups * bq
    row = lax.broadcasted_iota(jnp.int32, (rows, span), 0)
    col_minus_row = lax.broadcasted_iota(jnp.int32, (rows, span), 1) - jnp.where(row >= bq, row - bq, row)
    r1 = lax.broadcasted_iota(jnp.int32, (rows, 1), 0)
    sink = jnp.where(r1 >= bq, sink_ref[kv_head * groups + 1], sink_ref[kv_head * groups]) * LOG2E

    def body(b, carry):
        q0 = (step * nsub + b) * bq
        start = pl.multiple_of(jnp.clip(q0 - hw, 0, seq_len - span), bq)
        k = k_ref[pl.ds(start, span), :]
        v = v_ref[pl.ds(start, span), :]
        qoff = pl.multiple_of(b * bq, bq)
        qb = q_ref[pl.ds(qoff, bq), :]
        q2 = jnp.concatenate([qb[:, :HEAD_DIM], qb[:, HEAD_DIM:]], axis=0)
        s = lax.dot_general(q2, k, (((1,), (1,)), ((), ())), preferred_element_type=F32)
        s = jnp.where(jnp.abs(col_minus_row + (start - q0)) <= hw, s, NEG_BIG)
        m = jnp.maximum(jnp.max(s, axis=1, keepdims=True), sink)
        p = jnp.exp2(s - m)
        pv, psum = _weighted_values(p, v)
        o = pv / (psum + jnp.exp2(sink - m))
        o_ref[pl.ds(qoff, bq), :HEAD_DIM] = o[:bq].astype(o_ref.dtype)
        o_ref[pl.ds(qoff, bq), HEAD_DIM:] = o[bq:].astype(o_ref.dtype)
        return carry

    lax.fori_loop(0, nsub, body, 0, unroll=True)


def _window_attention(r_arr, p_arr, sink, *, seq_len, max_step=1024):
    nsub = min(max_step, seq_len) // WIN_BQ
    tq = nsub * WIN_BQ
    qc, kc, vc = ROPE_OFF["aq"] // (2 * HEAD_DIM), ROPE_OFF["ak"] // HEAD_DIM, PLAIN_OFF["av"] // HEAD_DIM
    return pl.pallas_call(
        functools.partial(_window_kernel, seq_len=seq_len, nsub=nsub),
        out_shape=jax.ShapeDtypeStruct((seq_len, BRANCH_WIDTH), BF16),
        grid=(2, seq_len // tq),
        in_specs=[pl.BlockSpec(memory_space=pltpu.SMEM),
                  pl.BlockSpec((tq, 2 * HEAD_DIM), lambda c, i: (i, qc + c)),
                  pl.BlockSpec((seq_len, HEAD_DIM), lambda c, i: (0, kc + c)),
                  pl.BlockSpec((seq_len, HEAD_DIM), lambda c, i: (0, vc + c))],
        out_specs=pl.BlockSpec((tq, 2 * HEAD_DIM), lambda c, i: (i, c)),
        compiler_params=_params(("arbitrary", "arbitrary")),
        name="window_attention",
    )(sink, r_arr, r_arr, p_arr)


DIL_BQ = 128
DIL_HALF = 64
DIL_HALO = DIL_HALF * max(d for _, d in C_PATTERNS)


def _dilated_kernel(q0_ref, q1_ref, q2_ref, kp_ref, kc_ref, kn_ref, vp_ref, vc_ref, vn_ref, o_ref,
                    qf_ref, kw_ref, vw_ref, og_ref, lg_ref, *, seq_len, tq):
    t0 = pl.program_id(1) * tq
    halo, bq, half = DIL_HALO, DIL_BQ, DIL_HALF
    span = bq + 2 * half
    for w_ref, prev, cur, nxt in ((kw_ref, kp_ref, kc_ref, kn_ref), (vw_ref, vp_ref, vc_ref, vn_ref)):
        w_ref[:halo] = prev[...].astype(F32)
        w_ref[halo:halo + tq] = cur[...].astype(F32)
        w_ref[halo + tq:] = nxt[...].astype(F32)
    for g, q_ref in enumerate((q0_ref, q1_ref, q2_ref)):
        qf_ref[g] = q_ref[...].astype(F32)

    ii = lax.broadcasted_iota(jnp.int32, (bq, span), 0)
    jj = lax.broadcasted_iota(jnp.int32, (bq, span), 1)
    in_band = jnp.abs(jj - half - ii) <= half

    for g, (_, dil) in enumerate(C_PATTERNS):
        nb = tq // (bq * dil)

        def body(it, carry, g=g, dil=dil, nb=nb):
            q_row = it // nb + dil * bq * (it % nb)
            w_row = halo + q_row - half * dil
            q = qf_ref[g, pl.ds(q_row, bq, stride=dil), :].astype(BF16)
            k = kw_ref[pl.ds(w_row, span, stride=dil), :].astype(BF16)
            v = vw_ref[pl.ds(w_row, span, stride=dil), :].astype(BF16)
            s = lax.dot_general(q, k, (((1,), (1,)), ((), ())), preferred_element_type=F32)
            ktok = t0 + q_row - half * dil + dil * jj
            s = jnp.where(in_band & (ktok >= 0) & (ktok < seq_len), s, NEG_BIG)
            m = jnp.max(s, axis=1, keepdims=True)
            p = jnp.exp2(s - m)
            pv, psum = _weighted_values(p, v)
            og_ref[g, pl.ds(q_row, bq, stride=dil), :] = pv / psum
            lg_ref[g, pl.ds(q_row, bq, stride=dil), :] = m + jnp.log2(psum)
            return carry

        lax.fori_loop(0, tq // bq, body, 0, unroll=True)

    l0, l1, l2 = lg_ref[0], lg_ref[1], lg_ref[2]
    mx = jnp.maximum(jnp.maximum(l0, l1), l2)
    e0, e1, e2 = jnp.exp2(l0 - mx), jnp.exp2(l1 - mx), jnp.exp2(l2 - mx)
    o_ref[...] = ((e0 * og_ref[0] + e1 * og_ref[1] + e2 * og_ref[2]) / (e0 + e1 + e2)).astype(o_ref.dtype)


def _dilated_attention(r_arr, p_arr, *, seq_len, tq=2048):
    tq = min(tq, seq_len)
    halo = DIL_HALO
    ratio, n_halo = tq // halo, seq_len // halo
    qc, kc, vc = ROPE_OFF["cq"] // HEAD_DIM, ROPE_OFF["ck"] // HEAD_DIM, PLAIN_OFF["cv"] // HEAD_DIM
    n_heads = BRANCH_WIDTH // HEAD_DIM

    def window_specs(col):
        return [pl.BlockSpec((halo, HEAD_DIM), lambda h, i: (jnp.maximum(i * ratio - 1, 0), col + h)),
                pl.BlockSpec((tq, HEAD_DIM), lambda h, i: (i, col + h)),
                pl.BlockSpec((halo, HEAD_DIM), lambda h, i: (jnp.minimum((i + 1) * ratio, n_halo - 1), col + h))]

    q_specs = [pl.BlockSpec((tq, HEAD_DIM), lambda h, i, g=g: (i, qc + n_heads * g + h)) for g in range(3)]
    return pl.pallas_call(
        functools.partial(_dilated_kernel, seq_len=seq_len, tq=tq),
        out_shape=jax.ShapeDtypeStruct((seq_len, BRANCH_WIDTH), BF16),
        grid=(n_heads, seq_len // tq),
        in_specs=q_specs + window_specs(kc) + window_specs(vc),
        out_specs=pl.BlockSpec((tq, HEAD_DIM), lambda h, i: (i, h)),
        scratch_shapes=[pltpu.VMEM((3, tq, HEAD_DIM), F32), pltpu.VMEM((tq + 2 * halo, HEAD_DIM), F32),
                        pltpu.VMEM((tq + 2 * halo, HEAD_DIM), F32), pltpu.VMEM((3, tq, HEAD_DIM), F32),
                        pltpu.VMEM((3, tq, HEAD_DIM), F32)],
        compiler_params=_params(("arbitrary", "arbitrary")),
        name="dilated_attention",
    )(r_arr, r_arr, r_arr, r_arr, r_arr, r_arr, p_arr, p_arr, p_arr)


NBR_ROWS = 4
NBR_UNION = 12
NBR_SUB = 8


def _nbr_bias_kernel(rpb_ref, o_ref, t_ref):
    h = pl.program_id(0)
    n_dr, n_dc = 2 * B_WIN_ROWS - 1, 2 * B_WIN_COLS - 1
    qc = lax.broadcasted_iota(jnp.int32, (GRID_W, GRID_W), 0)
    kc = lax.broadcasted_iota(jnp.int32, (GRID_W, GRID_W), 1)
    col_start = jnp.clip(qc - B_WIN_COLS // 2, 0, GRID_W - B_WIN_COLS)
    col_ok = (kc >= col_start) & (kc < col_start + B_WIN_COLS)
    dc = kc - qc + (B_WIN_COLS - 1)
    for dr in range(n_dr):
        t = jnp.full((GRID_W, GRID_W), NEG_BIG, F32)
        for d in range(n_dc):
            t = jnp.where(dc == d, rpb_ref[(h * n_dr + dr) * n_dc + d] * LOG2E, t)
        t_ref[dr] = jnp.where(col_ok, t, NEG_BIG)
    o_ref[...] = jnp.full(o_ref.shape, NEG_BIG, F32)
    half = B_WIN_ROWS // 2
    for case in range(3):
        for a in range(NBR_ROWS):
            win, off = ((0, a), (a, half), (NBR_UNION - B_WIN_ROWS, half + a))[case]
            for j in range(B_WIN_ROWS):
                o_ref[case, a * GRID_W:(a + 1) * GRID_W, (win + j) * GRID_W:(win + j + 1) * GRID_W] = (
                    t_ref[j - off + B_WIN_ROWS - 1])


def _nbr_bias(rpb):
    n_heads = rpb.shape[0]
    shape = (3, NBR_ROWS * GRID_W, NBR_UNION * GRID_W)
    return pl.pallas_call(
        _nbr_bias_kernel,
        out_shape=jax.ShapeDtypeStruct((n_heads,) + shape, F32),
        grid=(n_heads,),
        in_specs=[pl.BlockSpec(memory_space=pltpu.SMEM)],
        out_specs=pl.BlockSpec((None,) + shape, lambda h: (h, 0, 0, 0)),
        scratch_shapes=[pltpu.VMEM((2 * B_WIN_ROWS - 1, GRID_W, GRID_W), F32)],
        compiler_params=_params(("arbitrary",)),
        name="nbr_bias",
    )(rpb.reshape(-1))


def _nbr_kernel(q_ref, k_ref, v_ref, b_first_ref, b_mid_ref, b_last_ref, o_ref, *, n_rows):
    step = pl.program_id(1)
    nq, nk = NBR_ROWS * GRID_W, NBR_UNION * GRID_W
    for sb in range(NBR_SUB):
        bias_ref = b_first_ref if sb == 0 else (b_last_ref if sb == NBR_SUB - 1 else b_mid_ref)
        r0 = (step * NBR_SUB + sb) * NBR_ROWS
        u0 = jnp.clip(r0 - B_WIN_ROWS // 2, 0, n_rows - NBR_UNION)
        kstart = pl.multiple_of(u0 * GRID_W, GRID_W)
        k = k_ref[pl.ds(kstart, nk), :]
        v = v_ref[pl.ds(kstart, nk), :]
        q = q_ref[sb * nq:(sb + 1) * nq, :]
        s = lax.dot_general(q, k, (((1,), (1,)), ((), ())), preferred_element_type=F32) + bias_ref[...]
        m = jnp.max(s, axis=1, keepdims=True)
        p = jnp.exp2(s - m)
        pv, psum = _weighted_values(p, v)
        o_ref[sb * nq:(sb + 1) * nq, :] = (pv / psum).astype(o_ref.dtype)


def _nbr_attention(p_arr, bias, *, seq_len):
    n_rows = seq_len // GRID_W
    tq = NBR_SUB * NBR_ROWS * GRID_W
    n_steps = seq_len // tq
    qc, kc, vc = (PLAIN_OFF[n] // HEAD_DIM for n in ("bq", "bk", "bv"))
    tile = (None, None, NBR_ROWS * GRID_W, NBR_UNION * GRID_W)
    return pl.pallas_call(
        functools.partial(_nbr_kernel, n_rows=n_rows),
        out_shape=jax.ShapeDtypeStruct((seq_len, BRANCH_WIDTH), BF16),
        grid=(BRANCH_WIDTH // HEAD_DIM, n_steps),
        in_specs=[pl.BlockSpec((tq, HEAD_DIM), lambda h, i: (i, qc + h)),
                  pl.BlockSpec((seq_len, HEAD_DIM), lambda h, i: (0, kc + h)),
                  pl.BlockSpec((seq_len, HEAD_DIM), lambda h, i: (0, vc + h)),
                  pl.BlockSpec(tile, lambda h, i: (h, jnp.where(i == 0, 0, 1), 0, 0)),
                  pl.BlockSpec(tile, lambda h, i: (h, 1, 0, 0)),
                  pl.BlockSpec(tile, lambda h, i: (h, jnp.where(i == n_steps - 1, 2, 1), 0, 0))],
        out_specs=pl.BlockSpec((tq, HEAD_DIM), lambda h, i: (i, h)),
        compiler_params=_params(("arbitrary", "arbitrary")),
        name="nbr_attention",
    )(p_arr, p_arr, p_arr, bias, bias, bias)


DENSE_RB = 256


def _dense_kernel(q_ref, k_ref, v_ref, o_ref, q2_ref, v2_ref, m_ref, acc_ref, s0_ref, *, seq_len, tq, tk):
    @pl.when(pl.program_id(1) == 0)
    def _():
        v2_ref[:, :HEAD_DIM] = v_ref[...]
        v2_ref[:, HEAD_DIM:] = jnp.ones((seq_len, HEAD_DIM), BF16)

    q2_ref[:tq] = q_ref[:, :HEAD_DIM]
    q2_ref[tq:] = q_ref[:, HEAD_DIM:]
    m_ref[...] = jnp.full(m_ref.shape, NEG_BIG, F32)
    acc_ref[...] = jnp.zeros(acc_ref.shape, F32)
    n_chunks = seq_len // tk

    def logits(r, k):
        return lax.dot_general(q2_ref[pl.ds(r * DENSE_RB, DENSE_RB), :], k, (((1,), (1,)), ((), ())),
                               preferred_element_type=F32)

    s0_ref[...] = logits(0, k_ref[pl.ds(0, tk), :])

    def body(j, carry):
        koff = pl.multiple_of(j * tk, tk)
        k = k_ref[pl.ds(koff, tk), :]
        v2 = v2_ref[pl.ds(koff, tk), :]
        for r in range((2 * tq) // DENSE_RB):
            rows = pl.ds(r * DENSE_RB, DENSE_RB)
            s = s0_ref[...] if r == 0 else logits(r, k)
            m_prev = m_ref[rows, :]
            m_new = jnp.maximum(m_prev, jnp.max(s, axis=1, keepdims=True))
            alpha = jnp.exp2(m_prev - m_new)
            p = jnp.exp2(s - m_new).astype(BF16)
            acc_ref[rows, :] = acc_ref[rows, :] * alpha + jnp.dot(p, v2, preferred_element_type=F32)
            m_ref[rows, :] = m_new
        next_off = pl.multiple_of(jnp.minimum(j + 1, n_chunks - 1) * tk, tk)
        s0_ref[...] = logits(0, k_ref[pl.ds(next_off, tk), :])
        return carry

    lax.fori_loop(0, n_chunks, body, 0)
    acc = acc_ref[...]
    o = acc[:, :HEAD_DIM] / acc[:, HEAD_DIM:]
    o_ref[:, :HEAD_DIM] = o[:tq].astype(o_ref.dtype)
    o_ref[:, HEAD_DIM:] = o[tq:].astype(o_ref.dtype)


def _dense_attention(n_arr, p_arr, *, seq_len, tq=2048, tk=2048):
    tq, tk = min(tq, seq_len), min(tk, seq_len)
    kc = NORM_OFF["dk"] // HEAD_DIM
    vc = PLAIN_OFF["dv"] // HEAD_DIM
    return pl.pallas_call(
        functools.partial(_dense_kernel, seq_len=seq_len, tq=tq, tk=tk),
        out_shape=jax.ShapeDtypeStruct((seq_len, BRANCH_WIDTH), BF16),
        grid=(2, seq_len // tq),
        in_specs=[pl.BlockSpec((tq, 2 * HEAD_DIM), lambda g, i: (i, g)),
                  pl.BlockSpec((seq_len, HEAD_DIM), lambda g, i: (0, kc + g), pipeline_mode=pl.Buffered(1)),
                  pl.BlockSpec((seq_len, HEAD_DIM), lambda g, i: (0, vc + g), pipeline_mode=pl.Buffered(1))],
        out_specs=pl.BlockSpec((tq, 2 * HEAD_DIM), lambda g, i: (i, g)),
        scratch_shapes=[pltpu.VMEM((2 * tq, HEAD_DIM), BF16), pltpu.VMEM((seq_len, 2 * HEAD_DIM), BF16),
                        pltpu.VMEM((2 * tq, 1), F32), pltpu.VMEM((2 * tq, 2 * HEAD_DIM), F32),
                        pltpu.VMEM((DENSE_RB, tk), F32)],
        compiler_params=_params(("arbitrary", "arbitrary")),
        name="dense_attention",
    )(n_arr, n_arr, p_arr)


def _merge_kernel(ya_ref, yb_ref, yc_ref, yd_ref, z_ref, gate_ref, x_ref, wb_ref, wo_ref, g_ref, *out_refs, final):
    merged = None
    for n, y_ref in enumerate((ya_ref, yb_ref, yc_ref, yd_ref)):
        z = z_ref[:, n * BRANCH_WIDTH:(n + 1) * BRANCH_WIDTH].astype(F32)
        t = (y_ref[...].astype(F32) * (z * _sigmoid(z))).astype(BF16)
        br = jnp.dot(t, wb_ref[n], preferred_element_type=F32)
        term = gate_ref[:, n * D_MODEL:(n + 1) * D_MODEL].astype(F32) * br
        merged = term if merged is None else merged + term
    x_new = x_ref[...] + jnp.dot(merged.astype(BF16), wo_ref[...], preferred_element_type=F32)
    normed = x_new * lax.rsqrt(jnp.mean(x_new * x_new, axis=-1, keepdims=True) + NORM_EPS) * g_ref[...]
    if final:
        out_refs[0][...] = normed
    else:
        out_refs[0][...] = x_new
        out_refs[1][...] = normed.astype(BF16)


def _merge(ys, p_arr, gates, x, wb, wo, g_next, *, final, tm=256):
    s = x.shape[0]
    row = lambda w: pl.BlockSpec((tm, w), lambda i: (i, 0))
    once = dict(pipeline_mode=pl.Buffered(1))
    in_specs = ([row(BRANCH_WIDTH)] * N_BRANCH
                + [row(N_BRANCH * BRANCH_WIDTH), row(N_BRANCH * D_MODEL), row(D_MODEL),
                   pl.BlockSpec((N_BRANCH, BRANCH_WIDTH, D_MODEL), lambda i: (0, 0, 0), **once),
                   pl.BlockSpec((D_MODEL, D_MODEL), lambda i: (0, 0), **once),
                   pl.BlockSpec((1, D_MODEL), lambda i: (0, 0))])
    if final:
        out_shape = [jax.ShapeDtypeStruct((s, D_MODEL), F32)]
    else:
        out_shape = [jax.ShapeDtypeStruct((s, D_MODEL), F32), jax.ShapeDtypeStruct((s, D_MODEL), BF16)]
    return pl.pallas_call(
        functools.partial(_merge_kernel, final=final),
        out_shape=out_shape,
        grid=(s // tm,),
        in_specs=in_specs,
        out_specs=[row(D_MODEL)] * len(out_shape),
        compiler_params=_params(("arbitrary",)),
        name="merge",
    )(*ys, p_arr, gates, x, wb, wo, g_next.reshape(1, D_MODEL))


@functools.lru_cache(maxsize=None)
def _rope_tables(seq_len):
    t = np.arange(seq_len)

    def angles(pos, dim):
        inv_freq = np.float32(ROPE_THETA) ** (-np.arange(dim // 2, dtype=np.float32) / np.float32(dim // 2))
        return pos.astype(np.float32)[:, None] * inv_freq.astype(np.float32)[None, :]

    a1 = angles(t, HEAD_DIM)
    cos1 = np.concatenate([np.cos(a1), np.cos(a1)], axis=1)
    sin1 = np.concatenate([-np.sin(a1), np.sin(a1)], axis=1)
    ar = angles(t // GRID_W, HEAD_DIM // 2)
    ac = angles(t % GRID_W, HEAD_DIM // 2)
    cos2 = np.concatenate([np.cos(ar), np.cos(ac), np.cos(ar), np.cos(ac)], axis=1)
    sin2 = np.concatenate([-np.sin(ar), -np.sin(ac), np.sin(ar), np.sin(ac)], axis=1)
    assert all(a.dtype == np.float32 for a in (cos1, sin1, cos2, sin2))
    return (cos1, sin1), (cos2, sin2)


AXIAL_PERM = np.concatenate([np.arange(0, 32), np.arange(64, 96), np.arange(32, 64), np.arange(96, 128)])


def _gather_cols(w, seg, head_perm=None):
    cols = np.concatenate([np.arange(_OFF[n], _OFF[n] + _WID[n]) for n in seg])
    if head_perm is not None:
        cols = cols.reshape(-1, HEAD_DIM)[:, head_perm].reshape(-1)
    if np.all(np.diff(cols) == 1):
        return w[:, cols[0]:cols[-1] + 1].astype(BF16)
    runs = np.split(cols, np.nonzero(np.diff(cols) != 1)[0] + 1)
    return jnp.concatenate([w[:, r[0]:r[-1] + 1] for r in runs], axis=1).astype(BF16)


def _col_scale(seg, scaled):
    return jnp.asarray(np.concatenate(
        [np.full((_WID[n],), scaled.get(n, 1.0), np.float32) for n in seg]).reshape(1, -1))


def _layer(x, h, w_in_all, layer, gate_b, a_sink, b_rpb, d_q_norm, d_k_norm, w_branch, w_out, g_next, tables,
           final):
    s = x.shape[0]
    (cos1, sin1), (cos2, sin2) = tables
    tm = 1024
    w_in = w_in_all[layer]

    r_arr = _proj(h, _gather_cols(w_in, ROPE_SEG), "rope", [_col_scale(ROPE_SEG, {"aq": SCALE, "cq": SCALE})],
                  [cos1, sin1], tm=tm, tn=ROPE_W // 2, name="proj_rope")
    gains = jnp.concatenate([jnp.tile(d_q_norm[AXIAL_PERM], _WID["dq"] // HEAD_DIM),
                             jnp.tile(d_k_norm[AXIAL_PERM], _WID["dk"] // HEAD_DIM)])
    n_arr = _proj(h, _gather_cols(w_in, NORM_SEG, AXIAL_PERM), "norm_rope",
                  [_col_scale(NORM_SEG, {"dq": SCALE}), gains.reshape(1, -1)],
                  [cos2, sin2], tm=tm, tn=NORM_W, name="proj_norm_rope")
    p_arr = _proj(h, _gather_cols(w_in, PLAIN_SEG), "scale", [_col_scale(PLAIN_SEG, {"bq": SCALE})], [],
                  tm=tm, tn=PLAIN_W // 3, name="proj_plain")
    gates = _proj_gates(h, w_in_all, layer, gate_b.reshape(1, -1))

    ya = _window_attention(r_arr, p_arr, a_sink, seq_len=s)
    yb = _nbr_attention(p_arr, _nbr_bias(b_rpb), seq_len=s)
    yc = _dilated_attention(r_arr, p_arr, seq_len=s)
    yd = _dense_attention(n_arr, p_arr, seq_len=s)
    return _merge((ya, yb, yc, yd), p_arr, gates, x, w_branch.astype(BF16), w_out.astype(BF16), g_next, final=final)


def kernel(x, norm_g, w_in, gate_b, a_sink, b_rpb, d_q_norm, d_k_norm, w_branch, w_out, final_norm_g):
    b, s, d = x.shape
    depth = w_in.shape[0]
    tables = tuple(tuple(jnp.asarray(a) for a in grp) for grp in _rope_tables(s))
    outs = []
    for bi in range(b):
        xb = x.reshape(s, d) if b == 1 else x[bi]
        h = _rmsnorm(xb, norm_g[0], BF16)
        for layer in range(depth):
            final = layer == depth - 1
            g_next = final_norm_g if final else norm_g[layer + 1]
            res = _layer(xb, h, w_in, layer, gate_b[layer], a_sink[layer], b_rpb[layer], d_q_norm[layer],
                         d_k_norm[layer], w_branch[layer], w_out[layer], g_next, tables, final)
            if final:
                xb = res[0]
            else:
                xb, h = res
        outs.append(xb)
    return outs[0].reshape(1, s, d) if b == 1 else jnp.stack(outs, axis=0)
```

```python
import functools
import math

import numpy as np
import jax
import jax.numpy as jnp
from jax import lax
from jax.experimental import pallas as pl
from jax.experimental.pallas import tpu as pltpu

F32 = jnp.float32
BF16 = jnp.bfloat16

D_MODEL = 2048
HEAD_DIM = 128
BRANCH_WIDTH = 512
N_BRANCH = 4
A_HALF_WINDOW = 128
B_WIN_ROWS = 8
B_WIN_COLS = 16
C_PATTERNS = ((128, 1), (512, 4), (2048, 16))
GRID_W = 64
ROPE_THETA = 10000.0
NORM_EPS = 1e-6
LOG2E = math.log2(math.e)
SCALE = HEAD_DIM ** -0.5 * LOG2E
NEG_BIG = -1e30

LANE = 128
VMEM_LIMIT = 56 * 1024 * 1024

_SPLITS = (512, 256, 256, 512, 512, 512, 512, 512, 1536, 512, 512, 512, 512, 256, 256, 512, 8192)
_NAMES = ("aq", "ak", "av", "az", "bq", "bk", "bv", "bz", "cq", "ck", "cv", "cz", "dq", "dk", "dv", "dz", "gates")
_OFF = dict(zip(_NAMES, np.concatenate([[0], np.cumsum(_SPLITS)[:-1]]).tolist()))
_WID = dict(zip(_NAMES, _SPLITS))

ROPE_SEG = ("aq", "ak", "cq", "ck")
NORM_SEG = ("dq", "dk")
PLAIN_SEG = ("az", "bz", "cz", "dz", "av", "dv", "bq", "bk", "bv", "cv")


def _seg_offsets(seg):
    offs, o = {}, 0
    for n in seg:
        offs[n] = o
        o += _WID[n]
    return offs, o


ROPE_OFF, ROPE_W = _seg_offsets(ROPE_SEG)
NORM_OFF, NORM_W = _seg_offsets(NORM_SEG)
PLAIN_OFF, PLAIN_W = _seg_offsets(PLAIN_SEG)


def _params(sem):
    return pltpu.CompilerParams(dimension_semantics=sem, vmem_limit_bytes=VMEM_LIMIT)


def _sigmoid(v):
    return 1.0 / (1.0 + jnp.exp(-v))


def _weighted_values(p, v):
    v1 = jnp.concatenate([v, jnp.ones_like(v)], axis=1)
    r = jnp.dot(p.astype(BF16), v1, preferred_element_type=F32)
    return r[:, :HEAD_DIM], r[:, HEAD_DIM:]


PROJ_RB = 256


def _rope_half(xh, cos, sin_signed):
    return xh * cos + pltpu.roll(xh, HEAD_DIM // 2, 1) * sin_signed


def _proj_kernel(*refs, mode, norm_input):
    if norm_input:
        g_in_ref, refs, h_out_ref = refs[0], refs[1:-1], refs[-1]
    h_ref, w_ref = refs[0], refs[1]
    o_ref = refs[-1]
    tm, tn = o_ref.shape
    for r in range(tm // PROJ_RB):
        rows = pl.ds(r * PROJ_RB, PROJ_RB)
        if norm_input:
            x = h_ref[rows, :]
            h = (x * lax.rsqrt(jnp.mean(x * x, axis=-1, keepdims=True) + NORM_EPS) * g_in_ref[...]).astype(BF16)
            h_out_ref[rows, :] = h
        else:
            h = h_ref[rows, :]
        acc = jnp.dot(h, w_ref[...], preferred_element_type=F32)
        if mode == "scale":
            o_ref[rows, :] = (acc * refs[2][...]).astype(o_ref.dtype)
        elif mode == "rope":
            cs_ref, cos_ref, sin_ref = refs[2:5]
            cos, sin = cos_ref[rows, :], sin_ref[rows, :]
            for c in range(tn // HEAD_DIM):
                sl = slice(c * HEAD_DIM, (c + 1) * HEAD_DIM)
                o_ref[rows, sl] = (_rope_half(acc[:, sl], cos, sin) * cs_ref[:, sl]).astype(o_ref.dtype)
        elif mode == "norm_rope":
            cs_ref, g_ref, cos_ref, sin_ref = refs[2:6]
            cos, sin = cos_ref[rows, :], sin_ref[rows, :]
            for c in range(tn // HEAD_DIM):
                sl = slice(c * HEAD_DIM, (c + 1) * HEAD_DIM)
                xh = acc[:, sl]
                y = xh * lax.rsqrt(jnp.mean(xh * xh, axis=-1, keepdims=True) + NORM_EPS) * g_ref[:, sl]
                o_ref[rows, sl] = (_rope_half(y, cos, sin) * cs_ref[:, sl]).astype(o_ref.dtype)
        else:
            raise ValueError(mode)


def _proj(h, w, mode, row_vecs, tables, tm, tn, name, norm_gain=None):
    s, d = h.shape
    n = w.shape[1]
    tm = min(tm, s)
    in_specs = [pl.BlockSpec((tm, d), lambda i, j: (i, 0)), pl.BlockSpec((d, tn), lambda i, j: (0, j))]
    in_specs += [pl.BlockSpec((1, tn), lambda i, j: (0, j)) for _ in row_vecs]
    in_specs += [pl.BlockSpec((tm, HEAD_DIM), lambda i, j: (i, 0)) for _ in tables]
    out_shape = [jax.ShapeDtypeStruct((s, n), BF16)]
    out_specs = [pl.BlockSpec((tm, tn), lambda i, j: (i, j))]
    args = [h, w, *row_vecs, *tables]
    if norm_gain is not None:
        assert n == tn
        in_specs = [pl.BlockSpec((1, d), lambda i, j: (0, 0))] + in_specs
        args = [norm_gain.reshape(1, d)] + args
        out_shape.append(jax.ShapeDtypeStruct((s, d), BF16))
        out_specs.append(pl.BlockSpec((tm, d), lambda i, j: (i, 0)))
    outs = pl.pallas_call(
        functools.partial(_proj_kernel, mode=mode, norm_input=norm_gain is not None),
        out_shape=out_shape,
        grid=(s // tm, n // tn),
        in_specs=in_specs,
        out_specs=out_specs,
        compiler_params=_params(("arbitrary", "arbitrary")),
        name=name,
    )(*args)
    return outs[0] if norm_gain is None else outs


def _gates_kernel(h_ref, w_ref, b_ref, o_ref, wb_ref):
    @pl.when(pl.program_id(1) == 0)
    def _():
        wb_ref[...] = w_ref[...].astype(BF16)

    for r in range(o_ref.shape[0] // PROJ_RB):
        rows = pl.ds(r * PROJ_RB, PROJ_RB)
        acc = jnp.dot(h_ref[rows, :], wb_ref[...], preferred_element_type=F32)
        o_ref[rows, :] = _sigmoid(acc + b_ref[...]).astype(o_ref.dtype)


def _proj_gates(h, w_in_all, layer, bias, tm=1024, tn=1024):
    s, d = h.shape
    tm = min(tm, s)
    n = _WID["gates"]
    first = _OFF["gates"] // tn
    return pl.pallas_call(
        _gates_kernel,
        out_shape=jax.ShapeDtypeStruct((s, n), BF16),
        grid=(n // tn, s // tm),
        in_specs=[pl.BlockSpec((tm, d), lambda j, i: (i, 0)),
                  pl.BlockSpec((None, d, tn), lambda j, i: (layer, 0, first + j)),
                  pl.BlockSpec((1, tn), lambda j, i: (0, j))],
        out_specs=pl.BlockSpec((tm, tn), lambda j, i: (i, j)),
        scratch_shapes=[pltpu.VMEM((d, tn), BF16)],
        compiler_params=_params(("arbitrary", "arbitrary")),
        name="proj_gates",
    )(h, w_in_all, bias)


WIN_BQ = 128


def _window_kernel(sink_ref, q_ref, k_ref, v_ref, o_ref, *, seq_len, nsub):
    kv_head = pl.program_id(0)
    step = pl.program_id(1)
    bq, hw, groups = WIN_BQ, A_HALF_WINDOW, 2
    span = bq + 2 * hw
    rows = groups * bq
    row = lax.broadcasted_iota(jnp.int32, (rows, span), 0)
    col_minus_row = lax.broadcasted_iota(jnp.int32, (rows, span), 1) - jnp.where(row >= bq, row - bq, row)
    r1 = lax.broadcasted_iota(jnp.int32, (rows, 1), 0)
    sink = jnp.where(r1 >= bq, sink_ref[kv_head * groups + 1], sink_ref[kv_head * groups]) * LOG2E

    def body(b, carry):
        q0 = (step * nsub + b) * bq
        start = pl.multiple_of(jnp.clip(q0 - hw, 0, seq_len - span), bq)
        k = k_ref[pl.ds(start, span), :]
        v = v_ref[pl.ds(start, span), :]
        qoff = pl.multiple_of(b * bq, bq)
        qb = q_ref[pl.ds(qoff, bq), :]
        q2 = jnp.concatenate([qb[:, :HEAD_DIM], qb[:, HEAD_DIM:]], axis=0)
        s = lax.dot_general(q2, k, (((1,), (1,)), ((), ())), preferred_element_type=F32)
        s = jnp.where(jnp.abs(col_minus_row + (start - q0)) <= hw, s, NEG_BIG)
        m = jnp.maximum(jnp.max(s, axis=1, keepdims=True), sink)
        p = jnp.exp2(s - m)
        pv, psum = _weighted_values(p, v)
        o = pv / (psum + jnp.exp2(sink - m))
        o_ref[pl.ds(qoff, bq), :HEAD_DIM] = o[:bq].astype(o_ref.dtype)
        o_ref[pl.ds(qoff, bq), HEAD_DIM:] = o[bq:].astype(o_ref.dtype)
        return carry

    lax.fori_loop(0, nsub, body, 0, unroll=True)


def _window_attention(r_arr, p_arr, sink, *, seq_len, max_step=1024):
    nsub = min(max_step, seq_len) // WIN_BQ
    tq = nsub * WIN_BQ
    qc, kc, vc = ROPE_OFF["aq"] // (2 * HEAD_DIM), ROPE_OFF["ak"] // HEAD_DIM, PLAIN_OFF["av"] // HEAD_DIM
    return pl.pallas_call(
        functools.partial(_window_kernel, seq_len=seq_len, nsub=nsub),
        out_shape=jax.ShapeDtypeStruct((seq_len, BRANCH_WIDTH), BF16),
        grid=(2, seq_len // tq),
        in_specs=[pl.BlockSpec(memory_space=pltpu.SMEM),
                  pl.BlockSpec((tq, 2 * HEAD_DIM), lambda c, i: (i, qc + c)),
                  pl.BlockSpec((seq_len, HEAD_DIM), lambda c, i: (0, kc + c)),
                  pl.BlockSpec((seq_len, HEAD_DIM), lambda c, i: (0, vc + c))],
        out_specs=pl.BlockSpec((tq, 2 * HEAD_DIM), lambda c, i: (i, c)),
        compiler_params=_params(("arbitrary", "arbitrary")),
        name="window_attention",
    )(sink, r_arr, r_arr, p_arr)


DIL_BQ = 128
DIL_HALF = 64
DIL_HALO = DIL_HALF * max(d for _, d in C_PATTERNS)


def _dilated_kernel(q0_ref, q1_ref, q2_ref, kp_ref, kc_ref, kn_ref, vp_ref, vc_ref, vn_ref, o_ref,
                    qf_ref, kw_ref, vw_ref, og_ref, lg_ref, *, seq_len, tq):
    t0 = pl.program_id(1) * tq
    halo, bq, half = DIL_HALO, DIL_BQ, DIL_HALF
    span = bq + 2 * half
    for w_ref, prev, cur, nxt in ((kw_ref, kp_ref, kc_ref, kn_ref), (vw_ref, vp_ref, vc_ref, vn_ref)):
        w_ref[:halo] = prev[...].astype(F32)
        w_ref[halo:halo + tq] = cur[...].astype(F32)
        w_ref[halo + tq:] = nxt[...].astype(F32)
    for g, q_ref in enumerate((q0_ref, q1_ref, q2_ref)):
        qf_ref[g] = q_ref[...].astype(F32)

    ii = lax.broadcasted_iota(jnp.int32, (bq, span), 0)
    jj = lax.broadcasted_iota(jnp.int32, (bq, span), 1)
    in_band = jnp.abs(jj - half - ii) <= half

    for g, (_, dil) in enumerate(C_PATTERNS):
        nb = tq // (bq * dil)

        def body(it, carry, g=g, dil=dil, nb=nb):
            q_row = it // nb + dil * bq * (it % nb)
            w_row = halo + q_row - half * dil
            q = qf_ref[g, pl.ds(q_row, bq, stride=dil), :].astype(BF16)
            k = kw_ref[pl.ds(w_row, span, stride=dil), :].astype(BF16)
            v = vw_ref[pl.ds(w_row, span, stride=dil), :].astype(BF16)
            s = lax.dot_general(q, k, (((1,), (1,)), ((), ())), preferred_element_type=F32)
            ktok = t0 + q_row - half * dil + dil * jj
            s = jnp.where(in_band & (ktok >= 0) & (ktok < seq_len), s, NEG_BIG)
            m = jnp.max(s, axis=1, keepdims=True)
            p = jnp.exp2(s - m)
            pv, psum = _weighted_values(p, v)
            og_ref[g, pl.ds(q_row, bq, stride=dil), :] = pv / psum
            lg_ref[g, pl.ds(q_row, bq, stride=dil), :] = m + jnp.log2(psum)
            return carry

        lax.fori_loop(0, tq // bq, body, 0, unroll=True)

    l0, l1, l2 = lg_ref[0], lg_ref[1], lg_ref[2]
    mx = jnp.maximum(jnp.maximum(l0, l1), l2)
    e0, e1, e2 = jnp.exp2(l0 - mx), jnp.exp2(l1 - mx), jnp.exp2(l2 - mx)
    o_ref[...] = ((e0 * og_ref[0] + e1 * og_ref[1] + e2 * og_ref[2]) / (e0 + e1 + e2)).astype(o_ref.dtype)


def _dilated_attention(r_arr, p_arr, *, seq_len, tq=2048):
    tq = min(tq, seq_len)
    halo = DIL_HALO
    ratio, n_halo = tq // halo, seq_len // halo
    qc, kc, vc = ROPE_OFF["cq"] // HEAD_DIM, ROPE_OFF["ck"] // HEAD_DIM, PLAIN_OFF["cv"] // HEAD_DIM
    n_heads = BRANCH_WIDTH // HEAD_DIM

    def window_specs(col):
        return [pl.BlockSpec((halo, HEAD_DIM), lambda h, i: (jnp.maximum(i * ratio - 1, 0), col + h)),
                pl.BlockSpec((tq, HEAD_DIM), lambda h, i: (i, col + h)),
                pl.BlockSpec((halo, HEAD_DIM), lambda h, i: (jnp.minimum((i + 1) * ratio, n_halo - 1), col + h))]

    q_specs = [pl.BlockSpec((tq, HEAD_DIM), lambda h, i, g=g: (i, qc + n_heads * g + h)) for g in range(3)]
    return pl.pallas_call(
        functools.partial(_dilated_kernel, seq_len=seq_len, tq=tq),
        out_shape=jax.ShapeDtypeStruct((seq_len, BRANCH_WIDTH), BF16),
        grid=(n_heads, seq_len // tq),
        in_specs=q_specs + window_specs(kc) + window_specs(vc),
        out_specs=pl.BlockSpec((tq, HEAD_DIM), lambda h, i: (i, h)),
        scratch_shapes=[pltpu.VMEM((3, tq, HEAD_DIM), F32), pltpu.VMEM((tq + 2 * halo, HEAD_DIM), F32),
                        pltpu.VMEM((tq + 2 * halo, HEAD_DIM), F32), pltpu.VMEM((3, tq, HEAD_DIM), F32),
                        pltpu.VMEM((3, tq, HEAD_DIM), F32)],
        compiler_params=_params(("arbitrary", "arbitrary")),
        name="dilated_attention",
    )(r_arr, r_arr, r_arr, r_arr, r_arr, r_arr, p_arr, p_arr, p_arr)


NBR_ROWS = 4
NBR_UNION = 12
NBR_SUB = 8


def _nbr_bias_kernel(rpb_ref, o_ref, t_ref):
    h = pl.program_id(0)
    n_dr, n_dc = 2 * B_WIN_ROWS - 1, 2 * B_WIN_COLS - 1
    qc = lax.broadcasted_iota(jnp.int32, (GRID_W, GRID_W), 0)
    kc = lax.broadcasted_iota(jnp.int32, (GRID_W, GRID_W), 1)
    col_start = jnp.clip(qc - B_WIN_COLS // 2, 0, GRID_W - B_WIN_COLS)
    col_ok = (kc >= col_start) & (kc < col_start + B_WIN_COLS)
    dc = kc - qc + (B_WIN_COLS - 1)
    for dr in range(n_dr):
        t = jnp.full((GRID_W, GRID_W), NEG_BIG, F32)
        for d in range(n_dc):
            t = jnp.where(dc == d, rpb_ref[(h * n_dr + dr) * n_dc + d] * LOG2E, t)
        t_ref[dr] = jnp.where(col_ok, t, NEG_BIG)
    o_ref[...] = jnp.full(o_ref.shape, NEG_BIG, F32)
    half = B_WIN_ROWS // 2
    for case in range(3):
        for a in range(NBR_ROWS):
            win, off = ((0, a), (a, half), (NBR_UNION - B_WIN_ROWS, half + a))[case]
            for j in range(B_WIN_ROWS):
                o_ref[case, a * GRID_W:(a + 1) * GRID_W, (win + j) * GRID_W:(win + j + 1) * GRID_W] = (
                    t_ref[j - off + B_WIN_ROWS - 1])


def _nbr_bias(rpb):
    n_heads = rpb.shape[0]
    shape = (3, NBR_ROWS * GRID_W, NBR_UNION * GRID_W)
    return pl.pallas_call(
        _nbr_bias_kernel,
        out_shape=jax.ShapeDtypeStruct((n_heads,) + shape, F32),
        grid=(n_heads,),
        in_specs=[pl.BlockSpec(memory_space=pltpu.SMEM)],
        out_specs=pl.BlockSpec((None,) + shape, lambda h: (h, 0, 0, 0)),
        scratch_shapes=[pltpu.VMEM((2 * B_WIN_ROWS - 1, GRID_W, GRID_W), F32)],
        compiler_params=_params(("arbitrary",)),
        name="nbr_bias",
    )(rpb.reshape(-1))


def _nbr_kernel(q_ref, k_ref, v_ref, b_first_ref, b_mid_ref, b_last_ref, o_ref, *, n_rows):
    step = pl.program_id(1)
    nq, nk = NBR_ROWS * GRID_W, NBR_UNION * GRID_W
    for sb in range(NBR_SUB):
        bias_ref = b_first_ref if sb == 0 else (b_last_ref if sb == NBR_SUB - 1 else b_mid_ref)
        r0 = (step * NBR_SUB + sb) * NBR_ROWS
        u0 = jnp.clip(r0 - B_WIN_ROWS // 2, 0, n_rows - NBR_UNION)
        kstart = pl.multiple_of(u0 * GRID_W, GRID_W)
        k = k_ref[pl.ds(kstart, nk), :]
        v = v_ref[pl.ds(kstart, nk), :]
        q = q_ref[sb * nq:(sb + 1) * nq, :]
        s = lax.dot_general(q, k, (((1,), (1,)), ((), ())), preferred_element_type=F32) + bias_ref[...]
        m = jnp.max(s, axis=1, keepdims=True)
        p = jnp.exp2(s - m)
        pv, psum = _weighted_values(p, v)
        o_ref[sb * nq:(sb + 1) * nq, :] = (pv / psum).astype(o_ref.dtype)


def _nbr_attention(p_arr, bias, *, seq_len):
    n_rows = seq_len // GRID_W
    tq = NBR_SUB * NBR_ROWS * GRID_W
    n_steps = seq_len // tq
    qc, kc, vc = (PLAIN_OFF[n] // HEAD_DIM for n in ("bq", "bk", "bv"))
    tile = (None, None, NBR_ROWS * GRID_W, NBR_UNION * GRID_W)
    return pl.pallas_call(
        functools.partial(_nbr_kernel, n_rows=n_rows),
        out_shape=jax.ShapeDtypeStruct((seq_len, BRANCH_WIDTH), BF16),
        grid=(BRANCH_WIDTH // HEAD_DIM, n_steps),
        in_specs=[pl.BlockSpec((tq, HEAD_DIM), lambda h, i: (i, qc + h)),
                  pl.BlockSpec((seq_len, HEAD_DIM), lambda h, i: (0, kc + h)),
                  pl.BlockSpec((seq_len, HEAD_DIM), lambda h, i: (0, vc + h)),
                  pl.BlockSpec(tile, lambda h, i: (h, jnp.where(i == 0, 0, 1), 0, 0)),
                  pl.BlockSpec(tile, lambda h, i: (h, 1, 0, 0)),
                  pl.BlockSpec(tile, lambda h, i: (h, jnp.where(i == n_steps - 1, 2, 1), 0, 0))],
        out_specs=pl.BlockSpec((tq, HEAD_DIM), lambda h, i: (i, h)),
        compiler_params=_params(("arbitrary", "arbitrary")),
        name="nbr_attention",
    )(p_arr, p_arr, p_arr, bias, bias, bias)


DENSE_RB = 256


def _dense_kernel(q_ref, k_ref, v_ref, o_ref, q2_ref, v2_ref, m_ref, acc_ref, s0_ref, *, seq_len, tq, tk):
    @pl.when(pl.program_id(1) == 0)
    def _():
        v2_ref[:, :HEAD_DIM] = v_ref[...]
        v2_ref[:, HEAD_DIM:] = jnp.ones((seq_len, HEAD_DIM), BF16)

    q2_ref[:tq] = q_ref[:, :HEAD_DIM]
    q2_ref[tq:] = q_ref[:, HEAD_DIM:]
    m_ref[...] = jnp.full(m_ref.shape, NEG_BIG, F32)
    acc_ref[...] = jnp.zeros(acc_ref.shape, F32)
    n_chunks = seq_len // tk

    def logits(r, k):
        return lax.dot_general(q2_ref[pl.ds(r * DENSE_RB, DENSE_RB), :], k, (((1,), (1,)), ((), ())),
                               preferred_element_type=F32)

    s0_ref[...] = logits(0, k_ref[pl.ds(0, tk), :])

    def body(j, carry):
        koff = pl.multiple_of(j * tk, tk)
        k = k_ref[pl.ds(koff, tk), :]
        v2 = v2_ref[pl.ds(koff, tk), :]
        for r in range((2 * tq) // DENSE_RB):
            rows = pl.ds(r * DENSE_RB, DENSE_RB)
            s = s0_ref[...] if r == 0 else logits(r, k)
            m_prev = m_ref[rows, :]
            m_new = jnp.maximum(m_prev, jnp.max(s, axis=1, keepdims=True))
            alpha = jnp.exp2(m_prev - m_new)
            p = jnp.exp2(s - m_new).astype(BF16)
            acc_ref[rows, :] = acc_ref[rows, :] * alpha + jnp.dot(p, v2, preferred_element_type=F32)
            m_ref[rows, :] = m_new
        next_off = pl.multiple_of(jnp.minimum(j + 1, n_chunks - 1) * tk, tk)
        s0_ref[...] = logits(0, k_ref[pl.ds(next_off, tk), :])
        return carry

    lax.fori_loop(0, n_chunks, body, 0)
    acc = acc_ref[...]
    o = acc[:, :HEAD_DIM] / acc[:, HEAD_DIM:]
    o_ref[:, :HEAD_DIM] = o[:tq].astype(o_ref.dtype)
    o_ref[:, HEAD_DIM:] = o[tq:].astype(o_ref.dtype)


def _dense_attention(n_arr, p_arr, *, seq_len, tq=2048, tk=2048):
    tq, tk = min(tq, seq_len), min(tk, seq_len)
    kc = NORM_OFF["dk"] // HEAD_DIM
    vc = PLAIN_OFF["dv"] // HEAD_DIM
    return pl.pallas_call(
        functools.partial(_dense_kernel, seq_len=seq_len, tq=tq, tk=tk),
        out_shape=jax.ShapeDtypeStruct((seq_len, BRANCH_WIDTH), BF16),
        grid=(2, seq_len // tq),
        in_specs=[pl.BlockSpec((tq, 2 * HEAD_DIM), lambda g, i: (i, g)),
                  pl.BlockSpec((seq_len, HEAD_DIM), lambda g, i: (0, kc + g), pipeline_mode=pl.Buffered(1)),
                  pl.BlockSpec((seq_len, HEAD_DIM), lambda g, i: (0, vc + g), pipeline_mode=pl.Buffered(1))],
        out_specs=pl.BlockSpec((tq, 2 * HEAD_DIM), lambda g, i: (i, g)),
        scratch_shapes=[pltpu.VMEM((2 * tq, HEAD_DIM), BF16), pltpu.VMEM((seq_len, 2 * HEAD_DIM), BF16),
                        pltpu.VMEM((2 * tq, 1), F32), pltpu.VMEM((2 * tq, 2 * HEAD_DIM), F32),
                        pltpu.VMEM((DENSE_RB, tk), F32)],
        compiler_params=_params(("arbitrary", "arbitrary")),
        name="dense_attention",
    )(n_arr, n_arr, p_arr)


def _merge_kernel(ya_ref, yb_ref, yc_ref, yd_ref, z_ref, gate_ref, x_ref, wb_ref, wo_ref, g_ref, *out_refs, final):
    merged = None
    for n, y_ref in enumerate((ya_ref, yb_ref, yc_ref, yd_ref)):
        z = z_ref[:, n * BRANCH_WIDTH:(n + 1) * BRANCH_WIDTH].astype(F32)
        t = (y_ref[...].astype(F32) * (z * _sigmoid(z))).astype(BF16)
        br = jnp.dot(t, wb_ref[n], preferred_element_type=F32)
        term = gate_ref[:, n * D_MODEL:(n + 1) * D_MODEL].astype(F32) * br
        merged = term if merged is None else merged + term
    x_new = x_ref[...] + jnp.dot(merged.astype(BF16), wo_ref[...], preferred_element_type=F32)
    normed = x_new * lax.rsqrt(jnp.mean(x_new * x_new, axis=-1, keepdims=True) + NORM_EPS) * g_ref[...]
    if final:
        out_refs[0][...] = normed
    else:
        out_refs[0][...] = x_new
        out_refs[1][...] = normed.astype(BF16)


def _merge(ys, p_arr, gates, x, wb, wo, g_next, *, final, tm=256):
    s = x.shape[0]
    row = lambda w: pl.BlockSpec((tm, w), lambda i: (i, 0))
    once = dict(pipeline_mode=pl.Buffered(1))
    in_specs = ([row(BRANCH_WIDTH)] * N_BRANCH
                + [row(N_BRANCH * BRANCH_WIDTH), row(N_BRANCH * D_MODEL), row(D_MODEL),
                   pl.BlockSpec((N_BRANCH, BRANCH_WIDTH, D_MODEL), lambda i: (0, 0, 0), **once),
                   pl.BlockSpec((D_MODEL, D_MODEL), lambda i: (0, 0), **once),
                   pl.BlockSpec((1, D_MODEL), lambda i: (0, 0))])
    if final:
        out_shape = [jax.ShapeDtypeStruct((s, D_MODEL), F32)]
    else:
        out_shape = [jax.ShapeDtypeStruct((s, D_MODEL), F32), jax.ShapeDtypeStruct((s, D_MODEL), BF16)]
    return pl.pallas_call(
        functools.partial(_merge_kernel, final=final),
        out_shape=out_shape,
        grid=(s // tm,),
        in_specs=in_specs,
        out_specs=[row(D_MODEL)] * len(out_shape),
        compiler_params=_params(("arbitrary",)),
        name="merge",
    )(*ys, p_arr, gates, x, wb, wo, g_next.reshape(1, D_MODEL))


@functools.lru_cache(maxsize=None)
def _rope_tables(seq_len):
    t = np.arange(seq_len)

    def angles(pos, dim):
        inv_freq = np.float32(ROPE_THETA) ** (-np.arange(dim // 2, dtype=np.float32) / np.float32(dim // 2))
        return pos.astype(np.float32)[:, None] * inv_freq.astype(np.float32)[None, :]

    a1 = angles(t, HEAD_DIM)
    cos1 = np.concatenate([np.cos(a1), np.cos(a1)], axis=1)
    sin1 = np.concatenate([-np.sin(a1), np.sin(a1)], axis=1)
    ar = angles(t // GRID_W, HEAD_DIM // 2)
    ac = angles(t % GRID_W, HEAD_DIM // 2)
    cos2 = np.concatenate([np.cos(ar), np.cos(ac), np.cos(ar), np.cos(ac)], axis=1)
    sin2 = np.concatenate([-np.sin(ar), -np.sin(ac), np.sin(ar), np.sin(ac)], axis=1)
    assert all(a.dtype == np.float32 for a in (cos1, sin1, cos2, sin2))
    return (cos1, sin1), (cos2, sin2)


AXIAL_PERM = np.concatenate([np.arange(0, 32), np.arange(64, 96), np.arange(32, 64), np.arange(96, 128)])


def _gather_cols(w, seg, head_perm=None):
    cols = np.concatenate([np.arange(_OFF[n], _OFF[n] + _WID[n]) for n in seg])
    if head_perm is not None:
        cols = cols.reshape(-1, HEAD_DIM)[:, head_perm].reshape(-1)
    if np.all(np.diff(cols) == 1):
        return w[:, cols[0]:cols[-1] + 1].astype(BF16)
    runs = np.split(cols, np.nonzero(np.diff(cols) != 1)[0] + 1)
    return jnp.concatenate([w[:, r[0]:r[-1] + 1] for r in runs], axis=1).astype(BF16)


def _col_scale(seg, scaled):
    return jnp.asarray(np.concatenate(
        [np.full((_WID[n],), scaled.get(n, 1.0), np.float32) for n in seg]).reshape(1, -1))


def _layer(x, h, w_in_all, layer, norm_gain, gate_b, a_sink, b_rpb, d_q_norm, d_k_norm, w_branch, w_out, g_next,
           tables, final):
    s = x.shape[0]
    (cos1, sin1), (cos2, sin2) = tables
    tm = 1024
    w_in = w_in_all[layer, :, :_OFF["gates"]]

    gains = jnp.concatenate([jnp.tile(d_q_norm[AXIAL_PERM], _WID["dq"] // HEAD_DIM),
                             jnp.tile(d_k_norm[AXIAL_PERM], _WID["dk"] // HEAD_DIM)])
    n_arr = _proj(x if h is None else h, _gather_cols(w_in, NORM_SEG, AXIAL_PERM), "norm_rope",
                  [_col_scale(NORM_SEG, {"dq": SCALE}), gains.reshape(1, -1)],
                  [cos2, sin2], tm=tm, tn=NORM_W, name="proj_norm_rope", norm_gain=norm_gain if h is None else None)
    if h is None:
        n_arr, h = n_arr
    r_arr = _proj(h, _gather_cols(w_in, ROPE_SEG), "rope", [_col_scale(ROPE_SEG, {"aq": SCALE, "cq": SCALE})],
                  [cos1, sin1], tm=tm, tn=ROPE_W // 2, name="proj_rope")
    p_arr = _proj(h, _gather_cols(w_in, PLAIN_SEG), "scale", [_col_scale(PLAIN_SEG, {"bq": SCALE})], [],
                  tm=tm, tn=PLAIN_W // 3, name="proj_plain")
    gates = _proj_gates(h, w_in_all, layer, gate_b.reshape(1, -1))

    ya = _window_attention(r_arr, p_arr, a_sink, seq_len=s)
    yb = _nbr_attention(p_arr, _nbr_bias(b_rpb), seq_len=s)
    yc = _dilated_attention(r_arr, p_arr, seq_len=s)
    yd = _dense_attention(n_arr, p_arr, seq_len=s)
    return _merge((ya, yb, yc, yd), p_arr, gates, x, w_branch.astype(BF16), w_out.astype(BF16), g_next, final=final)


def kernel(x, norm_g, w_in, gate_b, a_sink, b_rpb, d_q_norm, d_k_norm, w_branch, w_out, final_norm_g):
    b, s, d = x.shape
    depth = w_in.shape[0]
    tables = tuple(tuple(jnp.asarray(a) for a in grp) for grp in _rope_tables(s))
    outs = []
    for bi in range(b):
        xb = x.reshape(s, d) if b == 1 else x[bi]
        h = None
        for layer in range(depth):
            final = layer == depth - 1
            g_next = final_norm_g if final else norm_g[layer + 1]
            res = _layer(xb, h, w_in, layer, norm_g[layer], gate_b[layer], a_sink[layer], b_rpb[layer], d_q_norm[layer],
                         d_k_norm[layer], w_branch[layer], w_out[layer], g_next, tables, final)
            if final:
                xb = res[0]
            else:
                xb, h = res
        outs.append(xb)
    return outs[0].reshape(1, s, d) if b == 1 else jnp.stack(outs, axis=0)
```

```python
import functools
import math

import numpy as np
import jax
import jax.numpy as jnp
from jax import lax
from jax.experimental import pallas as pl
from jax.experimental.pallas import tpu as pltpu

F32 = jnp.float32
BF16 = jnp.bfloat16

D_MODEL = 2048
HEAD_DIM = 128
BRANCH_WIDTH = 512
N_BRANCH = 4
A_HALF_WINDOW = 128
B_WIN_ROWS = 8
B_WIN_COLS = 16
C_PATTERNS = ((128, 1), (512, 4), (2048, 16))
GRID_W = 64
ROPE_THETA = 10000.0
NORM_EPS = 1e-6
LOG2E = math.log2(math.e)
SCALE = HEAD_DIM ** -0.5 * LOG2E
NEG_BIG = -1e30

LANE = 128
VMEM_LIMIT = 56 * 1024 * 1024

_SPLITS = (512, 256, 256, 512, 512, 512, 512, 512, 1536, 512, 512, 512, 512, 256, 256, 512, 8192)
_NAMES = ("aq", "ak", "av", "az", "bq", "bk", "bv", "bz", "cq", "ck", "cv", "cz", "dq", "dk", "dv", "dz", "gates")
_OFF = dict(zip(_NAMES, np.concatenate([[0], np.cumsum(_SPLITS)[:-1]]).tolist()))
_WID = dict(zip(_NAMES, _SPLITS))

ROPE_SEG = ("aq", "ak", "cq", "ck")
NORM_SEG = ("dq", "dk")
PLAIN_SEG = ("az", "bz", "cz", "dz", "av", "dv", "bq", "bk", "bv", "cv")


def _seg_offsets(seg):
    offs, o = {}, 0
    for n in seg:
        offs[n] = o
        o += _WID[n]
    return offs, o


ROPE_OFF, ROPE_W = _seg_offsets(ROPE_SEG)
NORM_OFF, NORM_W = _seg_offsets(NORM_SEG)
PLAIN_OFF, PLAIN_W = _seg_offsets(PLAIN_SEG)


def _params(sem):
    return pltpu.CompilerParams(dimension_semantics=sem, vmem_limit_bytes=VMEM_LIMIT)


def _sigmoid(v):
    return 1.0 / (1.0 + jnp.exp(-v))


def _weighted_values(p, v):
    v1 = jnp.concatenate([v, jnp.ones_like(v)], axis=1)
    r = jnp.dot(p.astype(BF16), v1, preferred_element_type=F32)
    return r[:, :HEAD_DIM], r[:, HEAD_DIM:]


PROJ_RB = 256


def _rope_half(xh, cos, sin_signed):
    return xh * cos + pltpu.roll(xh, HEAD_DIM // 2, 1) * sin_signed


def _proj_kernel(*refs, mode, norm_input):
    if norm_input:
        g_in_ref, refs, h_out_ref = refs[0], refs[1:-1], refs[-1]
    h_ref, w_ref = refs[0], refs[1]
    o_ref = refs[-1]
    tm, tn = o_ref.shape
    for r in range(tm // PROJ_RB):
        rows = pl.ds(r * PROJ_RB, PROJ_RB)
        if norm_input:
            x = h_ref[rows, :]
            h = (x * lax.rsqrt(jnp.mean(x * x, axis=-1, keepdims=True) + NORM_EPS) * g_in_ref[...]).astype(BF16)
            h_out_ref[rows, :] = h
        else:
            h = h_ref[rows, :]
        acc = jnp.dot(h, w_ref[...], preferred_element_type=F32)
        if mode == "scale":
            o_ref[rows, :] = (acc * refs[2][...]).astype(o_ref.dtype)
        elif mode == "rope":
            cs_ref, cos_ref, sin_ref = refs[2:5]
            cos, sin = cos_ref[rows, :], sin_ref[rows, :]
            for c in range(tn // HEAD_DIM):
                sl = slice(c * HEAD_DIM, (c + 1) * HEAD_DIM)
                o_ref[rows, sl] = (_rope_half(acc[:, sl], cos, sin) * cs_ref[:, sl]).astype(o_ref.dtype)
        elif mode == "norm_rope":
            cs_ref, g_ref, cos_ref, sin_ref = refs[2:6]
            cos, sin = cos_ref[rows, :], sin_ref[rows, :]
            for c in range(tn // HEAD_DIM):
                sl = slice(c * HEAD_DIM, (c + 1) * HEAD_DIM)
                xh = acc[:, sl]
                y = xh * lax.rsqrt(jnp.mean(xh * xh, axis=-1, keepdims=True) + NORM_EPS) * g_ref[:, sl]
                o_ref[rows, sl] = (_rope_half(y, cos, sin) * cs_ref[:, sl]).astype(o_ref.dtype)
        else:
            raise ValueError(mode)


def _proj(h, w, mode, row_vecs, tables, tm, tn, name, norm_gain=None):
    s, d = h.shape
    n = w.shape[1]
    tm = min(tm, s)
    in_specs = [pl.BlockSpec((tm, d), lambda i, j: (i, 0)), pl.BlockSpec((d, tn), lambda i, j: (0, j))]
    in_specs += [pl.BlockSpec((1, tn), lambda i, j: (0, j)) for _ in row_vecs]
    in_specs += [pl.BlockSpec((tm, HEAD_DIM), lambda i, j: (i, 0)) for _ in tables]
    out_shape = [jax.ShapeDtypeStruct((s, n), BF16)]
    out_specs = [pl.BlockSpec((tm, tn), lambda i, j: (i, j))]
    args = [h, w, *row_vecs, *tables]
    if norm_gain is not None:
        assert n == tn
        in_specs = [pl.BlockSpec((1, d), lambda i, j: (0, 0))] + in_specs
        args = [norm_gain.reshape(1, d)] + args
        out_shape.append(jax.ShapeDtypeStruct((s, d), BF16))
        out_specs.append(pl.BlockSpec((tm, d), lambda i, j: (i, 0)))
    outs = pl.pallas_call(
        functools.partial(_proj_kernel, mode=mode, norm_input=norm_gain is not None),
        out_shape=out_shape,
        grid=(s // tm, n // tn),
        in_specs=in_specs,
        out_specs=out_specs,
        compiler_params=_params(("arbitrary", "arbitrary")),
        name=name,
    )(*args)
    return outs[0] if norm_gain is None else outs


def _gates_kernel(h_ref, w_ref, b_ref, o_ref, wb_ref):
    @pl.when(pl.program_id(1) == 0)
    def _():
        wb_ref[...] = w_ref[...].astype(BF16)

    for r in range(o_ref.shape[0] // PROJ_RB):
        rows = pl.ds(r * PROJ_RB, PROJ_RB)
        acc = jnp.dot(h_ref[rows, :], wb_ref[...], preferred_element_type=F32)
        o_ref[rows, :] = _sigmoid(acc + b_ref[...]).astype(o_ref.dtype)


def _proj_gates(h, w_in_all, layer, bias, tm=2048, tn=1024):
    s, d = h.shape
    tm = min(tm, s)
    n = _WID["gates"]
    first = _OFF["gates"] // tn
    return pl.pallas_call(
        _gates_kernel,
        out_shape=jax.ShapeDtypeStruct((s, n), BF16),
        grid=(n // tn, s // tm),
        in_specs=[pl.BlockSpec((tm, d), lambda j, i: (i, 0)),
                  pl.BlockSpec((None, d, tn), lambda j, i: (layer, 0, first + j)),
                  pl.BlockSpec((1, tn), lambda j, i: (0, j))],
        out_specs=pl.BlockSpec((tm, tn), lambda j, i: (i, j)),
        scratch_shapes=[pltpu.VMEM((d, tn), BF16)],
        compiler_params=_params(("arbitrary", "arbitrary")),
        name="proj_gates",
    )(h, w_in_all, bias)


WIN_BQ = 128


def _window_kernel(sink_ref, q_ref, k_ref, v_ref, o_ref, *, seq_len, nsub):
    kv_head = pl.program_id(0)
    step = pl.program_id(1)
    bq, hw, groups = WIN_BQ, A_HALF_WINDOW, 2
    span = bq + 2 * hw
    rows = groups * bq
    row = lax.broadcasted_iota(jnp.int32, (rows, span), 0)
    col_minus_row = lax.broadcasted_iota(jnp.int32, (rows, span), 1) - jnp.where(row >= bq, row - bq, row)
    r1 = lax.broadcasted_iota(jnp.int32, (rows, 1), 0)
    sink = jnp.where(r1 >= bq, sink_ref[kv_head * groups + 1], sink_ref[kv_head * groups]) * LOG2E

    def body(b, carry):
        q0 = (step * nsub + b) * bq
        start = pl.multiple_of(jnp.clip(q0 - hw, 0, seq_len - span), bq)
        k = k_ref[pl.ds(start, span), :]
        v = v_ref[pl.ds(start, span), :]
        qoff = pl.multiple_of(b * bq, bq)
        qb = q_ref[pl.ds(qoff, bq), :]
        q2 = jnp.concatenate([qb[:, :HEAD_DIM], qb[:, HEAD_DIM:]], axis=0)
        s = lax.dot_general(q2, k, (((1,), (1,)), ((), ())), preferred_element_type=F32)
        s = jnp.where(jnp.abs(col_minus_row + (start - q0)) <= hw, s, NEG_BIG)
        m = jnp.maximum(jnp.max(s, axis=1, keepdims=True), sink)
        p = jnp.exp2(s - m)
        pv, psum = _weighted_values(p, v)
        o = pv / (psum + jnp.exp2(sink - m))
        o_ref[pl.ds(qoff, bq), :HEAD_DIM] = o[:bq].astype(o_ref.dtype)
        o_ref[pl.ds(qoff, bq), HEAD_DIM:] = o[bq:].astype(o_ref.dtype)
        return carry

    lax.fori_loop(0, nsub, body, 0, unroll=True)


def _window_attention(r_arr, p_arr, sink, *, seq_len, max_step=1024):
    nsub = min(max_step, seq_len) // WIN_BQ
    tq = nsub * WIN_BQ
    qc, kc, vc = ROPE_OFF["aq"] // (2 * HEAD_DIM), ROPE_OFF["ak"] // HEAD_DIM, PLAIN_OFF["av"] // HEAD_DIM
    return pl.pallas_call(
        functools.partial(_window_kernel, seq_len=seq_len, nsub=nsub),
        out_shape=jax.ShapeDtypeStruct((seq_len, BRANCH_WIDTH), BF16),
        grid=(2, seq_len // tq),
        in_specs=[pl.BlockSpec(memory_space=pltpu.SMEM),
                  pl.BlockSpec((tq, 2 * HEAD_DIM), lambda c, i: (i, qc + c)),
                  pl.BlockSpec((seq_len, HEAD_DIM), lambda c, i: (0, kc + c)),
                  pl.BlockSpec((seq_len, HEAD_DIM), lambda c, i: (0, vc + c))],
        out_specs=pl.BlockSpec((tq, 2 * HEAD_DIM), lambda c, i: (i, c)),
        compiler_params=_params(("arbitrary", "arbitrary")),
        name="window_attention",
    )(sink, r_arr, r_arr, p_arr)


DIL_BQ = 128
DIL_HALF = 64
DIL_HALO = DIL_HALF * max(d for _, d in C_PATTERNS)


def _dilated_kernel(q0_ref, q1_ref, q2_ref, kp_ref, kc_ref, kn_ref, vp_ref, vc_ref, vn_ref, o_ref,
                    qf_ref, kw_ref, vw_ref, og_ref, lg_ref, *, seq_len, tq):
    t0 = pl.program_id(1) * tq
    halo, bq, half = DIL_HALO, DIL_BQ, DIL_HALF
    span = bq + 2 * half
    for w_ref, prev, cur, nxt in ((kw_ref, kp_ref, kc_ref, kn_ref), (vw_ref, vp_ref, vc_ref, vn_ref)):
        w_ref[:halo] = prev[...].astype(F32)
        w_ref[halo:halo + tq] = cur[...].astype(F32)
        w_ref[halo + tq:] = nxt[...].astype(F32)
    for g, q_ref in enumerate((q0_ref, q1_ref, q2_ref)):
        qf_ref[g] = q_ref[...].astype(F32)

    ii = lax.broadcasted_iota(jnp.int32, (bq, span), 0)
    jj = lax.broadcasted_iota(jnp.int32, (bq, span), 1)
    in_band = jnp.abs(jj - half - ii) <= half

    for g, (_, dil) in enumerate(C_PATTERNS):
        nb = tq // (bq * dil)

        def body(it, carry, g=g, dil=dil, nb=nb):
            q_row = it // nb + dil * bq * (it % nb)
            w_row = halo + q_row - half * dil
            q = qf_ref[g, pl.ds(q_row, bq, stride=dil), :].astype(BF16)
            k = kw_ref[pl.ds(w_row, span, stride=dil), :].astype(BF16)
            v = vw_ref[pl.ds(w_row, span, stride=dil), :].astype(BF16)
            s = lax.dot_general(q, k, (((1,), (1,)), ((), ())), preferred_element_type=F32)
            ktok = t0 + q_row - half * dil + dil * jj
            s = jnp.where(in_band & (ktok >= 0) & (ktok < seq_len), s, NEG_BIG)
            m = jnp.max(s, axis=1, keepdims=True)
            p = jnp.exp2(s - m)
            pv, psum = _weighted_values(p, v)
            og_ref[g, pl.ds(q_row, bq, stride=dil), :] = pv / psum
            lg_ref[g, pl.ds(q_row, bq, stride=dil), :] = m + jnp.log2(psum)
            return carry

        lax.fori_loop(0, tq // bq, body, 0, unroll=True)

    l0, l1, l2 = lg_ref[0], lg_ref[1], lg_ref[2]
    mx = jnp.maximum(jnp.maximum(l0, l1), l2)
    e0, e1, e2 = jnp.exp2(l0 - mx), jnp.exp2(l1 - mx), jnp.exp2(l2 - mx)
    o_ref[...] = ((e0 * og_ref[0] + e1 * og_ref[1] + e2 * og_ref[2]) / (e0 + e1 + e2)).astype(o_ref.dtype)


def _dilated_attention(r_arr, p_arr, *, seq_len, tq=2048):
    tq = min(tq, seq_len)
    halo = DIL_HALO
    ratio, n_halo = tq // halo, seq_len // halo
    qc, kc, vc = ROPE_OFF["cq"] // HEAD_DIM, ROPE_OFF["ck"] // HEAD_DIM, PLAIN_OFF["cv"] // HEAD_DIM
    n_heads = BRANCH_WIDTH // HEAD_DIM

    def window_specs(col):
        return [pl.BlockSpec((halo, HEAD_DIM), lambda h, i: (jnp.maximum(i * ratio - 1, 0), col + h)),
                pl.BlockSpec((tq, HEAD_DIM), lambda h, i: (i, col + h)),
                pl.BlockSpec((halo, HEAD_DIM), lambda h, i: (jnp.minimum((i + 1) * ratio, n_halo - 1), col + h))]

    q_specs = [pl.BlockSpec((tq, HEAD_DIM), lambda h, i, g=g: (i, qc + n_heads * g + h)) for g in range(3)]
    return pl.pallas_call(
        functools.partial(_dilated_kernel, seq_len=seq_len, tq=tq),
        out_shape=jax.ShapeDtypeStruct((seq_len, BRANCH_WIDTH), BF16),
        grid=(n_heads, seq_len // tq),
        in_specs=q_specs + window_specs(kc) + window_specs(vc),
        out_specs=pl.BlockSpec((tq, HEAD_DIM), lambda h, i: (i, h)),
        scratch_shapes=[pltpu.VMEM((3, tq, HEAD_DIM), F32), pltpu.VMEM((tq + 2 * halo, HEAD_DIM), F32),
                        pltpu.VMEM((tq + 2 * halo, HEAD_DIM), F32), pltpu.VMEM((3, tq, HEAD_DIM), F32),
                        pltpu.VMEM((3, tq, HEAD_DIM), F32)],
        compiler_params=_params(("arbitrary", "arbitrary")),
        name="dilated_attention",
    )(r_arr, r_arr, r_arr, r_arr, r_arr, r_arr, p_arr, p_arr, p_arr)


NBR_ROWS = 4
NBR_UNION = 12
NBR_SUB = 16


def _nbr_bias_kernel(rpb_ref, o_ref, t_ref):
    h = pl.program_id(0)
    n_dr, n_dc = 2 * B_WIN_ROWS - 1, 2 * B_WIN_COLS - 1
    qc = lax.broadcasted_iota(jnp.int32, (GRID_W, GRID_W), 0)
    kc = lax.broadcasted_iota(jnp.int32, (GRID_W, GRID_W), 1)
    col_start = jnp.clip(qc - B_WIN_COLS // 2, 0, GRID_W - B_WIN_COLS)
    col_ok = (kc >= col_start) & (kc < col_start + B_WIN_COLS)
    dc = kc - qc + (B_WIN_COLS - 1)
    for dr in range(n_dr):
        t = jnp.full((GRID_W, GRID_W), NEG_BIG, F32)
        for d in range(n_dc):
            t = jnp.where(dc == d, rpb_ref[(h * n_dr + dr) * n_dc + d] * LOG2E, t)
        t_ref[dr] = jnp.where(col_ok, t, NEG_BIG)
    o_ref[...] = jnp.full(o_ref.shape, NEG_BIG, F32)
    half = B_WIN_ROWS // 2
    for case in range(3):
        for a in range(NBR_ROWS):
            win, off = ((0, a), (a, half), (NBR_UNION - B_WIN_ROWS, half + a))[case]
            for j in range(B_WIN_ROWS):
                o_ref[case, a * GRID_W:(a + 1) * GRID_W, (win + j) * GRID_W:(win + j + 1) * GRID_W] = (
                    t_ref[j - off + B_WIN_ROWS - 1])


def _nbr_bias(rpb):
    n_heads = rpb.shape[0]
    shape = (3, NBR_ROWS * GRID_W, NBR_UNION * GRID_W)
    return pl.pallas_call(
        _nbr_bias_kernel,
        out_shape=jax.ShapeDtypeStruct((n_heads,) + shape, F32),
        grid=(n_heads,),
        in_specs=[pl.BlockSpec(memory_space=pltpu.SMEM)],
        out_specs=pl.BlockSpec((None,) + shape, lambda h: (h, 0, 0, 0)),
        scratch_shapes=[pltpu.VMEM((2 * B_WIN_ROWS - 1, GRID_W, GRID_W), F32)],
        compiler_params=_params(("arbitrary",)),
        name="nbr_bias",
    )(rpb.reshape(-1))


def _nbr_kernel(q_ref, k_ref, v_ref, b_first_ref, b_mid_ref, b_last_ref, o_ref, *, n_rows):
    step = pl.program_id(1)
    nq, nk = NBR_ROWS * GRID_W, NBR_UNION * GRID_W
    for sb in range(NBR_SUB):
        bias_ref = b_first_ref if sb == 0 else (b_last_ref if sb == NBR_SUB - 1 else b_mid_ref)
        r0 = (step * NBR_SUB + sb) * NBR_ROWS
        u0 = jnp.clip(r0 - B_WIN_ROWS // 2, 0, n_rows - NBR_UNION)
        kstart = pl.multiple_of(u0 * GRID_W, GRID_W)
        k = k_ref[pl.ds(kstart, nk), :]
        v = v_ref[pl.ds(kstart, nk), :]
        q = q_ref[sb * nq:(sb + 1) * nq, :]
        s = lax.dot_general(q, k, (((1,), (1,)), ((), ())), preferred_element_type=F32) + bias_ref[...]
        m = jnp.max(s, axis=1, keepdims=True)
        p = jnp.exp2(s - m)
        pv, psum = _weighted_values(p, v)
        o_ref[sb * nq:(sb + 1) * nq, :] = (pv / psum).astype(o_ref.dtype)


def _nbr_attention(p_arr, bias, *, seq_len):
    n_rows = seq_len // GRID_W
    tq = NBR_SUB * NBR_ROWS * GRID_W
    n_steps = seq_len // tq
    qc, kc, vc = (PLAIN_OFF[n] // HEAD_DIM for n in ("bq", "bk", "bv"))
    tile = (None, None, NBR_ROWS * GRID_W, NBR_UNION * GRID_W)
    return pl.pallas_call(
        functools.partial(_nbr_kernel, n_rows=n_rows),
        out_shape=jax.ShapeDtypeStruct((seq_len, BRANCH_WIDTH), BF16),
        grid=(BRANCH_WIDTH // HEAD_DIM, n_steps),
        in_specs=[pl.BlockSpec((tq, HEAD_DIM), lambda h, i: (i, qc + h)),
                  pl.BlockSpec((seq_len, HEAD_DIM), lambda h, i: (0, kc + h)),
                  pl.BlockSpec((seq_len, HEAD_DIM), lambda h, i: (0, vc + h)),
                  pl.BlockSpec(tile, lambda h, i: (h, jnp.where(i == 0, 0, 1), 0, 0)),
                  pl.BlockSpec(tile, lambda h, i: (h, 1, 0, 0)),
                  pl.BlockSpec(tile, lambda h, i: (h, jnp.where(i == n_steps - 1, 2, 1), 0, 0))],
        out_specs=pl.BlockSpec((tq, HEAD_DIM), lambda h, i: (i, h)),
        compiler_params=_params(("arbitrary", "arbitrary")),
        name="nbr_attention",
    )(p_arr, p_arr, p_arr, bias, bias, bias)


DENSE_RB = 256


def _dense_kernel(q_ref, k_ref, v_ref, o_ref, q2_ref, v2_ref, m_ref, acc_ref, s0_ref, *, seq_len, tq, tk):
    @pl.when(pl.program_id(1) == 0)
    def _():
        v2_ref[:, :HEAD_DIM] = v_ref[...]
        v2_ref[:, HEAD_DIM:] = jnp.ones((seq_len, HEAD_DIM), BF16)

    q2_ref[:tq] = q_ref[:, :HEAD_DIM]
    q2_ref[tq:] = q_ref[:, HEAD_DIM:]
    m_ref[...] = jnp.full(m_ref.shape, NEG_BIG, F32)
    acc_ref[...] = jnp.zeros(acc_ref.shape, F32)
    n_chunks = seq_len // tk

    def logits(r, k):
        return lax.dot_general(q2_ref[pl.ds(r * DENSE_RB, DENSE_RB), :], k, (((1,), (1,)), ((), ())),
                               preferred_element_type=F32)

    s0_ref[...] = logits(0, k_ref[pl.ds(0, tk), :])

    def body(j, carry):
        koff = pl.multiple_of(j * tk, tk)
        k = k_ref[pl.ds(koff, tk), :]
        v2 = v2_ref[pl.ds(koff, tk), :]
        for r in range((2 * tq) // DENSE_RB):
            rows = pl.ds(r * DENSE_RB, DENSE_RB)
            s = s0_ref[...] if r == 0 else logits(r, k)
            m_prev = m_ref[rows, :]
            m_new = jnp.maximum(m_prev, jnp.max(s, axis=1, keepdims=True))
            alpha = jnp.exp2(m_prev - m_new)
            p = jnp.exp2(s - m_new).astype(BF16)
            acc_ref[rows, :] = acc_ref[rows, :] * alpha + jnp.dot(p, v2, preferred_element_type=F32)
            m_ref[rows, :] = m_new
        next_off = pl.multiple_of(jnp.minimum(j + 1, n_chunks - 1) * tk, tk)
        s0_ref[...] = logits(0, k_ref[pl.ds(next_off, tk), :])
        return carry

    lax.fori_loop(0, n_chunks, body, 0)
    acc = acc_ref[...]
    o = acc[:, :HEAD_DIM] / acc[:, HEAD_DIM:]
    o_ref[:, :HEAD_DIM] = o[:tq].astype(o_ref.dtype)
    o_ref[:, HEAD_DIM:] = o[tq:].astype(o_ref.dtype)


def _dense_attention(n_arr, p_arr, *, seq_len, tq=2048, tk=2048):
    tq, tk = min(tq, seq_len), min(tk, seq_len)
    kc = NORM_OFF["dk"] // HEAD_DIM
    vc = PLAIN_OFF["dv"] // HEAD_DIM
    return pl.pallas_call(
        functools.partial(_dense_kernel, seq_len=seq_len, tq=tq, tk=tk),
        out_shape=jax.ShapeDtypeStruct((seq_len, BRANCH_WIDTH), BF16),
        grid=(2, seq_len // tq),
        in_specs=[pl.BlockSpec((tq, 2 * HEAD_DIM), lambda g, i: (i, g)),
                  pl.BlockSpec((seq_len, HEAD_DIM), lambda g, i: (0, kc + g), pipeline_mode=pl.Buffered(1)),
                  pl.BlockSpec((seq_len, HEAD_DIM), lambda g, i: (0, vc + g), pipeline_mode=pl.Buffered(1))],
        out_specs=pl.BlockSpec((tq, 2 * HEAD_DIM), lambda g, i: (i, g)),
        scratch_shapes=[pltpu.VMEM((2 * tq, HEAD_DIM), BF16), pltpu.VMEM((seq_len, 2 * HEAD_DIM), BF16),
                        pltpu.VMEM((2 * tq, 1), F32), pltpu.VMEM((2 * tq, 2 * HEAD_DIM), F32),
                        pltpu.VMEM((DENSE_RB, tk), F32)],
        compiler_params=_params(("arbitrary", "arbitrary")),
        name="dense_attention",
    )(n_arr, n_arr, p_arr)


def _merge_kernel(ya_ref, yb_ref, yc_ref, yd_ref, z_ref, gate_ref, x_ref, wb_ref, wo_ref, g_ref, *out_refs, final):
    merged = None
    for n, y_ref in enumerate((ya_ref, yb_ref, yc_ref, yd_ref)):
        z = z_ref[:, n * BRANCH_WIDTH:(n + 1) * BRANCH_WIDTH].astype(F32)
        t = (y_ref[...].astype(F32) * (z * _sigmoid(z))).astype(BF16)
        br = jnp.dot(t, wb_ref[n], preferred_element_type=F32)
        term = gate_ref[:, n * D_MODEL:(n + 1) * D_MODEL].astype(F32) * br
        merged = term if merged is None else merged + term
    x_new = x_ref[...] + jnp.dot(merged.astype(BF16), wo_ref[...], preferred_element_type=F32)
    normed = x_new * lax.rsqrt(jnp.mean(x_new * x_new, axis=-1, keepdims=True) + NORM_EPS) * g_ref[...]
    if final:
        out_refs[0][...] = normed
    else:
        out_refs[0][...] = x_new
        out_refs[1][...] = normed.astype(BF16)


def _merge(ys, p_arr, gates, x, wb, wo, g_next, *, final, tm=256):
    s = x.shape[0]
    row = lambda w: pl.BlockSpec((tm, w), lambda i: (i, 0))
    once = dict(pipeline_mode=pl.Buffered(1))
    in_specs = ([row(BRANCH_WIDTH)] * N_BRANCH
                + [row(N_BRANCH * BRANCH_WIDTH), row(N_BRANCH * D_MODEL), row(D_MODEL),
                   pl.BlockSpec((N_BRANCH, BRANCH_WIDTH, D_MODEL), lambda i: (0, 0, 0), **once),
                   pl.BlockSpec((D_MODEL, D_MODEL), lambda i: (0, 0), **once),
                   pl.BlockSpec((1, D_MODEL), lambda i: (0, 0))])
    if final:
        out_shape = [jax.ShapeDtypeStruct((s, D_MODEL), F32)]
    else:
        out_shape = [jax.ShapeDtypeStruct((s, D_MODEL), F32), jax.ShapeDtypeStruct((s, D_MODEL), BF16)]
    return pl.pallas_call(
        functools.partial(_merge_kernel, final=final),
        out_shape=out_shape,
        grid=(s // tm,),
        in_specs=in_specs,
        out_specs=[row(D_MODEL)] * len(out_shape),
        compiler_params=_params(("arbitrary",)),
        name="merge",
    )(*ys, p_arr, gates, x, wb, wo, g_next.reshape(1, D_MODEL))


@functools.lru_cache(maxsize=None)
def _rope_tables(seq_len):
    t = np.arange(seq_len)

    def angles(pos, dim):
        inv_freq = np.float32(ROPE_THETA) ** (-np.arange(dim // 2, dtype=np.float32) / np.float32(dim // 2))
        return pos.astype(np.float32)[:, None] * inv_freq.astype(np.float32)[None, :]

    a1 = angles(t, HEAD_DIM)
    cos1 = np.concatenate([np.cos(a1), np.cos(a1)], axis=1)
    sin1 = np.concatenate([-np.sin(a1), np.sin(a1)], axis=1)
    ar = angles(t // GRID_W, HEAD_DIM // 2)
    ac = angles(t % GRID_W, HEAD_DIM // 2)
    cos2 = np.concatenate([np.cos(ar), np.cos(ac), np.cos(ar), np.cos(ac)], axis=1)
    sin2 = np.concatenate([-np.sin(ar), -np.sin(ac), np.sin(ar), np.sin(ac)], axis=1)
    assert all(a.dtype == np.float32 for a in (cos1, sin1, cos2, sin2))
    return (cos1, sin1), (cos2, sin2)


AXIAL_PERM = np.concatenate([np.arange(0, 32), np.arange(64, 96), np.arange(32, 64), np.arange(96, 128)])


def _gather_cols(w, seg, head_perm=None):
    cols = np.concatenate([np.arange(_OFF[n], _OFF[n] + _WID[n]) for n in seg])
    if head_perm is not None:
        cols = cols.reshape(-1, HEAD_DIM)[:, head_perm].reshape(-1)
    if np.all(np.diff(cols) == 1):
        return w[:, cols[0]:cols[-1] + 1].astype(BF16)
    runs = np.split(cols, np.nonzero(np.diff(cols) != 1)[0] + 1)
    return jnp.concatenate([w[:, r[0]:r[-1] + 1] for r in runs], axis=1).astype(BF16)


def _col_scale(seg, scaled):
    return jnp.asarray(np.concatenate(
        [np.full((_WID[n],), scaled.get(n, 1.0), np.float32) for n in seg]).reshape(1, -1))


def _layer(x, h, w_in_all, layer, norm_gain, gate_b, a_sink, b_rpb, d_q_norm, d_k_norm, w_branch, w_out, g_next,
           tables, final):
    s = x.shape[0]
    (cos1, sin1), (cos2, sin2) = tables
    tm = 1024
    w_in = w_in_all[layer, :, :_OFF["gates"]]

    gains = jnp.concatenate([jnp.tile(d_q_norm[AXIAL_PERM], _WID["dq"] // HEAD_DIM),
                             jnp.tile(d_k_norm[AXIAL_PERM], _WID["dk"] // HEAD_DIM)])
    n_arr = _proj(x if h is None else h, _gather_cols(w_in, NORM_SEG, AXIAL_PERM), "norm_rope",
                  [_col_scale(NORM_SEG, {"dq": SCALE}), gains.reshape(1, -1)],
                  [cos2, sin2], tm=tm, tn=NORM_W, name="proj_norm_rope", norm_gain=norm_gain if h is None else None)
    if h is None:
        n_arr, h = n_arr
    r_arr = _proj(h, _gather_cols(w_in, ROPE_SEG), "rope", [_col_scale(ROPE_SEG, {"aq": SCALE, "cq": SCALE})],
                  [cos1, sin1], tm=tm, tn=ROPE_W // 2, name="proj_rope")
    p_arr = _proj(h, _gather_cols(w_in, PLAIN_SEG), "scale", [_col_scale(PLAIN_SEG, {"bq": SCALE})], [],
                  tm=tm, tn=PLAIN_W // 3, name="proj_plain")
    gates = _proj_gates(h, w_in_all, layer, gate_b.reshape(1, -1))

    ya = _window_attention(r_arr, p_arr, a_sink, seq_len=s)
    yb = _nbr_attention(p_arr, _nbr_bias(b_rpb), seq_len=s)
    yc = _dilated_attention(r_arr, p_arr, seq_len=s)
    yd = _dense_attention(n_arr, p_arr, seq_len=s)
    return _merge((ya, yb, yc, yd), p_arr, gates, x, w_branch.astype(BF16), w_out.astype(BF16), g_next, final=final)


def kernel(x, norm_g, w_in, gate_b, a_sink, b_rpb, d_q_norm, d_k_norm, w_branch, w_out, final_norm_g):
    b, s, d = x.shape
    depth = w_in.shape[0]
    tables = tuple(tuple(jnp.asarray(a) for a in grp) for grp in _rope_tables(s))
    outs = []
    for bi in range(b):
        xb = x.reshape(s, d) if b == 1 else x[bi]
        h = None
        for layer in range(depth):
            final = layer == depth - 1
            g_next = final_norm_g if final else norm_g[layer + 1]
            res = _layer(xb, h, w_in, layer, norm_g[layer], gate_b[layer], a_sink[layer], b_rpb[layer], d_q_norm[layer],
                         d_k_norm[layer], w_branch[layer], w_out[layer], g_next, tables, final)
            if final:
                xb = res[0]
            else:
                xb, h = res
        outs.append(xb)
    return outs[0].reshape(1, s, d) if b == 1 else jnp.stack(outs, axis=0)
```

```python
import functools
import math

import numpy as np
import jax
import jax.numpy as jnp
from jax import lax
from jax.experimental import pallas as pl
from jax.experimental.pallas import tpu as pltpu

F32 = jnp.float32
BF16 = jnp.bfloat16

D_MODEL = 2048
HEAD_DIM = 128
BRANCH_WIDTH = 512
N_BRANCH = 4
A_HALF_WINDOW = 128
B_WIN_ROWS = 8
B_WIN_COLS = 16
C_PATTERNS = ((128, 1), (512, 4), (2048, 16))
GRID_W = 64
ROPE_THETA = 10000.0
NORM_EPS = 1e-6
LOG2E = math.log2(math.e)
SCALE = HEAD_DIM ** -0.5 * LOG2E
NEG_BIG = -1e30

LANE = 128
VMEM_LIMIT = 56 * 1024 * 1024

_SPLITS = (512, 256, 256, 512, 512, 512, 512, 512, 1536, 512, 512, 512, 512, 256, 256, 512, 8192)
_NAMES = ("aq", "ak", "av", "az", "bq", "bk", "bv", "bz", "cq", "ck", "cv", "cz", "dq", "dk", "dv", "dz", "gates")
_OFF = dict(zip(_NAMES, np.concatenate([[0], np.cumsum(_SPLITS)[:-1]]).tolist()))
_WID = dict(zip(_NAMES, _SPLITS))

ROPE_SEG = ("aq", "ak", "cq", "ck")
NORM_SEG = ("dq", "dk")
PLAIN_SEG = ("az", "bz", "cz", "dz", "av", "dv", "bq", "bk", "bv", "cv")


def _seg_offsets(seg):
    offs, o = {}, 0
    for n in seg:
        offs[n] = o
        o += _WID[n]
    return offs, o


ROPE_OFF, ROPE_W = _seg_offsets(ROPE_SEG)
NORM_OFF, NORM_W = _seg_offsets(NORM_SEG)
PLAIN_OFF, PLAIN_W = _seg_offsets(PLAIN_SEG)


def _params(sem):
    return pltpu.CompilerParams(dimension_semantics=sem, vmem_limit_bytes=VMEM_LIMIT)


def _sigmoid(v):
    return 1.0 / (1.0 + jnp.exp(-v))


def _weighted_values(p, v):
    v1 = jnp.concatenate([v, jnp.ones_like(v)], axis=1)
    r = jnp.dot(p.astype(BF16), v1, preferred_element_type=F32)
    return r[:, :HEAD_DIM], r[:, HEAD_DIM:]


PROJ_RB = 256


def _rope_half(xh, cos, sin_signed):
    return xh * cos + pltpu.roll(xh, HEAD_DIM // 2, 1) * sin_signed


def _proj_kernel(*refs, mode, norm_input):
    if norm_input:
        g_in_ref, refs, h_out_ref = refs[0], refs[1:-1], refs[-1]
    h_ref, w_ref = refs[0], refs[1]
    o_ref = refs[-1]
    tm, tn = o_ref.shape
    for r in range(tm // PROJ_RB):
        rows = pl.ds(r * PROJ_RB, PROJ_RB)
        if norm_input:
            x = h_ref[rows, :]
            h = (x * lax.rsqrt(jnp.mean(x * x, axis=-1, keepdims=True) + NORM_EPS) * g_in_ref[...]).astype(BF16)
            h_out_ref[rows, :] = h
        else:
            h = h_ref[rows, :]
        acc = jnp.dot(h, w_ref[...], preferred_element_type=F32)
        if mode == "scale":
            o_ref[rows, :] = (acc * refs[2][...]).astype(o_ref.dtype)
        elif mode == "rope":
            cs_ref, cos_ref, sin_ref = refs[2:5]
            cos, sin = cos_ref[rows, :], sin_ref[rows, :]
            for c in range(tn // HEAD_DIM):
                sl = slice(c * HEAD_DIM, (c + 1) * HEAD_DIM)
                o_ref[rows, sl] = (_rope_half(acc[:, sl], cos, sin) * cs_ref[:, sl]).astype(o_ref.dtype)
        elif mode == "norm_rope":
            cs_ref, g_ref, cos_ref, sin_ref = refs[2:6]
            cos, sin = cos_ref[rows, :], sin_ref[rows, :]
            for c in range(tn // HEAD_DIM):
                sl = slice(c * HEAD_DIM, (c + 1) * HEAD_DIM)
                xh = acc[:, sl]
                y = xh * lax.rsqrt(jnp.mean(xh * xh, axis=-1, keepdims=True) + NORM_EPS) * g_ref[:, sl]
                o_ref[rows, sl] = (_rope_half(y, cos, sin) * cs_ref[:, sl]).astype(o_ref.dtype)
        else:
            raise ValueError(mode)


def _proj(h, w, mode, row_vecs, tables, tm, tn, name, norm_gain=None):
    s, d = h.shape
    n = w.shape[1]
    tm = min(tm, s)
    in_specs = [pl.BlockSpec((tm, d), lambda i, j: (i, 0)), pl.BlockSpec((d, tn), lambda i, j: (0, j))]
    in_specs += [pl.BlockSpec((1, tn), lambda i, j: (0, j)) for _ in row_vecs]
    in_specs += [pl.BlockSpec((tm, HEAD_DIM), lambda i, j: (i, 0)) for _ in tables]
    out_shape = [jax.ShapeDtypeStruct((s, n), BF16)]
    out_specs = [pl.BlockSpec((tm, tn), lambda i, j: (i, j))]
    args = [h, w, *row_vecs, *tables]
    if norm_gain is not None:
        assert n == tn
        in_specs = [pl.BlockSpec((1, d), lambda i, j: (0, 0))] + in_specs
        args = [norm_gain.reshape(1, d)] + args
        out_shape.append(jax.ShapeDtypeStruct((s, d), BF16))
        out_specs.append(pl.BlockSpec((tm, d), lambda i, j: (i, 0)))
    outs = pl.pallas_call(
        functools.partial(_proj_kernel, mode=mode, norm_input=norm_gain is not None),
        out_shape=out_shape,
        grid=(s // tm, n // tn),
        in_specs=in_specs,
        out_specs=out_specs,
        compiler_params=_params(("arbitrary", "arbitrary")),
        name=name,
    )(*args)
    return outs[0] if norm_gain is None else outs


def _gates_kernel(h_ref, w_ref, b_ref, o_ref, wb_ref):
    @pl.when(pl.program_id(1) == 0)
    def _():
        wb_ref[...] = w_ref[...].astype(BF16)

    for r in range(o_ref.shape[0] // PROJ_RB):
        rows = pl.ds(r * PROJ_RB, PROJ_RB)
        acc = jnp.dot(h_ref[rows, :], wb_ref[...], preferred_element_type=F32)
        o_ref[rows, :] = _sigmoid(acc + b_ref[...]).astype(o_ref.dtype)


def _proj_gates(h, w_in_all, layer, bias, tm=2048, tn=1024):
    s, d = h.shape
    tm = min(tm, s)
    n = _WID["gates"]
    first = _OFF["gates"] // tn
    return pl.pallas_call(
        _gates_kernel,
        out_shape=jax.ShapeDtypeStruct((s, n), BF16),
        grid=(n // tn, s // tm),
        in_specs=[pl.BlockSpec((tm, d), lambda j, i: (i, 0)),
                  pl.BlockSpec((None, d, tn), lambda j, i: (layer, 0, first + j)),
                  pl.BlockSpec((1, tn), lambda j, i: (0, j))],
        out_specs=pl.BlockSpec((tm, tn), lambda j, i: (i, j)),
        scratch_shapes=[pltpu.VMEM((d, tn), BF16)],
        compiler_params=_params(("arbitrary", "arbitrary")),
        name="proj_gates",
    )(h, w_in_all, bias)


WIN_BQ = 128


def _window_kernel(sink_ref, q_ref, k_ref, v_ref, o_ref, *, seq_len, nsub):
    kv_head = pl.program_id(0)
    step = pl.program_id(1)
    bq, hw, groups = WIN_BQ, A_HALF_WINDOW, 2
    span = bq + 2 * hw
    rows = groups * bq
    row = lax.broadcasted_iota(jnp.int32, (rows, span), 0)
    col_minus_row = lax.broadcasted_iota(jnp.int32, (rows, span), 1) - jnp.where(row >= bq, row - bq, row)
    r1 = lax.broadcasted_iota(jnp.int32, (rows, 1), 0)
    sink = jnp.where(r1 >= bq, sink_ref[kv_head * groups + 1], sink_ref[kv_head * groups]) * LOG2E

    def body(b, carry):
        q0 = (step * nsub + b) * bq
        start = pl.multiple_of(jnp.clip(q0 - hw, 0, seq_len - span), bq)
        k = k_ref[pl.ds(start, span), :]
        v = v_ref[pl.ds(start, span), :]
        qoff = pl.multiple_of(b * bq, bq)
        qb = q_ref[pl.ds(qoff, bq), :]
        q2 = jnp.concatenate([qb[:, :HEAD_DIM], qb[:, HEAD_DIM:]], axis=0)
        s = lax.dot_general(q2, k, (((1,), (1,)), ((), ())), preferred_element_type=F32)
        s = jnp.where(jnp.abs(col_minus_row + (start - q0)) <= hw, s, NEG_BIG)
        m = jnp.maximum(jnp.max(s, axis=1, keepdims=True), sink)
        p = jnp.exp2(s - m)
        pv, psum = _weighted_values(p, v)
        o = pv / (psum + jnp.exp2(sink - m))
        o_ref[pl.ds(qoff, bq), :HEAD_DIM] = o[:bq].astype(o_ref.dtype)
        o_ref[pl.ds(qoff, bq), HEAD_DIM:] = o[bq:].astype(o_ref.dtype)
        return carry

    lax.fori_loop(0, nsub, body, 0, unroll=True)


def _window_attention(r_arr, p_arr, sink, *, seq_len, max_step=1024):
    nsub = min(max_step, seq_len) // WIN_BQ
    tq = nsub * WIN_BQ
    qc, kc, vc = ROPE_OFF["aq"] // (2 * HEAD_DIM), ROPE_OFF["ak"] // HEAD_DIM, PLAIN_OFF["av"] // HEAD_DIM
    return pl.pallas_call(
        functools.partial(_window_kernel, seq_len=seq_len, nsub=nsub),
        out_shape=jax.ShapeDtypeStruct((seq_len, BRANCH_WIDTH), BF16),
        grid=(2, seq_len // tq),
        in_specs=[pl.BlockSpec(memory_space=pltpu.SMEM),
                  pl.BlockSpec((tq, 2 * HEAD_DIM), lambda c, i: (i, qc + c)),
                  pl.BlockSpec((seq_len, HEAD_DIM), lambda c, i: (0, kc + c)),
                  pl.BlockSpec((seq_len, HEAD_DIM), lambda c, i: (0, vc + c))],
        out_specs=pl.BlockSpec((tq, 2 * HEAD_DIM), lambda c, i: (i, c)),
        compiler_params=_params(("arbitrary", "arbitrary")),
        name="window_attention",
    )(sink, r_arr, r_arr, p_arr)


DIL_BQ = 128
DIL_HALF = 64
DIL_HALO = DIL_HALF * max(d for _, d in C_PATTERNS)


def _dilated_kernel(q0_ref, q1_ref, q2_ref, kp_ref, kc_ref, kn_ref, vp_ref, vc_ref, vn_ref, o_ref,
                    qf_ref, kw_ref, vw_ref, og_ref, lg_ref, *, seq_len, tq):
    t0 = pl.program_id(1) * tq
    halo, bq, half = DIL_HALO, DIL_BQ, DIL_HALF
    span = bq + 2 * half
    for w_ref, prev, cur, nxt in ((kw_ref, kp_ref, kc_ref, kn_ref), (vw_ref, vp_ref, vc_ref, vn_ref)):
        w_ref[:halo] = prev[...].astype(F32)
        w_ref[halo:halo + tq] = cur[...].astype(F32)
        w_ref[halo + tq:] = nxt[...].astype(F32)
    for g, q_ref in enumerate((q0_ref, q1_ref, q2_ref)):
        qf_ref[g] = q_ref[...].astype(F32)

    ii = lax.broadcasted_iota(jnp.int32, (bq, span), 0)
    jj = lax.broadcasted_iota(jnp.int32, (bq, span), 1)
    in_band = jnp.abs(jj - half - ii) <= half

    for g, (_, dil) in enumerate(C_PATTERNS):
        nb = tq // (bq * dil)

        def body(it, carry, g=g, dil=dil, nb=nb):
            q_row = it // nb + dil * bq * (it % nb)
            w_row = halo + q_row - half * dil
            q = qf_ref[g, pl.ds(q_row, bq, stride=dil), :].astype(BF16)
            k = kw_ref[pl.ds(w_row, span, stride=dil), :].astype(BF16)
            v = vw_ref[pl.ds(w_row, span, stride=dil), :].astype(BF16)
            s = lax.dot_general(q, k, (((1,), (1,)), ((), ())), preferred_element_type=F32)
            ktok = t0 + q_row - half * dil + dil * jj
            s = jnp.where(in_band & (ktok >= 0) & (ktok < seq_len), s, NEG_BIG)
            m = jnp.max(s, axis=1, keepdims=True)
            p = jnp.exp2(s - m)
            pv, psum = _weighted_values(p, v)
            og_ref[g, pl.ds(q_row, bq, stride=dil), :] = pv / psum
            lg_ref[g, pl.ds(q_row, bq, stride=dil), :] = m + jnp.log2(psum)
            return carry

        lax.fori_loop(0, tq // bq, body, 0, unroll=True)

    l0, l1, l2 = lg_ref[0], lg_ref[1], lg_ref[2]
    mx = jnp.maximum(jnp.maximum(l0, l1), l2)
    e0, e1, e2 = jnp.exp2(l0 - mx), jnp.exp2(l1 - mx), jnp.exp2(l2 - mx)
    o_ref[...] = ((e0 * og_ref[0] + e1 * og_ref[1] + e2 * og_ref[2]) / (e0 + e1 + e2)).astype(o_ref.dtype)


def _dilated_attention(r_arr, p_arr, *, seq_len, tq=2048):
    tq = min(tq, seq_len)
    halo = DIL_HALO
    ratio, n_halo = tq // halo, seq_len // halo
    qc, kc, vc = ROPE_OFF["cq"] // HEAD_DIM, ROPE_OFF["ck"] // HEAD_DIM, PLAIN_OFF["cv"] // HEAD_DIM
    n_heads = BRANCH_WIDTH // HEAD_DIM

    def window_specs(col):
        return [pl.BlockSpec((halo, HEAD_DIM), lambda h, i: (jnp.maximum(i * ratio - 1, 0), col + h)),
                pl.BlockSpec((tq, HEAD_DIM), lambda h, i: (i, col + h)),
                pl.BlockSpec((halo, HEAD_DIM), lambda h, i: (jnp.minimum((i + 1) * ratio, n_halo - 1), col + h))]

    q_specs = [pl.BlockSpec((tq, HEAD_DIM), lambda h, i, g=g: (i, qc + n_heads * g + h)) for g in range(3)]
    return pl.pallas_call(
        functools.partial(_dilated_kernel, seq_len=seq_len, tq=tq),
        out_shape=jax.ShapeDtypeStruct((seq_len, BRANCH_WIDTH), BF16),
        grid=(n_heads, seq_len // tq),
        in_specs=q_specs + window_specs(kc) + window_specs(vc),
        out_specs=pl.BlockSpec((tq, HEAD_DIM), lambda h, i: (i, h)),
        scratch_shapes=[pltpu.VMEM((3, tq, HEAD_DIM), F32), pltpu.VMEM((tq + 2 * halo, HEAD_DIM), F32),
                        pltpu.VMEM((tq + 2 * halo, HEAD_DIM), F32), pltpu.VMEM((3, tq, HEAD_DIM), F32),
                        pltpu.VMEM((3, tq, HEAD_DIM), F32)],
        compiler_params=_params(("arbitrary", "arbitrary")),
        name="dilated_attention",
    )(r_arr, r_arr, r_arr, r_arr, r_arr, r_arr, p_arr, p_arr, p_arr)


NBR_ROWS = 4
NBR_UNION = 12
NBR_SUB = 16


def _nbr_bias_kernel(rpb_ref, o_ref, t_ref):
    h = pl.program_id(0)
    n_dr, n_dc = 2 * B_WIN_ROWS - 1, 2 * B_WIN_COLS - 1
    qc = lax.broadcasted_iota(jnp.int32, (GRID_W, GRID_W), 0)
    kc = lax.broadcasted_iota(jnp.int32, (GRID_W, GRID_W), 1)
    col_start = jnp.clip(qc - B_WIN_COLS // 2, 0, GRID_W - B_WIN_COLS)
    col_ok = (kc >= col_start) & (kc < col_start + B_WIN_COLS)
    dc = kc - qc + (B_WIN_COLS - 1)
    for dr in range(n_dr):
        t = jnp.full((GRID_W, GRID_W), NEG_BIG, F32)
        for d in range(n_dc):
            t = jnp.where(dc == d, rpb_ref[(h * n_dr + dr) * n_dc + d] * LOG2E, t)
        t_ref[dr] = jnp.where(col_ok, t, NEG_BIG)
    o_ref[...] = jnp.full(o_ref.shape, NEG_BIG, F32)
    half = B_WIN_ROWS // 2
    for case in range(3):
        for a in range(NBR_ROWS):
            win, off = ((0, a), (a, half), (NBR_UNION - B_WIN_ROWS, half + a))[case]
            for j in range(B_WIN_ROWS):
                o_ref[case, a * GRID_W:(a + 1) * GRID_W, (win + j) * GRID_W:(win + j + 1) * GRID_W] = (
                    t_ref[j - off + B_WIN_ROWS - 1])


def _nbr_bias(rpb):
    n_heads = rpb.shape[0]
    shape = (3, NBR_ROWS * GRID_W, NBR_UNION * GRID_W)
    return pl.pallas_call(
        _nbr_bias_kernel,
        out_shape=jax.ShapeDtypeStruct((n_heads,) + shape, F32),
        grid=(n_heads,),
        in_specs=[pl.BlockSpec(memory_space=pltpu.SMEM)],
        out_specs=pl.BlockSpec((None,) + shape, lambda h: (h, 0, 0, 0)),
        scratch_shapes=[pltpu.VMEM((2 * B_WIN_ROWS - 1, GRID_W, GRID_W), F32)],
        compiler_params=_params(("arbitrary",)),
        name="nbr_bias",
    )(rpb.reshape(-1))


def _nbr_kernel(q_ref, k_ref, v_ref, b_first_ref, b_mid_ref, b_last_ref, o_ref, *, n_rows):
    step = pl.program_id(1)
    nq, nk = NBR_ROWS * GRID_W, NBR_UNION * GRID_W
    for sb in range(NBR_SUB):
        bias_ref = b_first_ref if sb == 0 else (b_last_ref if sb == NBR_SUB - 1 else b_mid_ref)
        r0 = (step * NBR_SUB + sb) * NBR_ROWS
        u0 = jnp.clip(r0 - B_WIN_ROWS // 2, 0, n_rows - NBR_UNION)
        kstart = pl.multiple_of(u0 * GRID_W, GRID_W)
        k = k_ref[pl.ds(kstart, nk), :]
        v = v_ref[pl.ds(kstart, nk), :]
        q = q_ref[sb * nq:(sb + 1) * nq, :]
        s = lax.dot_general(q, k, (((1,), (1,)), ((), ())), preferred_element_type=F32) + bias_ref[...]
        m = jnp.max(s, axis=1, keepdims=True)
        p = jnp.exp2(s - m)
        pv, psum = _weighted_values(p, v)
        o_ref[sb * nq:(sb + 1) * nq, :] = (pv / psum).astype(o_ref.dtype)


def _nbr_attention(p_arr, bias, *, seq_len):
    n_rows = seq_len // GRID_W
    tq = NBR_SUB * NBR_ROWS * GRID_W
    n_steps = seq_len // tq
    qc, kc, vc = (PLAIN_OFF[n] // HEAD_DIM for n in ("bq", "bk", "bv"))
    tile = (None, None, NBR_ROWS * GRID_W, NBR_UNION * GRID_W)
    return pl.pallas_call(
        functools.partial(_nbr_kernel, n_rows=n_rows),
        out_shape=jax.ShapeDtypeStruct((seq_len, BRANCH_WIDTH), BF16),
        grid=(BRANCH_WIDTH // HEAD_DIM, n_steps),
        in_specs=[pl.BlockSpec((tq, HEAD_DIM), lambda h, i: (i, qc + h)),
                  pl.BlockSpec((seq_len, HEAD_DIM), lambda h, i: (0, kc + h)),
                  pl.BlockSpec((seq_len, HEAD_DIM), lambda h, i: (0, vc + h)),
                  pl.BlockSpec(tile, lambda h, i: (h, jnp.where(i == 0, 0, 1), 0, 0)),
                  pl.BlockSpec(tile, lambda h, i: (h, 1, 0, 0)),
                  pl.BlockSpec(tile, lambda h, i: (h, jnp.where(i == n_steps - 1, 2, 1), 0, 0))],
        out_specs=pl.BlockSpec((tq, HEAD_DIM), lambda h, i: (i, h)),
        compiler_params=_params(("arbitrary", "arbitrary")),
        name="nbr_attention",
    )(p_arr, p_arr, p_arr, bias, bias, bias)


DENSE_RB = 256


def _dense_kernel(q_ref, k_ref, v_ref, o_ref, q2_ref, v2_ref, m_ref, acc_ref, s0_ref, *, seq_len, tq, tk):
    @pl.when(pl.program_id(1) == 0)
    def _():
        v2_ref[:, :HEAD_DIM] = v_ref[...]
        v2_ref[:, HEAD_DIM:] = jnp.ones((seq_len, HEAD_DIM), BF16)

    q2_ref[:tq] = q_ref[:, :HEAD_DIM]
    q2_ref[tq:] = q_ref[:, HEAD_DIM:]
    m_ref[...] = jnp.full(m_ref.shape, NEG_BIG, F32)
    acc_ref[...] = jnp.zeros(acc_ref.shape, F32)
    n_chunks = seq_len // tk

    def logits(r, k):
        return lax.dot_general(q2_ref[pl.ds(r * DENSE_RB, DENSE_RB), :], k, (((1,), (1,)), ((), ())),
                               preferred_element_type=F32)

    s0_ref[...] = logits(0, k_ref[pl.ds(0, tk), :])

    def body(j, carry):
        koff = pl.multiple_of(j * tk, tk)
        k = k_ref[pl.ds(koff, tk), :]
        v2 = v2_ref[pl.ds(koff, tk), :]
        for r in range((2 * tq) // DENSE_RB):
            rows = pl.ds(r * DENSE_RB, DENSE_RB)
            s = s0_ref[...] if r == 0 else logits(r, k)
            m_prev = m_ref[rows, :]
            m_new = jnp.maximum(m_prev, jnp.max(s, axis=1, keepdims=True))
            alpha = jnp.exp2(m_prev - m_new)
            p = jnp.exp2(s - m_new).astype(BF16)
            acc_ref[rows, :] = acc_ref[rows, :] * alpha + jnp.dot(p, v2, preferred_element_type=F32)
            m_ref[rows, :] = m_new
        next_off = pl.multiple_of(jnp.minimum(j + 1, n_chunks - 1) * tk, tk)
        s0_ref[...] = logits(0, k_ref[pl.ds(next_off, tk), :])
        return carry

    lax.fori_loop(0, n_chunks, body, 0)
    acc = acc_ref[...]
    o = acc[:, :HEAD_DIM] / acc[:, HEAD_DIM:]
    o_ref[:, :HEAD_DIM] = o[:tq].astype(o_ref.dtype)
    o_ref[:, HEAD_DIM:] = o[tq:].astype(o_ref.dtype)


def _dense_attention(n_arr, p_arr, *, seq_len, tq=2048, tk=2048):
    tq, tk = min(tq, seq_len), min(tk, seq_len)
    kc = NORM_OFF["dk"] // HEAD_DIM
    vc = PLAIN_OFF["dv"] // HEAD_DIM
    return pl.pallas_call(
        functools.partial(_dense_kernel, seq_len=seq_len, tq=tq, tk=tk),
        out_shape=jax.ShapeDtypeStruct((seq_len, BRANCH_WIDTH), BF16),
        grid=(2, seq_len // tq),
        in_specs=[pl.BlockSpec((tq, 2 * HEAD_DIM), lambda g, i: (i, g)),
                  pl.BlockSpec((seq_len, HEAD_DIM), lambda g, i: (0, kc + g), pipeline_mode=pl.Buffered(1)),
                  pl.BlockSpec((seq_len, HEAD_DIM), lambda g, i: (0, vc + g), pipeline_mode=pl.Buffered(1))],
        out_specs=pl.BlockSpec((tq, 2 * HEAD_DIM), lambda g, i: (i, g)),
        scratch_shapes=[pltpu.VMEM((2 * tq, HEAD_DIM), BF16), pltpu.VMEM((seq_len, 2 * HEAD_DIM), BF16),
                        pltpu.VMEM((2 * tq, 1), F32), pltpu.VMEM((2 * tq, 2 * HEAD_DIM), F32),
                        pltpu.VMEM((DENSE_RB, tk), F32)],
        compiler_params=_params(("arbitrary", "arbitrary")),
        name="dense_attention",
    )(n_arr, n_arr, p_arr)


def _merge_kernel(ya_ref, yb_ref, yc_ref, yd_ref, z_ref, gate_ref, x_ref, wb_ref, wo_ref, g_ref, *out_refs, final):
    merged = None
    for n, y_ref in enumerate((ya_ref, yb_ref, yc_ref, yd_ref)):
        z = z_ref[:, n * BRANCH_WIDTH:(n + 1) * BRANCH_WIDTH].astype(F32)
        t = (y_ref[...].astype(F32) * (z * _sigmoid(z))).astype(BF16)
        br = jnp.dot(t, wb_ref[n], preferred_element_type=F32)
        term = gate_ref[:, n * D_MODEL:(n + 1) * D_MODEL].astype(F32) * br
        merged = term if merged is None else merged + term
    x_new = x_ref[...] + jnp.dot(merged.astype(BF16), wo_ref[...], preferred_element_type=F32)
    normed = x_new * lax.rsqrt(jnp.mean(x_new * x_new, axis=-1, keepdims=True) + NORM_EPS) * g_ref[...]
    if final:
        out_refs[0][...] = normed
    else:
        out_refs[0][...] = x_new
        out_refs[1][...] = normed.astype(BF16)


def _merge(ys, p_arr, gates, x, wb, wo, g_next, *, final, tm=256):
    s = x.shape[0]
    row = lambda w: pl.BlockSpec((tm, w), lambda i: (i, 0))
    once = dict(pipeline_mode=pl.Buffered(1))
    in_specs = ([row(BRANCH_WIDTH)] * N_BRANCH
                + [row(N_BRANCH * BRANCH_WIDTH), row(N_BRANCH * D_MODEL), row(D_MODEL),
                   pl.BlockSpec((N_BRANCH, BRANCH_WIDTH, D_MODEL), lambda i: (0, 0, 0), **once),
                   pl.BlockSpec((D_MODEL, D_MODEL), lambda i: (0, 0), **once),
                   pl.BlockSpec((1, D_MODEL), lambda i: (0, 0))])
    if final:
        out_shape = [jax.ShapeDtypeStruct((s, D_MODEL), F32)]
    else:
        out_shape = [jax.ShapeDtypeStruct((s, D_MODEL), F32), jax.ShapeDtypeStruct((s, D_MODEL), BF16)]
    return pl.pallas_call(
        functools.partial(_merge_kernel, final=final),
        out_shape=out_shape,
        grid=(s // tm,),
        in_specs=in_specs,
        out_specs=[row(D_MODEL)] * len(out_shape),
        compiler_params=_params(("arbitrary",)),
        name="merge",
    )(*ys, p_arr, gates, x, wb, wo, g_next.reshape(1, D_MODEL))


@functools.lru_cache(maxsize=None)
def _rope_tables(seq_len):
    t = np.arange(seq_len)

    def angles(pos, dim):
        inv_freq = np.float32(ROPE_THETA) ** (-np.arange(dim // 2, dtype=np.float32) / np.float32(dim // 2))
        return pos.astype(np.float32)[:, None] * inv_freq.astype(np.float32)[None, :]

    a1 = angles(t, HEAD_DIM)
    cos1 = np.concatenate([np.cos(a1), np.cos(a1)], axis=1)
    sin1 = np.concatenate([-np.sin(a1), np.sin(a1)], axis=1)
    ar = angles(t // GRID_W, HEAD_DIM // 2)
    ac = angles(t % GRID_W, HEAD_DIM // 2)
    cos2 = np.concatenate([np.cos(ar), np.cos(ac), np.cos(ar), np.cos(ac)], axis=1)
    sin2 = np.concatenate([-np.sin(ar), -np.sin(ac), np.sin(ar), np.sin(ac)], axis=1)
    assert all(a.dtype == np.float32 for a in (cos1, sin1, cos2, sin2))
    return (cos1, sin1), (cos2, sin2)


def _axial_reorder(a):
    lead = a.shape[:-1]
    return a.reshape(*lead, -1, 2, 2, HEAD_DIM // 4).swapaxes(-3, -2).reshape(*lead, -1)


def _gather_cols(w, seg):
    cols = np.concatenate([np.arange(_OFF[n], _OFF[n] + _WID[n]) for n in seg])
    runs = np.split(cols, np.nonzero(np.diff(cols) != 1)[0] + 1)
    return jnp.concatenate([w[:, r[0]:r[-1] + 1] for r in runs], axis=1).astype(BF16)


def _col_scale(seg, scaled):
    return jnp.asarray(np.concatenate(
        [np.full((_WID[n],), scaled.get(n, 1.0), np.float32) for n in seg]).reshape(1, -1))


def _layer(x, h, w_in_all, layer, norm_gain, gate_b, a_sink, b_rpb, d_q_norm, d_k_norm, w_branch, w_out, g_next,
           tables, final):
    s = x.shape[0]
    (cos1, sin1), (cos2, sin2) = tables
    tm = 1024
    w_in = w_in_all[layer, :, :_OFF["gates"]]

    gains = _axial_reorder(jnp.concatenate([jnp.tile(d_q_norm, _WID["dq"] // HEAD_DIM),
                                            jnp.tile(d_k_norm, _WID["dk"] // HEAD_DIM)]))
    n_arr = _proj(x if h is None else h, _axial_reorder(_gather_cols(w_in, NORM_SEG)), "norm_rope",
                  [_col_scale(NORM_SEG, {"dq": SCALE}), gains.reshape(1, -1)],
                  [cos2, sin2], tm=tm, tn=NORM_W, name="proj_norm_rope", norm_gain=norm_gain if h is None else None)
    if h is None:
        n_arr, h = n_arr
    r_arr = _proj(h, _gather_cols(w_in, ROPE_SEG), "rope", [_col_scale(ROPE_SEG, {"aq": SCALE, "cq": SCALE})],
                  [cos1, sin1], tm=tm, tn=ROPE_W // 2, name="proj_rope")
    p_arr = _proj(h, _gather_cols(w_in, PLAIN_SEG), "scale", [_col_scale(PLAIN_SEG, {"bq": SCALE})], [],
                  tm=tm, tn=PLAIN_W // 3, name="proj_plain")
    gates = _proj_gates(h, w_in_all, layer, gate_b.reshape(1, -1))

    ya = _window_attention(r_arr, p_arr, a_sink, seq_len=s)
    yb = _nbr_attention(p_arr, _nbr_bias(b_rpb), seq_len=s)
    yc = _dilated_attention(r_arr, p_arr, seq_len=s)
    yd = _dense_attention(n_arr, p_arr, seq_len=s)
    return _merge((ya, yb, yc, yd), p_arr, gates, x, w_branch.astype(BF16), w_out.astype(BF16), g_next, final=final)


def kernel(x, norm_g, w_in, gate_b, a_sink, b_rpb, d_q_norm, d_k_norm, w_branch, w_out, final_norm_g):
    b, s, d = x.shape
    depth = w_in.shape[0]
    tables = tuple(tuple(jnp.asarray(a) for a in grp) for grp in _rope_tables(s))
    outs = []
    for bi in range(b):
        xb = x.reshape(s, d) if b == 1 else x[bi]
        h = None
        for layer in range(depth):
            final = layer == depth - 1
            g_next = final_norm_g if final else norm_g[layer + 1]
            res = _layer(xb, h, w_in, layer, norm_g[layer], gate_b[layer], a_sink[layer], b_rpb[layer], d_q_norm[layer],
                         d_k_norm[layer], w_branch[layer], w_out[layer], g_next, tables, final)
            if final:
                xb = res[0]
            else:
                xb, h = res
        outs.append(xb)
    return outs[0].reshape(1, s, d) if b == 1 else jnp.stack(outs, axis=0)
```

```python
import functools
import math

import numpy as np
import jax
import jax.numpy as jnp
from jax import lax
from jax.experimental import pallas as pl
from jax.experimental.pallas import tpu as pltpu

F32 = jnp.float32
BF16 = jnp.bfloat16

D_MODEL = 2048
HEAD_DIM = 128
BRANCH_WIDTH = 512
N_BRANCH = 4
A_HALF_WINDOW = 128
B_WIN_ROWS = 8
B_WIN_COLS = 16
C_PATTERNS = ((128, 1), (512, 4), (2048, 16))
GRID_W = 64
ROPE_THETA = 10000.0
NORM_EPS = 1e-6
LOG2E = math.log2(math.e)
SCALE = HEAD_DIM ** -0.5 * LOG2E
NEG_BIG = -1e30

VMEM_LIMIT = 56 * 1024 * 1024

_SPLITS = (512, 256, 256, 512, 512, 512, 512, 512, 1536, 512, 512, 512, 512, 256, 256, 512, 8192)
_NAMES = ("aq", "ak", "av", "az", "bq", "bk", "bv", "bz", "cq", "ck", "cv", "cz", "dq", "dk", "dv", "dz", "gates")
_OFF = dict(zip(_NAMES, np.concatenate([[0], np.cumsum(_SPLITS)[:-1]]).tolist()))
_WID = dict(zip(_NAMES, _SPLITS))

ROPE_SEG = ("aq", "ak", "cq", "ck")
NORM_SEG = ("dq", "dk")
PLAIN_SEG = ("az", "bz", "cz", "dz", "av", "dv", "bq", "bk", "bv", "cv")


def _seg_offsets(seg):
    offs, o = {}, 0
    for n in seg:
        offs[n] = o
        o += _WID[n]
    return offs, o


ROPE_OFF, ROPE_W = _seg_offsets(ROPE_SEG)
NORM_OFF, NORM_W = _seg_offsets(NORM_SEG)
PLAIN_OFF, PLAIN_W = _seg_offsets(PLAIN_SEG)


def _params(sem):
    return pltpu.CompilerParams(dimension_semantics=sem, vmem_limit_bytes=VMEM_LIMIT)


def _sigmoid(v):
    return 1.0 / (1.0 + jnp.exp(-v))


def _weighted_values(p, v):
    v1 = jnp.concatenate([v, jnp.ones_like(v)], axis=1)
    r = jnp.dot(p.astype(BF16), v1, preferred_element_type=F32)
    return r[:, :HEAD_DIM], r[:, HEAD_DIM:]


PROJ_RB = 256


def _rope_half(xh, cos, sin_signed):
    return xh * cos + pltpu.roll(xh, HEAD_DIM // 2, 1) * sin_signed


def _proj_kernel(*refs, mode, norm_input):
    if norm_input:
        g_in_ref, refs, h_out_ref = refs[0], refs[1:-1], refs[-1]
    h_ref, w_ref = refs[0], refs[1]
    o_ref = refs[-1]
    tm, tn = o_ref.shape
    for r in range(tm // PROJ_RB):
        rows = pl.ds(r * PROJ_RB, PROJ_RB)
        if norm_input:
            x = h_ref[rows, :]
            h = (x * lax.rsqrt(jnp.mean(x * x, axis=-1, keepdims=True) + NORM_EPS) * g_in_ref[...]).astype(BF16)
            h_out_ref[rows, :] = h
        else:
            h = h_ref[rows, :]
        acc = jnp.dot(h, w_ref[...], preferred_element_type=F32)
        if mode == "scale":
            o_ref[rows, :] = (acc * refs[2][...]).astype(o_ref.dtype)
        elif mode == "rope":
            cs_ref, cos_ref, sin_ref = refs[2:5]
            cos, sin = cos_ref[rows, :], sin_ref[rows, :]
            for c in range(tn // HEAD_DIM):
                sl = slice(c * HEAD_DIM, (c + 1) * HEAD_DIM)
                o_ref[rows, sl] = (_rope_half(acc[:, sl], cos, sin) * cs_ref[:, sl]).astype(o_ref.dtype)
        elif mode == "norm_rope":
            cs_ref, g_ref, cos_ref, sin_ref = refs[2:6]
            cos, sin = cos_ref[rows, :], sin_ref[rows, :]
            for c in range(tn // HEAD_DIM):
                sl = slice(c * HEAD_DIM, (c + 1) * HEAD_DIM)
                xh = acc[:, sl]
                y = xh * lax.rsqrt(jnp.mean(xh * xh, axis=-1, keepdims=True) + NORM_EPS) * g_ref[:, sl]
                o_ref[rows, sl] = (_rope_half(y, cos, sin) * cs_ref[:, sl]).astype(o_ref.dtype)
        else:
            raise ValueError(mode)


def _proj(h, w, mode, row_vecs, tables, tm, tn, name, norm_gain=None):
    s, d = h.shape
    n = w.shape[1]
    tm = min(tm, s)
    in_specs = [pl.BlockSpec((tm, d), lambda i, j: (i, 0)), pl.BlockSpec((d, tn), lambda i, j: (0, j))]
    in_specs += [pl.BlockSpec((1, tn), lambda i, j: (0, j)) for _ in row_vecs]
    in_specs += [pl.BlockSpec((tm, HEAD_DIM), lambda i, j: (i, 0)) for _ in tables]
    out_shape = [jax.ShapeDtypeStruct((s, n), BF16)]
    out_specs = [pl.BlockSpec((tm, tn), lambda i, j: (i, j))]
    args = [h, w, *row_vecs, *tables]
    if norm_gain is not None:
        assert n == tn
        in_specs = [pl.BlockSpec((1, d), lambda i, j: (0, 0))] + in_specs
        args = [norm_gain.reshape(1, d)] + args
        out_shape.append(jax.ShapeDtypeStruct((s, d), BF16))
        out_specs.append(pl.BlockSpec((tm, d), lambda i, j: (i, 0)))
    outs = pl.pallas_call(
        functools.partial(_proj_kernel, mode=mode, norm_input=norm_gain is not None),
        out_shape=out_shape,
        grid=(s // tm, n // tn),
        in_specs=in_specs,
        out_specs=out_specs,
        compiler_params=_params(("arbitrary", "arbitrary")),
        name=name,
    )(*args)
    return outs[0] if norm_gain is None else outs


def _gates_kernel(h_ref, w_ref, b_ref, o_ref, wb_ref):
    @pl.when(pl.program_id(1) == 0)
    def _():
        wb_ref[...] = w_ref[...].astype(BF16)

    for r in range(o_ref.shape[0] // PROJ_RB):
        rows = pl.ds(r * PROJ_RB, PROJ_RB)
        acc = jnp.dot(h_ref[rows, :], wb_ref[...], preferred_element_type=F32)
        o_ref[rows, :] = _sigmoid(acc + b_ref[...]).astype(o_ref.dtype)


def _proj_gates(h, w_in_all, layer, bias, tm=2048, tn=1024):
    s, d = h.shape
    tm = min(tm, s)
    n = _WID["gates"]
    first = _OFF["gates"] // tn
    return pl.pallas_call(
        _gates_kernel,
        out_shape=jax.ShapeDtypeStruct((s, n), BF16),
        grid=(n // tn, s // tm),
        in_specs=[pl.BlockSpec((tm, d), lambda j, i: (i, 0)),
                  pl.BlockSpec((None, d, tn), lambda j, i: (layer, 0, first + j)),
                  pl.BlockSpec((1, tn), lambda j, i: (0, j))],
        out_specs=pl.BlockSpec((tm, tn), lambda j, i: (i, j)),
        scratch_shapes=[pltpu.VMEM((d, tn), BF16)],
        compiler_params=_params(("arbitrary", "arbitrary")),
        name="proj_gates",
    )(h, w_in_all, bias)


WIN_BQ = 128


def _window_kernel(sink_ref, q_ref, k_ref, v_ref, o_ref, *, seq_len, nsub):
    kv_head = pl.program_id(0)
    step = pl.program_id(1)
    bq, hw, groups = WIN_BQ, A_HALF_WINDOW, 2
    span = bq + 2 * hw
    rows = groups * bq
    row = lax.broadcasted_iota(jnp.int32, (rows, span), 0)
    col_minus_row = lax.broadcasted_iota(jnp.int32, (rows, span), 1) - jnp.where(row >= bq, row - bq, row)
    r1 = lax.broadcasted_iota(jnp.int32, (rows, 1), 0)
    sink = jnp.where(r1 >= bq, sink_ref[kv_head * groups + 1], sink_ref[kv_head * groups]) * LOG2E

    def body(b, carry):
        q0 = (step * nsub + b) * bq
        start = pl.multiple_of(jnp.clip(q0 - hw, 0, seq_len - span), bq)
        k = k_ref[pl.ds(start, span), :]
        v = v_ref[pl.ds(start, span), :]
        qoff = pl.multiple_of(b * bq, bq)
        qb = q_ref[pl.ds(qoff, bq), :]
        q2 = jnp.concatenate([qb[:, :HEAD_DIM], qb[:, HEAD_DIM:]], axis=0)
        s = lax.dot_general(q2, k, (((1,), (1,)), ((), ())), preferred_element_type=F32)
        s = jnp.where(jnp.abs(col_minus_row + (start - q0)) <= hw, s, NEG_BIG)
        m = jnp.maximum(jnp.max(s, axis=1, keepdims=True), sink)
        p = jnp.exp2(s - m)
        pv, psum = _weighted_values(p, v)
        o = pv / (psum + jnp.exp2(sink - m))
        o_ref[pl.ds(qoff, bq), :HEAD_DIM] = o[:bq].astype(o_ref.dtype)
        o_ref[pl.ds(qoff, bq), HEAD_DIM:] = o[bq:].astype(o_ref.dtype)
        return carry

    lax.fori_loop(0, nsub, body, 0, unroll=True)


def _window_attention(r_arr, p_arr, sink, *, seq_len, max_step=1024):
    nsub = min(max_step, seq_len) // WIN_BQ
    tq = nsub * WIN_BQ
    qc, kc, vc = ROPE_OFF["aq"] // (2 * HEAD_DIM), ROPE_OFF["ak"] // HEAD_DIM, PLAIN_OFF["av"] // HEAD_DIM
    return pl.pallas_call(
        functools.partial(_window_kernel, seq_len=seq_len, nsub=nsub),
        out_shape=jax.ShapeDtypeStruct((seq_len, BRANCH_WIDTH), BF16),
        grid=(2, seq_len // tq),
        in_specs=[pl.BlockSpec(memory_space=pltpu.SMEM),
                  pl.BlockSpec((tq, 2 * HEAD_DIM), lambda c, i: (i, qc + c)),
                  pl.BlockSpec((seq_len, HEAD_DIM), lambda c, i: (0, kc + c)),
                  pl.BlockSpec((seq_len, HEAD_DIM), lambda c, i: (0, vc + c))],
        out_specs=pl.BlockSpec((tq, 2 * HEAD_DIM), lambda c, i: (i, c)),
        compiler_params=_params(("arbitrary", "arbitrary")),
        name="window_attention",
    )(sink, r_arr, r_arr, p_arr)


DIL_BQ = 128
DIL_HALF = 64
DIL_HALO = DIL_HALF * max(d for _, d in C_PATTERNS)


DIL_BLK = 128


def _residue_major(dil):
    per = DIL_BLK // dil
    p = np.zeros((DIL_BLK, DIL_BLK), np.float32)
    for r in range(dil):
        for i in range(per):
            p[r * per + i, i * dil + r] = 1.0
    return p


def _to_residue_major(src_ref, dst_ref, perm, dil, first_blk, n_blk):
    per = DIL_BLK // dil
    for b in range(first_blk, first_blk + n_blk, 2):
        pair = jnp.concatenate([src_ref[b * DIL_BLK:(b + 1) * DIL_BLK, :],
                                src_ref[(b + 1) * DIL_BLK:(b + 2) * DIL_BLK, :]], axis=1)
        res = jnp.dot(perm, pair, preferred_element_type=F32)
        dst_ref[b * dil:(b + 1) * dil] = res[:, :HEAD_DIM].reshape(dil, per, HEAD_DIM)
        dst_ref[(b + 1) * dil:(b + 2) * dil] = res[:, HEAD_DIM:].reshape(dil, per, HEAD_DIM)


def _dilated_kernel(q0_ref, q1_ref, q2_ref, kp_ref, kc_ref, kn_ref, vp_ref, vc_ref, vn_ref, perm1_ref, perm2_ref,
                    o_ref, kw_ref, vw_ref, qd1_ref, kd1_ref, vd1_ref, qd2_ref, kd2_ref, vd2_ref, og_ref, lg_ref,
                    *, seq_len, tq):
    t0 = pl.program_id(1) * tq
    halo, bq, half = DIL_HALO, DIL_BQ, DIL_HALF
    span = bq + 2 * half
    for w_ref, prev, cur, nxt in ((kw_ref, kp_ref, kc_ref, kn_ref), (vw_ref, vp_ref, vc_ref, vn_ref)):
        w_ref[:halo] = prev[...]
        w_ref[halo:halo + tq] = cur[...]
        w_ref[halo + tq:] = nxt[...]
    residue_major = {}
    for (_, dil), q_ref, perm_ref, qd_ref, kd_ref, vd_ref in (
            (C_PATTERNS[1], q1_ref, perm1_ref, qd1_ref, kd1_ref, vd1_ref),
            (C_PATTERNS[2], q2_ref, perm2_ref, qd2_ref, kd2_ref, vd2_ref)):
        perm = perm_ref[...]
        first_blk = (halo - half * dil) // DIL_BLK
        n_blk = tq // DIL_BLK + dil
        _to_residue_major(q_ref, qd_ref, perm, dil, 0, tq // DIL_BLK)
        _to_residue_major(kw_ref, kd_ref, perm, dil, first_blk, n_blk)
        _to_residue_major(vw_ref, vd_ref, perm, dil, first_blk, n_blk)
        residue_major[dil] = (qd_ref, kd_ref, vd_ref)

    ii = lax.broadcasted_iota(jnp.int32, (bq, span), 0)
    jj = lax.broadcasted_iota(jnp.int32, (bq, span), 1)
    in_band = jnp.abs(jj - half - ii) <= half

    for g, (_, dil) in enumerate(C_PATTERNS):
        nb = tq // (bq * dil)

        def body(it, carry, g=g, dil=dil, nb=nb):
            res, sub = it // nb, it % nb
            q_row = res + dil * bq * sub
            w_row = halo + q_row - half * dil
            if dil == 1:
                q = q0_ref[pl.ds(pl.multiple_of(q_row, bq), bq), :]
                k = kw_ref[pl.ds(pl.multiple_of(w_row, half), span), :]
                v = vw_ref[pl.ds(pl.multiple_of(w_row, half), span), :]
            else:
                qd_ref, kd_ref, vd_ref = residue_major[dil]
                q_blk = sub * bq * dil // DIL_BLK
                k_blk = (halo - half * dil) // DIL_BLK + q_blk
                q = qd_ref[pl.ds(q_blk * dil + res, bq * dil // DIL_BLK, stride=dil)]
                k = kd_ref[pl.ds(k_blk * dil + res, span * dil // DIL_BLK, stride=dil)]
                v = vd_ref[pl.ds(k_blk * dil + res, span * dil // DIL_BLK, stride=dil)]
                q = q.reshape(bq, HEAD_DIM).astype(BF16)
                k = k.reshape(span, HEAD_DIM).astype(BF16)
                v = v.reshape(span, HEAD_DIM).astype(BF16)
            s = lax.dot_general(q, k, (((1,), (1,)), ((), ())), preferred_element_type=F32)
            ktok = t0 + q_row - half * dil + dil * jj
            s = jnp.where(in_band & (ktok >= 0) & (ktok < seq_len), s, NEG_BIG)
            m = jnp.max(s, axis=1, keepdims=True)
            p = jnp.exp2(s - m)
            pv, psum = _weighted_values(p, v)
            og_ref[g, pl.ds(q_row, bq, stride=dil), :] = pv / psum
            lg_ref[g, pl.ds(q_row, bq, stride=dil), :] = m + jnp.log2(psum)
            return carry

        lax.fori_loop(0, tq // bq, body, 0, unroll=True)

    l0, l1, l2 = lg_ref[0], lg_ref[1], lg_ref[2]
    mx = jnp.maximum(jnp.maximum(l0, l1), l2)
    e0, e1, e2 = jnp.exp2(l0 - mx), jnp.exp2(l1 - mx), jnp.exp2(l2 - mx)
    o_ref[...] = ((e0 * og_ref[0] + e1 * og_ref[1] + e2 * og_ref[2]) / (e0 + e1 + e2)).astype(o_ref.dtype)


def _dilated_attention(r_arr, p_arr, *, seq_len, tq=2048):
    tq = min(tq, seq_len)
    halo = DIL_HALO
    ratio, n_halo = tq // halo, seq_len // halo
    qc, kc, vc = ROPE_OFF["cq"] // HEAD_DIM, ROPE_OFF["ck"] // HEAD_DIM, PLAIN_OFF["cv"] // HEAD_DIM
    n_heads = BRANCH_WIDTH // HEAD_DIM

    def window_specs(col):
        return [pl.BlockSpec((halo, HEAD_DIM), lambda h, i: (jnp.maximum(i * ratio - 1, 0), col + h)),
                pl.BlockSpec((tq, HEAD_DIM), lambda h, i: (i, col + h)),
                pl.BlockSpec((halo, HEAD_DIM), lambda h, i: (jnp.minimum((i + 1) * ratio, n_halo - 1), col + h))]

    q_specs = [pl.BlockSpec((tq, HEAD_DIM), lambda h, i, g=g: (i, qc + n_heads * g + h)) for g in range(3)]
    perm_spec = pl.BlockSpec((DIL_BLK, DIL_BLK), lambda h, i: (0, 0))
    dilations = [d for _, d in C_PATTERNS[1:]]
    assert C_PATTERNS[0][1] == 1 and all(DIL_BLK % d == 0 and (DIL_BLK // d) % 8 == 0 for d in dilations)
    window = tq + 2 * halo
    residue_scratch = []
    for d in dilations:
        residue_scratch += [pltpu.VMEM((tq // DIL_BLK * d, DIL_BLK // d, HEAD_DIM), F32),
                            pltpu.VMEM((window // DIL_BLK * d, DIL_BLK // d, HEAD_DIM), F32),
                            pltpu.VMEM((window // DIL_BLK * d, DIL_BLK // d, HEAD_DIM), F32)]
    return pl.pallas_call(
        functools.partial(_dilated_kernel, seq_len=seq_len, tq=tq),
        out_shape=jax.ShapeDtypeStruct((seq_len, BRANCH_WIDTH), BF16),
        grid=(n_heads, seq_len // tq),
        in_specs=q_specs + window_specs(kc) + window_specs(vc) + [perm_spec] * len(dilations),
        out_specs=pl.BlockSpec((tq, HEAD_DIM), lambda h, i: (i, h)),
        scratch_shapes=[pltpu.VMEM((window, HEAD_DIM), BF16), pltpu.VMEM((window, HEAD_DIM), BF16)]
        + residue_scratch + [pltpu.VMEM((3, tq, HEAD_DIM), F32), pltpu.VMEM((3, tq, HEAD_DIM), F32)],
        compiler_params=_params(("arbitrary", "arbitrary")),
        name="dilated_attention",
    )(r_arr, r_arr, r_arr, r_arr, r_arr, r_arr, p_arr, p_arr, p_arr,
      *[jnp.asarray(_residue_major(d), BF16) for d in dilations])


NBR_ROWS = 4
NBR_UNION = 12
NBR_SUB = 16


def _nbr_bias_kernel(rpb_ref, o_ref, t_ref):
    h = pl.program_id(0)
    n_dr, n_dc = 2 * B_WIN_ROWS - 1, 2 * B_WIN_COLS - 1
    qc = lax.broadcasted_iota(jnp.int32, (GRID_W, GRID_W), 0)
    kc = lax.broadcasted_iota(jnp.int32, (GRID_W, GRID_W), 1)
    col_start = jnp.clip(qc - B_WIN_COLS // 2, 0, GRID_W - B_WIN_COLS)
    col_ok = (kc >= col_start) & (kc < col_start + B_WIN_COLS)
    dc = kc - qc + (B_WIN_COLS - 1)
    for dr in range(n_dr):
        t = jnp.full((GRID_W, GRID_W), NEG_BIG, F32)
        for d in range(n_dc):
            t = jnp.where(dc == d, rpb_ref[(h * n_dr + dr) * n_dc + d] * LOG2E, t)
        t_ref[dr] = jnp.where(col_ok, t, NEG_BIG)
    o_ref[...] = jnp.full(o_ref.shape, NEG_BIG, F32)
    half = B_WIN_ROWS // 2
    for case in range(3):
        for a in range(NBR_ROWS):
            win, off = ((0, a), (a, half), (NBR_UNION - B_WIN_ROWS, half + a))[case]
            for j in range(B_WIN_ROWS):
                o_ref[case, a * GRID_W:(a + 1) * GRID_W, (win + j) * GRID_W:(win + j + 1) * GRID_W] = (
                    t_ref[j - off + B_WIN_ROWS - 1])


def _nbr_bias(rpb):
    n_heads = rpb.shape[0]
    shape = (3, NBR_ROWS * GRID_W, NBR_UNION * GRID_W)
    return pl.pallas_call(
        _nbr_bias_kernel,
        out_shape=jax.ShapeDtypeStruct((n_heads,) + shape, F32),
        grid=(n_heads,),
        in_specs=[pl.BlockSpec(memory_space=pltpu.SMEM)],
        out_specs=pl.BlockSpec((None,) + shape, lambda h: (h, 0, 0, 0)),
        scratch_shapes=[pltpu.VMEM((2 * B_WIN_ROWS - 1, GRID_W, GRID_W), F32)],
        compiler_params=_params(("arbitrary",)),
        name="nbr_bias",
    )(rpb.reshape(-1))


def _nbr_kernel(q_ref, k_ref, v_ref, b_first_ref, b_mid_ref, b_last_ref, o_ref, *, n_rows):
    step = pl.program_id(1)
    nq, nk = NBR_ROWS * GRID_W, NBR_UNION * GRID_W
    for sb in range(NBR_SUB):
        bias_ref = b_first_ref if sb == 0 else (b_last_ref if sb == NBR_SUB - 1 else b_mid_ref)
        r0 = (step * NBR_SUB + sb) * NBR_ROWS
        u0 = jnp.clip(r0 - B_WIN_ROWS // 2, 0, n_rows - NBR_UNION)
        kstart = pl.multiple_of(u0 * GRID_W, GRID_W)
        k = k_ref[pl.ds(kstart, nk), :]
        v = v_ref[pl.ds(kstart, nk), :]
        q = q_ref[sb * nq:(sb + 1) * nq, :]
        s = lax.dot_general(q, k, (((1,), (1,)), ((), ())), preferred_element_type=F32) + bias_ref[...]
        m = jnp.max(s, axis=1, keepdims=True)
        p = jnp.exp2(s - m)
        pv, psum = _weighted_values(p, v)
        o_ref[sb * nq:(sb + 1) * nq, :] = (pv / psum).astype(o_ref.dtype)


def _nbr_attention(p_arr, bias, *, seq_len):
    n_rows = seq_len // GRID_W
    tq = NBR_SUB * NBR_ROWS * GRID_W
    n_steps = seq_len // tq
    qc, kc, vc = (PLAIN_OFF[n] // HEAD_DIM for n in ("bq", "bk", "bv"))
    tile = (None, None, NBR_ROWS * GRID_W, NBR_UNION * GRID_W)
    return pl.pallas_call(
        functools.partial(_nbr_kernel, n_rows=n_rows),
        out_shape=jax.ShapeDtypeStruct((seq_len, BRANCH_WIDTH), BF16),
        grid=(BRANCH_WIDTH // HEAD_DIM, n_steps),
        in_specs=[pl.BlockSpec((tq, HEAD_DIM), lambda h, i: (i, qc + h)),
                  pl.BlockSpec((seq_len, HEAD_DIM), lambda h, i: (0, kc + h)),
                  pl.BlockSpec((seq_len, HEAD_DIM), lambda h, i: (0, vc + h)),
                  pl.BlockSpec(tile, lambda h, i: (h, jnp.where(i == 0, 0, 1), 0, 0)),
                  pl.BlockSpec(tile, lambda h, i: (h, 1, 0, 0)),
                  pl.BlockSpec(tile, lambda h, i: (h, jnp.where(i == n_steps - 1, 2, 1), 0, 0))],
        out_specs=pl.BlockSpec((tq, HEAD_DIM), lambda h, i: (i, h)),
        compiler_params=_params(("arbitrary", "arbitrary")),
        name="nbr_attention",
    )(p_arr, p_arr, p_arr, bias, bias, bias)


DENSE_RB = 256


def _dense_kernel(q_ref, k_ref, v_ref, o_ref, q2_ref, v2_ref, m_ref, acc_ref, s0_ref, *, seq_len, tq, tk):
    @pl.when(pl.program_id(1) == 0)
    def _():
        v2_ref[:, :HEAD_DIM] = v_ref[...]
        v2_ref[:, HEAD_DIM:] = jnp.ones((seq_len, HEAD_DIM), BF16)

    q2_ref[:tq] = q_ref[:, :HEAD_DIM]
    q2_ref[tq:] = q_ref[:, HEAD_DIM:]
    m_ref[...] = jnp.full(m_ref.shape, NEG_BIG, F32)
    acc_ref[...] = jnp.zeros(acc_ref.shape, F32)
    n_chunks = seq_len // tk

    def logits(r, k):
        return lax.dot_general(q2_ref[pl.ds(r * DENSE_RB, DENSE_RB), :], k, (((1,), (1,)), ((), ())),
                               preferred_element_type=F32)

    s0_ref[...] = logits(0, k_ref[pl.ds(0, tk), :])

    def body(j, carry):
        koff = pl.multiple_of(j * tk, tk)
        k = k_ref[pl.ds(koff, tk), :]
        v2 = v2_ref[pl.ds(koff, tk), :]
        for r in range((2 * tq) // DENSE_RB):
            rows = pl.ds(r * DENSE_RB, DENSE_RB)
            s = s0_ref[...] if r == 0 else logits(r, k)
            m_prev = m_ref[rows, :]
            m_new = jnp.maximum(m_prev, jnp.max(s, axis=1, keepdims=True))
            alpha = jnp.exp2(m_prev - m_new)
            p = jnp.exp2(s - m_new).astype(BF16)
            acc_ref[rows, :] = acc_ref[rows, :] * alpha + jnp.dot(p, v2, preferred_element_type=F32)
            m_ref[rows, :] = m_new
        next_off = pl.multiple_of(jnp.minimum(j + 1, n_chunks - 1) * tk, tk)
        s0_ref[...] = logits(0, k_ref[pl.ds(next_off, tk), :])
        return carry

    lax.fori_loop(0, n_chunks, body, 0)
    acc = acc_ref[...]
    o = acc[:, :HEAD_DIM] / acc[:, HEAD_DIM:]
    o_ref[:, :HEAD_DIM] = o[:tq].astype(o_ref.dtype)
    o_ref[:, HEAD_DIM:] = o[tq:].astype(o_ref.dtype)


def _dense_attention(n_arr, p_arr, *, seq_len, tq=2048, tk=2048):
    tq, tk = min(tq, seq_len), min(tk, seq_len)
    kc = NORM_OFF["dk"] // HEAD_DIM
    vc = PLAIN_OFF["dv"] // HEAD_DIM
    return pl.pallas_call(
        functools.partial(_dense_kernel, seq_len=seq_len, tq=tq, tk=tk),
        out_shape=jax.ShapeDtypeStruct((seq_len, BRANCH_WIDTH), BF16),
        grid=(2, seq_len // tq),
        in_specs=[pl.BlockSpec((tq, 2 * HEAD_DIM), lambda g, i: (i, g)),
                  pl.BlockSpec((seq_len, HEAD_DIM), lambda g, i: (0, kc + g), pipeline_mode=pl.Buffered(1)),
                  pl.BlockSpec((seq_len, HEAD_DIM), lambda g, i: (0, vc + g), pipeline_mode=pl.Buffered(1))],
        out_specs=pl.BlockSpec((tq, 2 * HEAD_DIM), lambda g, i: (i, g)),
        scratch_shapes=[pltpu.VMEM((2 * tq, HEAD_DIM), BF16), pltpu.VMEM((seq_len, 2 * HEAD_DIM), BF16),
                        pltpu.VMEM((2 * tq, 1), F32), pltpu.VMEM((2 * tq, 2 * HEAD_DIM), F32),
                        pltpu.VMEM((DENSE_RB, tk), F32)],
        compiler_params=_params(("arbitrary", "arbitrary")),
        name="dense_attention",
    )(n_arr, n_arr, p_arr)


def _merge_kernel(ya_ref, yb_ref, yc_ref, yd_ref, z_ref, gate_ref, x_ref, wb_ref, wo_ref, g_ref, *out_refs, final):
    merged = None
    for n, y_ref in enumerate((ya_ref, yb_ref, yc_ref, yd_ref)):
        z = z_ref[:, n * BRANCH_WIDTH:(n + 1) * BRANCH_WIDTH].astype(F32)
        t = (y_ref[...].astype(F32) * (z * _sigmoid(z))).astype(BF16)
        br = jnp.dot(t, wb_ref[n], preferred_element_type=F32)
        term = gate_ref[:, n * D_MODEL:(n + 1) * D_MODEL].astype(F32) * br
        merged = term if merged is None else merged + term
    x_new = x_ref[...] + jnp.dot(merged.astype(BF16), wo_ref[...], preferred_element_type=F32)
    normed = x_new * lax.rsqrt(jnp.mean(x_new * x_new, axis=-1, keepdims=True) + NORM_EPS) * g_ref[...]
    if final:
        out_refs[0][...] = normed
    else:
        out_refs[0][...] = x_new
        out_refs[1][...] = normed.astype(BF16)


def _merge(ys, p_arr, gates, x, wb, wo, g_next, *, final, tm=256):
    s = x.shape[0]
    row = lambda w: pl.BlockSpec((tm, w), lambda i: (i, 0))
    once = dict(pipeline_mode=pl.Buffered(1))
    in_specs = ([row(BRANCH_WIDTH)] * N_BRANCH
                + [row(N_BRANCH * BRANCH_WIDTH), row(N_BRANCH * D_MODEL), row(D_MODEL),
                   pl.BlockSpec((N_BRANCH, BRANCH_WIDTH, D_MODEL), lambda i: (0, 0, 0), **once),
                   pl.BlockSpec((D_MODEL, D_MODEL), lambda i: (0, 0), **once),
                   pl.BlockSpec((1, D_MODEL), lambda i: (0, 0))])
    if final:
        out_shape = [jax.ShapeDtypeStruct((s, D_MODEL), F32)]
    else:
        out_shape = [jax.ShapeDtypeStruct((s, D_MODEL), F32), jax.ShapeDtypeStruct((s, D_MODEL), BF16)]
    return pl.pallas_call(
        functools.partial(_merge_kernel, final=final),
        out_shape=out_shape,
        grid=(s // tm,),
        in_specs=in_specs,
        out_specs=[row(D_MODEL)] * len(out_shape),
        compiler_params=_params(("arbitrary",)),
        name="merge",
    )(*ys, p_arr, gates, x, wb, wo, g_next.reshape(1, D_MODEL))


@functools.lru_cache(maxsize=None)
def _rope_tables(seq_len):
    t = np.arange(seq_len)

    def angles(pos, dim):
        inv_freq = np.float32(ROPE_THETA) ** (-np.arange(dim // 2, dtype=np.float32) / np.float32(dim // 2))
        return pos.astype(np.float32)[:, None] * inv_freq.astype(np.float32)[None, :]

    a1 = angles(t, HEAD_DIM)
    cos1 = np.concatenate([np.cos(a1), np.cos(a1)], axis=1)
    sin1 = np.concatenate([-np.sin(a1), np.sin(a1)], axis=1)
    ar = angles(t // GRID_W, HEAD_DIM // 2)
    ac = angles(t % GRID_W, HEAD_DIM // 2)
    cos2 = np.concatenate([np.cos(ar), np.cos(ac), np.cos(ar), np.cos(ac)], axis=1)
    sin2 = np.concatenate([-np.sin(ar), -np.sin(ac), np.sin(ar), np.sin(ac)], axis=1)
    assert all(a.dtype == np.float32 for a in (cos1, sin1, cos2, sin2))
    return (cos1, sin1), (cos2, sin2)


def _axial_reorder(a):
    lead = a.shape[:-1]
    return a.reshape(*lead, -1, 2, 2, HEAD_DIM // 4).swapaxes(-3, -2).reshape(*lead, -1)


def _gather_cols(w, seg):
    cols = np.concatenate([np.arange(_OFF[n], _OFF[n] + _WID[n]) for n in seg])
    runs = np.split(cols, np.nonzero(np.diff(cols) != 1)[0] + 1)
    return jnp.concatenate([w[:, r[0]:r[-1] + 1] for r in runs], axis=1).astype(BF16)


def _col_scale(seg, scaled):
    return jnp.asarray(np.concatenate(
        [np.full((_WID[n],), scaled.get(n, 1.0), np.float32) for n in seg]).reshape(1, -1))


def _layer(x, h, w_in_all, layer, norm_gain, gate_b, a_sink, b_rpb, d_q_norm, d_k_norm, w_branch, w_out, g_next,
           tables, final):
    s = x.shape[0]
    (cos1, sin1), (cos2, sin2) = tables
    tm = 1024
    w_in = w_in_all[layer, :, :_OFF["gates"]]

    gains = _axial_reorder(jnp.concatenate([jnp.tile(d_q_norm, _WID["dq"] // HEAD_DIM),
                                            jnp.tile(d_k_norm, _WID["dk"] // HEAD_DIM)]))
    n_arr = _proj(x if h is None else h, _axial_reorder(_gather_cols(w_in, NORM_SEG)), "norm_rope",
                  [_col_scale(NORM_SEG, {"dq": SCALE}), gains.reshape(1, -1)],
                  [cos2, sin2], tm=tm, tn=NORM_W, name="proj_norm_rope", norm_gain=norm_gain if h is None else None)
    if h is None:
        n_arr, h = n_arr
    r_arr = _proj(h, _gather_cols(w_in, ROPE_SEG), "rope", [_col_scale(ROPE_SEG, {"aq": SCALE, "cq": SCALE})],
                  [cos1, sin1], tm=tm, tn=ROPE_W // 2, name="proj_rope")
    p_arr = _proj(h, _gather_cols(w_in, PLAIN_SEG), "scale", [_col_scale(PLAIN_SEG, {"bq": SCALE})], [],
                  tm=tm, tn=PLAIN_W // 3, name="proj_plain")
    gates = _proj_gates(h, w_in_all, layer, gate_b.reshape(1, -1))

    ya = _window_attention(r_arr, p_arr, a_sink, seq_len=s)
    yb = _nbr_attention(p_arr, _nbr_bias(b_rpb), seq_len=s)
    yc = _dilated_attention(r_arr, p_arr, seq_len=s)
    yd = _dense_attention(n_arr, p_arr, seq_len=s)
    return _merge((ya, yb, yc, yd), p_arr, gates, x, w_branch.astype(BF16), w_out.astype(BF16), g_next, final=final)


def kernel(x, norm_g, w_in, gate_b, a_sink, b_rpb, d_q_norm, d_k_norm, w_branch, w_out, final_norm_g):
    b, s, d = x.shape
    depth = w_in.shape[0]
    tables = tuple(tuple(jnp.asarray(a) for a in grp) for grp in _rope_tables(s))
    outs = []
    for bi in range(b):
        xb = x.reshape(s, d) if b == 1 else x[bi]
        h = None
        for layer in range(depth):
            final = layer == depth - 1
            g_next = final_norm_g if final else norm_g[layer + 1]
            res = _layer(xb, h, w_in, layer, norm_g[layer], gate_b[layer], a_sink[layer], b_rpb[layer], d_q_norm[layer],
                         d_k_norm[layer], w_branch[layer], w_out[layer], g_next, tables, final)
            if final:
                xb = res[0]
            else:
                xb, h = res
        outs.append(xb)
    return outs[0].reshape(1, s, d) if b == 1 else jnp.stack(outs, axis=0)
```

```python
import functools
import math

import numpy as np
import jax
import jax.numpy as jnp
from jax import lax
from jax.experimental import pallas as pl
from jax.experimental.pallas import tpu as pltpu

F32 = jnp.float32
BF16 = jnp.bfloat16

D_MODEL = 2048
HEAD_DIM = 128
BRANCH_WIDTH = 512
N_BRANCH = 4
A_HALF_WINDOW = 128
B_WIN_ROWS = 8
B_WIN_COLS = 16
C_PATTERNS = ((128, 1), (512, 4), (2048, 16))
GRID_W = 64
ROPE_THETA = 10000.0
NORM_EPS = 1e-6
LOG2E = math.log2(math.e)
SCALE = HEAD_DIM ** -0.5 * LOG2E
NEG_BIG = -1e30

VMEM_LIMIT = 56 * 1024 * 1024

_SPLITS = (512, 256, 256, 512, 512, 512, 512, 512, 1536, 512, 512, 512, 512, 256, 256, 512, 8192)
_NAMES = ("aq", "ak", "av", "az", "bq", "bk", "bv", "bz", "cq", "ck", "cv", "cz", "dq", "dk", "dv", "dz", "gates")
_OFF = dict(zip(_NAMES, np.concatenate([[0], np.cumsum(_SPLITS)[:-1]]).tolist()))
_WID = dict(zip(_NAMES, _SPLITS))

ROPE_SEG = ("aq", "ak", "cq", "ck")
NORM_SEG = ("dq", "dk")
PLAIN_SEG = ("az", "bz", "cz", "dz", "av", "dv", "bq", "bk", "bv", "cv")


def _seg_offsets(seg):
    offs, o = {}, 0
    for n in seg:
        offs[n] = o
        o += _WID[n]
    return offs, o


ROPE_OFF, ROPE_W = _seg_offsets(ROPE_SEG)
NORM_OFF, NORM_W = _seg_offsets(NORM_SEG)
PLAIN_OFF, PLAIN_W = _seg_offsets(PLAIN_SEG)


def _params(sem):
    return pltpu.CompilerParams(dimension_semantics=sem, vmem_limit_bytes=VMEM_LIMIT)


def _sigmoid(v):
    return 1.0 / (1.0 + jnp.exp(-v))


def _weighted_values(p, v):
    v1 = jnp.concatenate([v, jnp.ones_like(v)], axis=1)
    r = jnp.dot(p.astype(BF16), v1, preferred_element_type=F32)
    return r[:, :HEAD_DIM], r[:, HEAD_DIM:]


PROJ_RB = 256


def _rope_half(xh, cos, sin_signed):
    return xh * cos + pltpu.roll(xh, HEAD_DIM // 2, 1) * sin_signed


def _proj_kernel(*refs, mode, norm_input):
    if norm_input:
        g_in_ref, refs, h_out_ref = refs[0], refs[1:-1], refs[-1]
    h_ref, w_ref = refs[0], refs[1]
    o_ref = refs[-1]
    tm, tn = o_ref.shape
    for r in range(tm // PROJ_RB):
        rows = pl.ds(r * PROJ_RB, PROJ_RB)
        if norm_input:
            x = h_ref[rows, :]
            h = (x * lax.rsqrt(jnp.mean(x * x, axis=-1, keepdims=True) + NORM_EPS) * g_in_ref[...]).astype(BF16)
            h_out_ref[rows, :] = h
        else:
            h = h_ref[rows, :]
        acc = jnp.dot(h, w_ref[...], preferred_element_type=F32)
        if mode == "scale":
            o_ref[rows, :] = (acc * refs[2][...]).astype(o_ref.dtype)
        elif mode == "rope":
            cs_ref, cos_ref, sin_ref = refs[2:5]
            cos, sin = cos_ref[rows, :], sin_ref[rows, :]
            for c in range(tn // HEAD_DIM):
                sl = slice(c * HEAD_DIM, (c + 1) * HEAD_DIM)
                o_ref[rows, sl] = (_rope_half(acc[:, sl], cos, sin) * cs_ref[:, sl]).astype(o_ref.dtype)
        elif mode == "norm_rope":
            cs_ref, g_ref, cos_ref, sin_ref = refs[2:6]
            cos, sin = cos_ref[rows, :], sin_ref[rows, :]
            for c in range(tn // HEAD_DIM):
                sl = slice(c * HEAD_DIM, (c + 1) * HEAD_DIM)
                xh = acc[:, sl]
                y = xh * lax.rsqrt(jnp.mean(xh * xh, axis=-1, keepdims=True) + NORM_EPS) * g_ref[:, sl]
                o_ref[rows, sl] = (_rope_half(y, cos, sin) * cs_ref[:, sl]).astype(o_ref.dtype)
        else:
            raise ValueError(mode)


def _proj(h, w, mode, row_vecs, tables, tm, tn, name, norm_gain=None):
    s, d = h.shape
    n = w.shape[1]
    tm = min(tm, s)
    in_specs = [pl.BlockSpec((tm, d), lambda i, j: (i, 0)), pl.BlockSpec((d, tn), lambda i, j: (0, j))]
    in_specs += [pl.BlockSpec((1, tn), lambda i, j: (0, j)) for _ in row_vecs]
    in_specs += [pl.BlockSpec((tm, HEAD_DIM), lambda i, j: (i, 0)) for _ in tables]
    out_shape = [jax.ShapeDtypeStruct((s, n), BF16)]
    out_specs = [pl.BlockSpec((tm, tn), lambda i, j: (i, j))]
    args = [h, w, *row_vecs, *tables]
    if norm_gain is not None:
        assert n == tn
        in_specs = [pl.BlockSpec((1, d), lambda i, j: (0, 0))] + in_specs
        args = [norm_gain.reshape(1, d)] + args
        out_shape.append(jax.ShapeDtypeStruct((s, d), BF16))
        out_specs.append(pl.BlockSpec((tm, d), lambda i, j: (i, 0)))
    outs = pl.pallas_call(
        functools.partial(_proj_kernel, mode=mode, norm_input=norm_gain is not None),
        out_shape=out_shape,
        grid=(s // tm, n // tn),
        in_specs=in_specs,
        out_specs=out_specs,
        compiler_params=_params(("arbitrary", "arbitrary")),
        name=name,
    )(*args)
    return outs[0] if norm_gain is None else outs


def _gates_kernel(h_ref, w_ref, b_ref, o_ref, wb_ref):
    @pl.when(pl.program_id(1) == 0)
    def _():
        wb_ref[...] = w_ref[...].astype(BF16)

    for r in range(o_ref.shape[0] // PROJ_RB):
        rows = pl.ds(r * PROJ_RB, PROJ_RB)
        acc = jnp.dot(h_ref[rows, :], wb_ref[...], preferred_element_type=F32)
        o_ref[rows, :] = _sigmoid(acc + b_ref[...]).astype(o_ref.dtype)


def _proj_gates(h, w_in_all, layer, bias, tm=512, tn=2048):
    s, d = h.shape
    tm = min(tm, s)
    n = _WID["gates"]
    first = _OFF["gates"] // tn
    return pl.pallas_call(
        _gates_kernel,
        out_shape=jax.ShapeDtypeStruct((s, n), BF16),
        grid=(n // tn, s // tm),
        in_specs=[pl.BlockSpec((tm, d), lambda j, i: (i, 0)),
                  pl.BlockSpec((None, d, tn), lambda j, i: (layer, 0, first + j)),
                  pl.BlockSpec((1, tn), lambda j, i: (0, j))],
        out_specs=pl.BlockSpec((tm, tn), lambda j, i: (i, j)),
        scratch_shapes=[pltpu.VMEM((d, tn), BF16)],
        compiler_params=_params(("arbitrary", "arbitrary")),
        name="proj_gates",
    )(h, w_in_all, bias)


WIN_BQ = 128


def _window_consts(sink_ref, kv_head):
    bq, groups = WIN_BQ, 2
    span, rows = bq + 2 * A_HALF_WINDOW, groups * bq
    row = lax.broadcasted_iota(jnp.int32, (rows, span), 0)
    col_minus_row = lax.broadcasted_iota(jnp.int32, (rows, span), 1) - jnp.where(row >= bq, row - bq, row)
    r1 = lax.broadcasted_iota(jnp.int32, (rows, 1), 0)
    sink = jnp.where(r1 >= bq, sink_ref[kv_head * groups + 1], sink_ref[kv_head * groups]) * LOG2E
    return col_minus_row, sink


def _window_block(consts, q_ref, k_ref, v_ref, o_ref, q0, qoff, seq_len):
    col_minus_row, sink = consts
    bq, hw = WIN_BQ, A_HALF_WINDOW
    span = bq + 2 * hw
    start = pl.multiple_of(jnp.clip(q0 - hw, 0, seq_len - span), bq)
    k = k_ref[pl.ds(start, span), :]
    v = v_ref[pl.ds(start, span), :]
    qb = q_ref[pl.ds(qoff, bq), :]
    q2 = jnp.concatenate([qb[:, :HEAD_DIM], qb[:, HEAD_DIM:]], axis=0)
    s = lax.dot_general(q2, k, (((1,), (1,)), ((), ())), preferred_element_type=F32)
    s = jnp.where(jnp.abs(col_minus_row + (start - q0)) <= hw, s, NEG_BIG)
    m = jnp.maximum(jnp.max(s, axis=1, keepdims=True), sink)
    p = jnp.exp2(s - m)
    pv, psum = _weighted_values(p, v)
    o = pv / (psum + jnp.exp2(sink - m))
    o_ref[pl.ds(qoff, bq), :HEAD_DIM] = o[:bq].astype(o_ref.dtype)
    o_ref[pl.ds(qoff, bq), HEAD_DIM:] = o[bq:].astype(o_ref.dtype)


def _window_kernel(sink_ref, q_ref, k_ref, v_ref, o_ref, *, seq_len, nsub):
    consts = _window_consts(sink_ref, pl.program_id(0))
    t0 = pl.program_id(1) * nsub * WIN_BQ

    def body(b, carry):
        qoff = pl.multiple_of(b * WIN_BQ, WIN_BQ)
        _window_block(consts, q_ref, k_ref, v_ref, o_ref, t0 + qoff, qoff, seq_len)
        return carry

    lax.fori_loop(0, nsub, body, 0, unroll=True)


def _window_attention(r_arr, p_arr, sink, *, seq_len, max_step=1024):
    nsub = min(max_step, seq_len) // WIN_BQ
    tq = nsub * WIN_BQ
    qc, kc, vc = ROPE_OFF["aq"] // (2 * HEAD_DIM), ROPE_OFF["ak"] // HEAD_DIM, PLAIN_OFF["av"] // HEAD_DIM
    return pl.pallas_call(
        functools.partial(_window_kernel, seq_len=seq_len, nsub=nsub),
        out_shape=jax.ShapeDtypeStruct((seq_len, BRANCH_WIDTH), BF16),
        grid=(2, seq_len // tq),
        in_specs=[pl.BlockSpec(memory_space=pltpu.SMEM),
                  pl.BlockSpec((tq, 2 * HEAD_DIM), lambda c, i: (i, qc + c)),
                  pl.BlockSpec((seq_len, HEAD_DIM), lambda c, i: (0, kc + c)),
                  pl.BlockSpec((seq_len, HEAD_DIM), lambda c, i: (0, vc + c))],
        out_specs=pl.BlockSpec((tq, 2 * HEAD_DIM), lambda c, i: (i, c)),
        compiler_params=_params(("arbitrary", "arbitrary")),
        name="window_attention",
    )(sink, r_arr, r_arr, p_arr)


DIL_BQ = 128
DIL_HALF = 64
DIL_HALO = DIL_HALF * max(d for _, d in C_PATTERNS)


DIL_BLK = 128


def _residue_major(dil):
    per = DIL_BLK // dil
    p = np.zeros((DIL_BLK, DIL_BLK), np.float32)
    for r in range(dil):
        for i in range(per):
            p[r * per + i, i * dil + r] = 1.0
    return p


def _to_residue_major(src_ref, dst_ref, perm, dil, first_blk, n_blk):
    per = DIL_BLK // dil
    for b in range(first_blk, first_blk + n_blk, 2):
        pair = jnp.concatenate([src_ref[b * DIL_BLK:(b + 1) * DIL_BLK, :],
                                src_ref[(b + 1) * DIL_BLK:(b + 2) * DIL_BLK, :]], axis=1)
        res = jnp.dot(perm, pair, preferred_element_type=F32)
        dst_ref[b * dil:(b + 1) * dil] = res[:, :HEAD_DIM].reshape(dil, per, HEAD_DIM)
        dst_ref[(b + 1) * dil:(b + 2) * dil] = res[:, HEAD_DIM:].reshape(dil, per, HEAD_DIM)


def _dilated_kernel(q0_ref, q1_ref, q2_ref, kp_ref, kc_ref, kn_ref, vp_ref, vc_ref, vn_ref, perm1_ref, perm2_ref,
                    o_ref, kw_ref, vw_ref, qd1_ref, kd1_ref, vd1_ref, qd2_ref, kd2_ref, vd2_ref, og_ref, lg_ref,
                    *, seq_len, tq):
    t0 = pl.program_id(1) * tq
    halo, bq, half = DIL_HALO, DIL_BQ, DIL_HALF
    span = bq + 2 * half
    for w_ref, prev, cur, nxt in ((kw_ref, kp_ref, kc_ref, kn_ref), (vw_ref, vp_ref, vc_ref, vn_ref)):
        w_ref[:halo] = prev[...]
        w_ref[halo:halo + tq] = cur[...]
        w_ref[halo + tq:] = nxt[...]
    residue_major = {}
    for (_, dil), q_ref, perm_ref, qd_ref, kd_ref, vd_ref in (
            (C_PATTERNS[1], q1_ref, perm1_ref, qd1_ref, kd1_ref, vd1_ref),
            (C_PATTERNS[2], q2_ref, perm2_ref, qd2_ref, kd2_ref, vd2_ref)):
        perm = perm_ref[...]
        first_blk = (halo - half * dil) // DIL_BLK
        n_blk = tq // DIL_BLK + dil
        _to_residue_major(q_ref, qd_ref, perm, dil, 0, tq // DIL_BLK)
        _to_residue_major(kw_ref, kd_ref, perm, dil, first_blk, n_blk)
        _to_residue_major(vw_ref, vd_ref, perm, dil, first_blk, n_blk)
        residue_major[dil] = (qd_ref, kd_ref, vd_ref)

    ii = lax.broadcasted_iota(jnp.int32, (bq, span), 0)
    jj = lax.broadcasted_iota(jnp.int32, (bq, span), 1)
    in_band = jnp.abs(jj - half - ii) <= half

    for g, (_, dil) in enumerate(C_PATTERNS):
        nb = tq // (bq * dil)

        def body(it, carry, g=g, dil=dil, nb=nb):
            res, sub = it // nb, it % nb
            q_row = res + dil * bq * sub
            w_row = halo + q_row - half * dil
            if dil == 1:
                q = q0_ref[pl.ds(pl.multiple_of(q_row, bq), bq), :]
                k = kw_ref[pl.ds(pl.multiple_of(w_row, half), span), :]
                v = vw_ref[pl.ds(pl.multiple_of(w_row, half), span), :]
            else:
                qd_ref, kd_ref, vd_ref = residue_major[dil]
                q_blk = sub * bq * dil // DIL_BLK
                k_blk = (halo - half * dil) // DIL_BLK + q_blk
                q = qd_ref[pl.ds(q_blk * dil + res, bq * dil // DIL_BLK, stride=dil)]
                k = kd_ref[pl.ds(k_blk * dil + res, span * dil // DIL_BLK, stride=dil)]
                v = vd_ref[pl.ds(k_blk * dil + res, span * dil // DIL_BLK, stride=dil)]
                q = q.reshape(bq, HEAD_DIM).astype(BF16)
                k = k.reshape(span, HEAD_DIM).astype(BF16)
                v = v.reshape(span, HEAD_DIM).astype(BF16)
            s = lax.dot_general(q, k, (((1,), (1,)), ((), ())), preferred_element_type=F32)
            ktok = t0 + q_row - half * dil + dil * jj
            s = jnp.where(in_band & (ktok >= 0) & (ktok < seq_len), s, NEG_BIG)
            m = jnp.max(s, axis=1, keepdims=True)
            p = jnp.exp2(s - m)
            pv, psum = _weighted_values(p, v)
            og_ref[g, pl.ds(q_row, bq, stride=dil), :] = pv / psum
            lg_ref[g, pl.ds(q_row, bq, stride=dil), :] = m + jnp.log2(psum)
            return carry

        lax.fori_loop(0, tq // bq, body, 0, unroll=True)

    l0, l1, l2 = lg_ref[0], lg_ref[1], lg_ref[2]
    mx = jnp.maximum(jnp.maximum(l0, l1), l2)
    e0, e1, e2 = jnp.exp2(l0 - mx), jnp.exp2(l1 - mx), jnp.exp2(l2 - mx)
    o_ref[...] = ((e0 * og_ref[0] + e1 * og_ref[1] + e2 * og_ref[2]) / (e0 + e1 + e2)).astype(o_ref.dtype)


def _dilated_attention(r_arr, p_arr, *, seq_len, tq=2048):
    tq = min(tq, seq_len)
    halo = DIL_HALO
    ratio, n_halo = tq // halo, seq_len // halo
    qc, kc, vc = ROPE_OFF["cq"] // HEAD_DIM, ROPE_OFF["ck"] // HEAD_DIM, PLAIN_OFF["cv"] // HEAD_DIM
    n_heads = BRANCH_WIDTH // HEAD_DIM

    def window_specs(col):
        return [pl.BlockSpec((halo, HEAD_DIM), lambda h, i: (jnp.maximum(i * ratio - 1, 0), col + h)),
                pl.BlockSpec((tq, HEAD_DIM), lambda h, i: (i, col + h)),
                pl.BlockSpec((halo, HEAD_DIM), lambda h, i: (jnp.minimum((i + 1) * ratio, n_halo - 1), col + h))]

    q_specs = [pl.BlockSpec((tq, HEAD_DIM), lambda h, i, g=g: (i, qc + n_heads * g + h)) for g in range(3)]
    perm_spec = pl.BlockSpec((DIL_BLK, DIL_BLK), lambda h, i: (0, 0))
    dilations = [d for _, d in C_PATTERNS[1:]]
    assert C_PATTERNS[0][1] == 1 and all(DIL_BLK % d == 0 and (DIL_BLK // d) % 8 == 0 for d in dilations)
    window = tq + 2 * halo
    residue_scratch = []
    for d in dilations:
        residue_scratch += [pltpu.VMEM((tq // DIL_BLK * d, DIL_BLK // d, HEAD_DIM), F32),
                            pltpu.VMEM((window // DIL_BLK * d, DIL_BLK // d, HEAD_DIM), F32),
                            pltpu.VMEM((window // DIL_BLK * d, DIL_BLK // d, HEAD_DIM), F32)]
    return pl.pallas_call(
        functools.partial(_dilated_kernel, seq_len=seq_len, tq=tq),
        out_shape=jax.ShapeDtypeStruct((seq_len, BRANCH_WIDTH), BF16),
        grid=(n_heads, seq_len // tq),
        in_specs=q_specs + window_specs(kc) + window_specs(vc) + [perm_spec] * len(dilations),
        out_specs=pl.BlockSpec((tq, HEAD_DIM), lambda h, i: (i, h)),
        scratch_shapes=[pltpu.VMEM((window, HEAD_DIM), BF16), pltpu.VMEM((window, HEAD_DIM), BF16)]
        + residue_scratch + [pltpu.VMEM((3, tq, HEAD_DIM), F32), pltpu.VMEM((3, tq, HEAD_DIM), F32)],
        compiler_params=_params(("arbitrary", "arbitrary")),
        name="dilated_attention",
    )(r_arr, r_arr, r_arr, r_arr, r_arr, r_arr, p_arr, p_arr, p_arr,
      *[jnp.asarray(_residue_major(d), BF16) for d in dilations])


NBR_ROWS = 4
NBR_UNION = 12
NBR_SUB = 16


def _nbr_bias_kernel(rpb_ref, o_ref, t_ref):
    h = pl.program_id(0)
    n_dr, n_dc = 2 * B_WIN_ROWS - 1, 2 * B_WIN_COLS - 1
    qc = lax.broadcasted_iota(jnp.int32, (GRID_W, GRID_W), 0)
    kc = lax.broadcasted_iota(jnp.int32, (GRID_W, GRID_W), 1)
    col_start = jnp.clip(qc - B_WIN_COLS // 2, 0, GRID_W - B_WIN_COLS)
    col_ok = (kc >= col_start) & (kc < col_start + B_WIN_COLS)
    dc = kc - qc + (B_WIN_COLS - 1)
    for dr in range(n_dr):
        t = jnp.full((GRID_W, GRID_W), NEG_BIG, F32)
        for d in range(n_dc):
            t = jnp.where(dc == d, rpb_ref[(h * n_dr + dr) * n_dc + d] * LOG2E, t)
        t_ref[dr] = jnp.where(col_ok, t, NEG_BIG)
    o_ref[...] = jnp.full(o_ref.shape, NEG_BIG, F32)
    half = B_WIN_ROWS // 2
    for case in range(3):
        for a in range(NBR_ROWS):
            win, off = ((0, a), (a, half), (NBR_UNION - B_WIN_ROWS, half + a))[case]
            for j in range(B_WIN_ROWS):
                o_ref[case, a * GRID_W:(a + 1) * GRID_W, (win + j) * GRID_W:(win + j + 1) * GRID_W] = (
                    t_ref[j - off + B_WIN_ROWS - 1])


def _nbr_bias(rpb):
    n_heads = rpb.shape[0]
    shape = (3, NBR_ROWS * GRID_W, NBR_UNION * GRID_W)
    return pl.pallas_call(
        _nbr_bias_kernel,
        out_shape=jax.ShapeDtypeStruct((n_heads,) + shape, F32),
        grid=(n_heads,),
        in_specs=[pl.BlockSpec(memory_space=pltpu.SMEM)],
        out_specs=pl.BlockSpec((None,) + shape, lambda h: (h, 0, 0, 0)),
        scratch_shapes=[pltpu.VMEM((2 * B_WIN_ROWS - 1, GRID_W, GRID_W), F32)],
        compiler_params=_params(("arbitrary",)),
        name="nbr_bias",
    )(rpb.reshape(-1))


def _nbr_kernel(q_ref, k_ref, v_ref, b_first_ref, b_mid_ref, b_last_ref, o_ref, *, n_rows):
    step = pl.program_id(1)
    nq, nk = NBR_ROWS * GRID_W, NBR_UNION * GRID_W
    for sb in range(NBR_SUB):
        bias_ref = b_first_ref if sb == 0 else (b_last_ref if sb == NBR_SUB - 1 else b_mid_ref)
        r0 = (step * NBR_SUB + sb) * NBR_ROWS
        u0 = jnp.clip(r0 - B_WIN_ROWS // 2, 0, n_rows - NBR_UNION)
        kstart = pl.multiple_of(u0 * GRID_W, GRID_W)
        k = k_ref[pl.ds(kstart, nk), :]
        v = v_ref[pl.ds(kstart, nk), :]
        q = q_ref[sb * nq:(sb + 1) * nq, :]
        s = lax.dot_general(q, k, (((1,), (1,)), ((), ())), preferred_element_type=F32) + bias_ref[...]
        m = jnp.max(s, axis=1, keepdims=True)
        p = jnp.exp2(s - m)
        pv, psum = _weighted_values(p, v)
        o_ref[sb * nq:(sb + 1) * nq, :] = (pv / psum).astype(o_ref.dtype)


def _nbr_attention(p_arr, bias, *, seq_len):
    n_rows = seq_len // GRID_W
    tq = NBR_SUB * NBR_ROWS * GRID_W
    n_steps = seq_len // tq
    qc, kc, vc = (PLAIN_OFF[n] // HEAD_DIM for n in ("bq", "bk", "bv"))
    tile = (None, None, NBR_ROWS * GRID_W, NBR_UNION * GRID_W)
    return pl.pallas_call(
        functools.partial(_nbr_kernel, n_rows=n_rows),
        out_shape=jax.ShapeDtypeStruct((seq_len, BRANCH_WIDTH), BF16),
        grid=(BRANCH_WIDTH // HEAD_DIM, n_steps),
        in_specs=[pl.BlockSpec((tq, HEAD_DIM), lambda h, i: (i, qc + h)),
                  pl.BlockSpec((seq_len, HEAD_DIM), lambda h, i: (0, kc + h)),
                  pl.BlockSpec((seq_len, HEAD_DIM), lambda h, i: (0, vc + h)),
                  pl.BlockSpec(tile, lambda h, i: (h, jnp.where(i == 0, 0, 1), 0, 0)),
                  pl.BlockSpec(tile, lambda h, i: (h, 1, 0, 0)),
                  pl.BlockSpec(tile, lambda h, i: (h, jnp.where(i == n_steps - 1, 2, 1), 0, 0))],
        out_specs=pl.BlockSpec((tq, HEAD_DIM), lambda h, i: (i, h)),
        compiler_params=_params(("arbitrary", "arbitrary")),
        name="nbr_attention",
    )(p_arr, p_arr, p_arr, bias, bias, bias)


DENSE_RB = 256


def _dense_kernel(q_ref, k_ref, v_ref, o_ref, q2_ref, v2_ref, m_ref, acc_ref, s0_ref, *, seq_len, tq, tk):
    @pl.when(pl.program_id(1) == 0)
    def _():
        v2_ref[:, :HEAD_DIM] = v_ref[...]
        v2_ref[:, HEAD_DIM:] = jnp.ones((seq_len, HEAD_DIM), BF16)

    q2_ref[:tq] = q_ref[:, :HEAD_DIM]
    q2_ref[tq:] = q_ref[:, HEAD_DIM:]
    m_ref[...] = jnp.full(m_ref.shape, NEG_BIG, F32)
    acc_ref[...] = jnp.zeros(acc_ref.shape, F32)
    n_chunks = seq_len // tk

    def logits(r, k):
        return lax.dot_general(q2_ref[pl.ds(r * DENSE_RB, DENSE_RB), :], k, (((1,), (1,)), ((), ())),
                               preferred_element_type=F32)

    s0_ref[...] = logits(0, k_ref[pl.ds(0, tk), :])

    def body(j, carry):
        koff = pl.multiple_of(j * tk, tk)
        k = k_ref[pl.ds(koff, tk), :]
        v2 = v2_ref[pl.ds(koff, tk), :]
        for r in range((2 * tq) // DENSE_RB):
            rows = pl.ds(r * DENSE_RB, DENSE_RB)
            s = s0_ref[...] if r == 0 else logits(r, k)
            m_prev = m_ref[rows, :]
            m_new = jnp.maximum(m_prev, jnp.max(s, axis=1, keepdims=True))
            alpha = jnp.exp2(m_prev - m_new)
            p = jnp.exp2(s - m_new).astype(BF16)
            acc_ref[rows, :] = acc_ref[rows, :] * alpha + jnp.dot(p, v2, preferred_element_type=F32)
            m_ref[rows, :] = m_new
        next_off = pl.multiple_of(jnp.minimum(j + 1, n_chunks - 1) * tk, tk)
        s0_ref[...] = logits(0, k_ref[pl.ds(next_off, tk), :])
        return carry

    lax.fori_loop(0, n_chunks, body, 0)
    acc = acc_ref[...]
    o = acc[:, :HEAD_DIM] / acc[:, HEAD_DIM:]
    o_ref[:, :HEAD_DIM] = o[:tq].astype(o_ref.dtype)
    o_ref[:, HEAD_DIM:] = o[tq:].astype(o_ref.dtype)


def _dense_attention(n_arr, p_arr, *, seq_len, tq=2048, tk=2048):
    tq, tk = min(tq, seq_len), min(tk, seq_len)
    kc = NORM_OFF["dk"] // HEAD_DIM
    vc = PLAIN_OFF["dv"] // HEAD_DIM
    return pl.pallas_call(
        functools.partial(_dense_kernel, seq_len=seq_len, tq=tq, tk=tk),
        out_shape=jax.ShapeDtypeStruct((seq_len, BRANCH_WIDTH), BF16),
        grid=(2, seq_len // tq),
        in_specs=[pl.BlockSpec((tq, 2 * HEAD_DIM), lambda g, i: (i, g)),
                  pl.BlockSpec((seq_len, HEAD_DIM), lambda g, i: (0, kc + g), pipeline_mode=pl.Buffered(1)),
                  pl.BlockSpec((seq_len, HEAD_DIM), lambda g, i: (0, vc + g), pipeline_mode=pl.Buffered(1))],
        out_specs=pl.BlockSpec((tq, 2 * HEAD_DIM), lambda g, i: (i, g)),
        scratch_shapes=[pltpu.VMEM((2 * tq, HEAD_DIM), BF16), pltpu.VMEM((seq_len, 2 * HEAD_DIM), BF16),
                        pltpu.VMEM((2 * tq, 1), F32), pltpu.VMEM((2 * tq, 2 * HEAD_DIM), F32),
                        pltpu.VMEM((DENSE_RB, tk), F32)],
        compiler_params=_params(("arbitrary", "arbitrary")),
        name="dense_attention",
    )(n_arr, n_arr, p_arr)


def _merge_kernel(ya_ref, yb_ref, yc_ref, yd_ref, z_ref, gate_ref, x_ref, wb_ref, wo_ref, g_ref, *out_refs, final):
    merged = None
    for n, y_ref in enumerate((ya_ref, yb_ref, yc_ref, yd_ref)):
        z = z_ref[:, n * BRANCH_WIDTH:(n + 1) * BRANCH_WIDTH].astype(F32)
        t = (y_ref[...].astype(F32) * (z * _sigmoid(z))).astype(BF16)
        br = jnp.dot(t, wb_ref[n], preferred_element_type=F32)
        term = gate_ref[:, n * D_MODEL:(n + 1) * D_MODEL].astype(F32) * br
        merged = term if merged is None else merged + term
    x_new = x_ref[...] + jnp.dot(merged.astype(BF16), wo_ref[...], preferred_element_type=F32)
    normed = x_new * lax.rsqrt(jnp.mean(x_new * x_new, axis=-1, keepdims=True) + NORM_EPS) * g_ref[...]
    if final:
        out_refs[0][...] = normed
    else:
        out_refs[0][...] = x_new
        out_refs[1][...] = normed.astype(BF16)


def _merge(ys, p_arr, gates, x, wb, wo, g_next, *, final, tm=256):
    s = x.shape[0]
    row = lambda w: pl.BlockSpec((tm, w), lambda i: (i, 0))
    once = dict(pipeline_mode=pl.Buffered(1))
    in_specs = ([row(BRANCH_WIDTH)] * N_BRANCH
                + [row(N_BRANCH * BRANCH_WIDTH), row(N_BRANCH * D_MODEL), row(D_MODEL),
                   pl.BlockSpec((N_BRANCH, BRANCH_WIDTH, D_MODEL), lambda i: (0, 0, 0), **once),
                   pl.BlockSpec((D_MODEL, D_MODEL), lambda i: (0, 0), **once),
                   pl.BlockSpec((1, D_MODEL), lambda i: (0, 0))])
    if final:
        out_shape = [jax.ShapeDtypeStruct((s, D_MODEL), F32)]
    else:
        out_shape = [jax.ShapeDtypeStruct((s, D_MODEL), F32), jax.ShapeDtypeStruct((s, D_MODEL), BF16)]
    return pl.pallas_call(
        functools.partial(_merge_kernel, final=final),
        out_shape=out_shape,
        grid=(s // tm,),
        in_specs=in_specs,
        out_specs=[row(D_MODEL)] * len(out_shape),
        compiler_params=_params(("arbitrary",)),
        name="merge",
    )(*ys, p_arr, gates, x, wb, wo, g_next.reshape(1, D_MODEL))


@functools.lru_cache(maxsize=None)
def _rope_tables(seq_len):
    t = np.arange(seq_len)

    def angles(pos, dim):
        inv_freq = np.float32(ROPE_THETA) ** (-np.arange(dim // 2, dtype=np.float32) / np.float32(dim // 2))
        return pos.astype(np.float32)[:, None] * inv_freq.astype(np.float32)[None, :]

    a1 = angles(t, HEAD_DIM)
    cos1 = np.concatenate([np.cos(a1), np.cos(a1)], axis=1)
    sin1 = np.concatenate([-np.sin(a1), np.sin(a1)], axis=1)
    ar = angles(t // GRID_W, HEAD_DIM // 2)
    ac = angles(t % GRID_W, HEAD_DIM // 2)
    cos2 = np.concatenate([np.cos(ar), np.cos(ac), np.cos(ar), np.cos(ac)], axis=1)
    sin2 = np.concatenate([-np.sin(ar), -np.sin(ac), np.sin(ar), np.sin(ac)], axis=1)
    assert all(a.dtype == np.float32 for a in (cos1, sin1, cos2, sin2))
    return (cos1, sin1), (cos2, sin2)


def _axial_reorder(a):
    lead = a.shape[:-1]
    return a.reshape(*lead, -1, 2, 2, HEAD_DIM // 4).swapaxes(-3, -2).reshape(*lead, -1)


def _gather_cols(w, seg):
    cols = np.concatenate([np.arange(_OFF[n], _OFF[n] + _WID[n]) for n in seg])
    runs = np.split(cols, np.nonzero(np.diff(cols) != 1)[0] + 1)
    return jnp.concatenate([w[:, r[0]:r[-1] + 1] for r in runs], axis=1).astype(BF16)


def _col_scale(seg, scaled):
    return jnp.asarray(np.concatenate(
        [np.full((_WID[n],), scaled.get(n, 1.0), np.float32) for n in seg]).reshape(1, -1))


def _layer(x, h, w_in_all, layer, norm_gain, gate_b, a_sink, b_rpb, d_q_norm, d_k_norm, w_branch, w_out, g_next,
           tables, final):
    s = x.shape[0]
    (cos1, sin1), (cos2, sin2) = tables
    tm = 1024
    w_in = w_in_all[layer, :, :_OFF["gates"]]

    gains = _axial_reorder(jnp.concatenate([jnp.tile(d_q_norm, _WID["dq"] // HEAD_DIM),
                                            jnp.tile(d_k_norm, _WID["dk"] // HEAD_DIM)]))
    n_arr = _proj(x if h is None else h, _axial_reorder(_gather_cols(w_in, NORM_SEG)), "norm_rope",
                  [_col_scale(NORM_SEG, {"dq": SCALE}), gains.reshape(1, -1)],
                  [cos2, sin2], tm=tm, tn=NORM_W, name="proj_norm_rope", norm_gain=norm_gain if h is None else None)
    if h is None:
        n_arr, h = n_arr
    r_arr = _proj(h, _gather_cols(w_in, ROPE_SEG), "rope", [_col_scale(ROPE_SEG, {"aq": SCALE, "cq": SCALE})],
                  [cos1, sin1], tm=tm, tn=ROPE_W // 2, name="proj_rope")
    p_arr = _proj(h, _gather_cols(w_in, PLAIN_SEG), "scale", [_col_scale(PLAIN_SEG, {"bq": SCALE})], [],
                  tm=tm, tn=PLAIN_W // 3, name="proj_plain")
    gates = _proj_gates(h, w_in_all, layer, gate_b.reshape(1, -1))

    ya = _window_attention(r_arr, p_arr, a_sink, seq_len=s)
    yb = _nbr_attention(p_arr, _nbr_bias(b_rpb), seq_len=s)
    yc = _dilated_attention(r_arr, p_arr, seq_len=s)
    yd = _dense_attention(n_arr, p_arr, seq_len=s)
    return _merge((ya, yb, yc, yd), p_arr, gates, x, w_branch.astype(BF16), w_out.astype(BF16), g_next, final=final)


def kernel(x, norm_g, w_in, gate_b, a_sink, b_rpb, d_q_norm, d_k_norm, w_branch, w_out, final_norm_g):
    b, s, d = x.shape
    depth = w_in.shape[0]
    tables = tuple(tuple(jnp.asarray(a) for a in grp) for grp in _rope_tables(s))
    outs = []
    for bi in range(b):
        xb = x.reshape(s, d) if b == 1 else x[bi]
        h = None
        for layer in range(depth):
            final = layer == depth - 1
            g_next = final_norm_g if final else norm_g[layer + 1]
            res = _layer(xb, h, w_in, layer, norm_g[layer], gate_b[layer], a_sink[layer], b_rpb[layer], d_q_norm[layer],
                         d_k_norm[layer], w_branch[layer], w_out[layer], g_next, tables, final)
            if final:
                xb = res[0]
            else:
                xb, h = res
        outs.append(xb)
    return outs[0].reshape(1, s, d) if b == 1 else jnp.stack(outs, axis=0)
```

```python
import functools
import math

import numpy as np
import jax
import jax.numpy as jnp
from jax import lax
from jax.experimental import pallas as pl
from jax.experimental.pallas import tpu as pltpu

F32 = jnp.float32
BF16 = jnp.bfloat16

D_MODEL = 2048
HEAD_DIM = 128
BRANCH_WIDTH = 512
N_BRANCH = 4
A_HALF_WINDOW = 128
B_WIN_ROWS = 8
B_WIN_COLS = 16
C_PATTERNS = ((128, 1), (512, 4), (2048, 16))
GRID_W = 64
ROPE_THETA = 10000.0
NORM_EPS = 1e-6
LOG2E = math.log2(math.e)
SCALE = HEAD_DIM ** -0.5 * LOG2E
NEG_BIG = -1e30

VMEM_LIMIT = 56 * 1024 * 1024

_SPLITS = (512, 256, 256, 512, 512, 512, 512, 512, 1536, 512, 512, 512, 512, 256, 256, 512, 8192)
_NAMES = ("aq", "ak", "av", "az", "bq", "bk", "bv", "bz", "cq", "ck", "cv", "cz", "dq", "dk", "dv", "dz", "gates")
_OFF = dict(zip(_NAMES, np.concatenate([[0], np.cumsum(_SPLITS)[:-1]]).tolist()))
_WID = dict(zip(_NAMES, _SPLITS))

ROPE_SEG = ("aq", "ak", "cq", "ck")
NORM_SEG = ("dq", "dk")
PLAIN_SEG = ("az", "bz", "cz", "dz", "av", "dv", "bq", "bk", "bv", "cv")


def _seg_offsets(seg):
    offs, o = {}, 0
    for n in seg:
        offs[n] = o
        o += _WID[n]
    return offs, o


ROPE_OFF, ROPE_W = _seg_offsets(ROPE_SEG)
NORM_OFF, NORM_W = _seg_offsets(NORM_SEG)
PLAIN_OFF, PLAIN_W = _seg_offsets(PLAIN_SEG)


def _params(sem):
    return pltpu.CompilerParams(dimension_semantics=sem, vmem_limit_bytes=VMEM_LIMIT)


def _sigmoid(v):
    return 1.0 / (1.0 + jnp.exp(-v))


def _weighted_values(p, v):
    v1 = jnp.concatenate([v, jnp.ones_like(v)], axis=1)
    r = jnp.dot(p.astype(BF16), v1, preferred_element_type=F32)
    return r[:, :HEAD_DIM], r[:, HEAD_DIM:]


PROJ_RB = 256


def _rope_half(xh, cos, sin_signed):
    return xh * cos + pltpu.roll(xh, HEAD_DIM // 2, 1) * sin_signed


def _proj_kernel(*refs, mode, norm_input):
    if norm_input:
        g_in_ref, refs, h_out_ref = refs[0], refs[1:-1], refs[-1]
    h_ref, w_ref = refs[0], refs[1]
    o_ref = refs[-1]
    tm, tn = o_ref.shape
    for r in range(tm // PROJ_RB):
        rows = pl.ds(r * PROJ_RB, PROJ_RB)
        if norm_input:
            x = h_ref[rows, :]
            h = (x * lax.rsqrt(jnp.mean(x * x, axis=-1, keepdims=True) + NORM_EPS) * g_in_ref[...]).astype(BF16)
            h_out_ref[rows, :] = h
        else:
            h = h_ref[rows, :]
        acc = jnp.dot(h, w_ref[...], preferred_element_type=F32)
        if mode == "scale":
            o_ref[rows, :] = (acc * refs[2][...]).astype(o_ref.dtype)
        elif mode == "rope":
            cs_ref, cos_ref, sin_ref = refs[2:5]
            cos, sin = cos_ref[rows, :], sin_ref[rows, :]
            for c in range(tn // HEAD_DIM):
                sl = slice(c * HEAD_DIM, (c + 1) * HEAD_DIM)
                o_ref[rows, sl] = (_rope_half(acc[:, sl], cos, sin) * cs_ref[:, sl]).astype(o_ref.dtype)
        elif mode == "norm_rope":
            cs_ref, g_ref, cos_ref, sin_ref = refs[2:6]
            cos, sin = cos_ref[rows, :], sin_ref[rows, :]
            for c in range(tn // HEAD_DIM):
                sl = slice(c * HEAD_DIM, (c + 1) * HEAD_DIM)
                xh = acc[:, sl]
                y = xh * lax.rsqrt(jnp.mean(xh * xh, axis=-1, keepdims=True) + NORM_EPS) * g_ref[:, sl]
                o_ref[rows, sl] = (_rope_half(y, cos, sin) * cs_ref[:, sl]).astype(o_ref.dtype)
        else:
            raise ValueError(mode)


def _proj(h, w, mode, row_vecs, tables, tm, tn, name, norm_gain=None):
    s, d = h.shape
    n = w.shape[1]
    tm = min(tm, s)
    in_specs = [pl.BlockSpec((tm, d), lambda i, j: (i, 0)), pl.BlockSpec((d, tn), lambda i, j: (0, j))]
    in_specs += [pl.BlockSpec((1, tn), lambda i, j: (0, j)) for _ in row_vecs]
    in_specs += [pl.BlockSpec((tm, HEAD_DIM), lambda i, j: (i, 0)) for _ in tables]
    out_shape = [jax.ShapeDtypeStruct((s, n), BF16)]
    out_specs = [pl.BlockSpec((tm, tn), lambda i, j: (i, j))]
    args = [h, w, *row_vecs, *tables]
    if norm_gain is not None:
        assert n == tn
        in_specs = [pl.BlockSpec((1, d), lambda i, j: (0, 0))] + in_specs
        args = [norm_gain.reshape(1, d)] + args
        out_shape.append(jax.ShapeDtypeStruct((s, d), BF16))
        out_specs.append(pl.BlockSpec((tm, d), lambda i, j: (i, 0)))
    outs = pl.pallas_call(
        functools.partial(_proj_kernel, mode=mode, norm_input=norm_gain is not None),
        out_shape=out_shape,
        grid=(s // tm, n // tn),
        in_specs=in_specs,
        out_specs=out_specs,
        compiler_params=_params(("arbitrary", "arbitrary")),
        name=name,
    )(*args)
    return outs[0] if norm_gain is None else outs


def _gates_kernel(h_ref, w_ref, b_ref, o_ref, wb_ref):
    @pl.when(pl.program_id(1) == 0)
    def _():
        wb_ref[...] = w_ref[...].astype(BF16)

    for r in range(o_ref.shape[0] // PROJ_RB):
        rows = pl.ds(r * PROJ_RB, PROJ_RB)
        acc = jnp.dot(h_ref[rows, :], wb_ref[...], preferred_element_type=F32)
        o_ref[rows, :] = _sigmoid(acc + b_ref[...]).astype(o_ref.dtype)


def _proj_gates(h, w_in_all, layer, bias, tm=2048, tn=1024):
    s, d = h.shape
    tm = min(tm, s)
    n = _WID["gates"]
    first = _OFF["gates"] // tn
    return pl.pallas_call(
        _gates_kernel,
        out_shape=jax.ShapeDtypeStruct((s, n), BF16),
        grid=(n // tn, s // tm),
        in_specs=[pl.BlockSpec((tm, d), lambda j, i: (i, 0)),
                  pl.BlockSpec((None, d, tn), lambda j, i: (layer, 0, first + j)),
                  pl.BlockSpec((1, tn), lambda j, i: (0, j))],
        out_specs=pl.BlockSpec((tm, tn), lambda j, i: (i, j)),
        scratch_shapes=[pltpu.VMEM((d, tn), BF16)],
        compiler_params=_params(("arbitrary", "arbitrary")),
        name="proj_gates",
    )(h, w_in_all, bias)


WIN_BQ = 128


def _window_consts(sink_ref, kv_head):
    bq, groups = WIN_BQ, 2
    span, rows = bq + 2 * A_HALF_WINDOW, groups * bq
    row = lax.broadcasted_iota(jnp.int32, (rows, span), 0)
    col_minus_row = lax.broadcasted_iota(jnp.int32, (rows, span), 1) - jnp.where(row >= bq, row - bq, row)
    r1 = lax.broadcasted_iota(jnp.int32, (rows, 1), 0)
    sink = jnp.where(r1 >= bq, sink_ref[kv_head * groups + 1], sink_ref[kv_head * groups]) * LOG2E
    return col_minus_row, sink


def _window_block(consts, q_ref, k_ref, v_ref, o_ref, q0, qoff, seq_len):
    col_minus_row, sink = consts
    bq, hw = WIN_BQ, A_HALF_WINDOW
    span = bq + 2 * hw
    start = pl.multiple_of(jnp.clip(q0 - hw, 0, seq_len - span), bq)
    k = k_ref[pl.ds(start, span), :]
    v = v_ref[pl.ds(start, span), :]
    qb = q_ref[pl.ds(qoff, bq), :]
    q2 = jnp.concatenate([qb[:, :HEAD_DIM], qb[:, HEAD_DIM:]], axis=0)
    s = lax.dot_general(q2, k, (((1,), (1,)), ((), ())), preferred_element_type=F32)
    s = jnp.where(jnp.abs(col_minus_row + (start - q0)) <= hw, s, NEG_BIG)
    m = jnp.maximum(jnp.max(s, axis=1, keepdims=True), sink)
    p = jnp.exp2(s - m)
    pv, psum = _weighted_values(p, v)
    o = pv / (psum + jnp.exp2(sink - m))
    o_ref[pl.ds(qoff, bq), :HEAD_DIM] = o[:bq].astype(o_ref.dtype)
    o_ref[pl.ds(qoff, bq), HEAD_DIM:] = o[bq:].astype(o_ref.dtype)


def _window_kernel(sink_ref, q_ref, k_ref, v_ref, o_ref, *, seq_len, nsub):
    consts = _window_consts(sink_ref, pl.program_id(0))
    t0 = pl.program_id(1) * nsub * WIN_BQ

    def body(b, carry):
        qoff = pl.multiple_of(b * WIN_BQ, WIN_BQ)
        _window_block(consts, q_ref, k_ref, v_ref, o_ref, t0 + qoff, qoff, seq_len)
        return carry

    lax.fori_loop(0, nsub, body, 0, unroll=True)


def _window_attention(r_arr, p_arr, sink, *, seq_len, max_step=1024):
    nsub = min(max_step, seq_len) // WIN_BQ
    tq = nsub * WIN_BQ
    qc, kc, vc = ROPE_OFF["aq"] // (2 * HEAD_DIM), ROPE_OFF["ak"] // HEAD_DIM, PLAIN_OFF["av"] // HEAD_DIM
    return pl.pallas_call(
        functools.partial(_window_kernel, seq_len=seq_len, nsub=nsub),
        out_shape=jax.ShapeDtypeStruct((seq_len, BRANCH_WIDTH), BF16),
        grid=(2, seq_len // tq),
        in_specs=[pl.BlockSpec(memory_space=pltpu.SMEM),
                  pl.BlockSpec((tq, 2 * HEAD_DIM), lambda c, i: (i, qc + c)),
                  pl.BlockSpec((seq_len, HEAD_DIM), lambda c, i: (0, kc + c)),
                  pl.BlockSpec((seq_len, HEAD_DIM), lambda c, i: (0, vc + c))],
        out_specs=pl.BlockSpec((tq, 2 * HEAD_DIM), lambda c, i: (i, c)),
        compiler_params=_params(("arbitrary", "arbitrary")),
        name="window_attention",
    )(sink, r_arr, r_arr, p_arr)


DIL_BQ = 128
DIL_HALF = 64
DIL_HALO = DIL_HALF * max(d for _, d in C_PATTERNS)


DIL_BLK = 128


def _residue_major(dil):
    per = DIL_BLK // dil
    p = np.zeros((DIL_BLK, DIL_BLK), np.float32)
    for r in range(dil):
        for i in range(per):
            p[r * per + i, i * dil + r] = 1.0
    return p


def _to_residue_major(src_ref, dst_ref, perm, dil, first_blk, n_blk):
    per = DIL_BLK // dil
    for b in range(first_blk, first_blk + n_blk, 2):
        pair = jnp.concatenate([src_ref[b * DIL_BLK:(b + 1) * DIL_BLK, :],
                                src_ref[(b + 1) * DIL_BLK:(b + 2) * DIL_BLK, :]], axis=1)
        res = jnp.dot(perm, pair, preferred_element_type=F32)
        dst_ref[b * dil:(b + 1) * dil] = res[:, :HEAD_DIM].reshape(dil, per, HEAD_DIM)
        dst_ref[(b + 1) * dil:(b + 2) * dil] = res[:, HEAD_DIM:].reshape(dil, per, HEAD_DIM)


def _dilated_kernel(q0_ref, q1_ref, q2_ref, kp_ref, kc_ref, kn_ref, vp_ref, vc_ref, vn_ref, perm1_ref, perm2_ref,
                    o_ref, kw_ref, vw_ref, qd1_ref, kd1_ref, vd1_ref, qd2_ref, kd2_ref, vd2_ref, og_ref, lg_ref,
                    *, seq_len, tq):
    t0 = pl.program_id(1) * tq
    halo, bq, half = DIL_HALO, DIL_BQ, DIL_HALF
    span = bq + 2 * half
    for w_ref, prev, cur, nxt in ((kw_ref, kp_ref, kc_ref, kn_ref), (vw_ref, vp_ref, vc_ref, vn_ref)):
        w_ref[:halo] = prev[...]
        w_ref[halo:halo + tq] = cur[...]
        w_ref[halo + tq:] = nxt[...]
    residue_major = {}
    for (_, dil), q_ref, perm_ref, qd_ref, kd_ref, vd_ref in (
            (C_PATTERNS[1], q1_ref, perm1_ref, qd1_ref, kd1_ref, vd1_ref),
            (C_PATTERNS[2], q2_ref, perm2_ref, qd2_ref, kd2_ref, vd2_ref)):
        perm = perm_ref[...]
        first_blk = (halo - half * dil) // DIL_BLK
        n_blk = tq // DIL_BLK + dil
        _to_residue_major(q_ref, qd_ref, perm, dil, 0, tq // DIL_BLK)
        _to_residue_major(kw_ref, kd_ref, perm, dil, first_blk, n_blk)
        _to_residue_major(vw_ref, vd_ref, perm, dil, first_blk, n_blk)
        residue_major[dil] = (qd_ref, kd_ref, vd_ref)

    ii = lax.broadcasted_iota(jnp.int32, (bq, span), 0)
    jj = lax.broadcasted_iota(jnp.int32, (bq, span), 1)
    in_band = jnp.abs(jj - half - ii) <= half

    for g, (_, dil) in enumerate(C_PATTERNS):
        nb = tq // (bq * dil)

        def body(it, carry, g=g, dil=dil, nb=nb):
            res, sub = it // nb, it % nb
            q_row = res + dil * bq * sub
            w_row = halo + q_row - half * dil
            if dil == 1:
                q = q0_ref[pl.ds(pl.multiple_of(q_row, bq), bq), :]
                k = kw_ref[pl.ds(pl.multiple_of(w_row, half), span), :]
                v = vw_ref[pl.ds(pl.multiple_of(w_row, half), span), :]
            else:
                qd_ref, kd_ref, vd_ref = residue_major[dil]
                q_blk = sub * bq * dil // DIL_BLK
                k_blk = (halo - half * dil) // DIL_BLK + q_blk
                q = qd_ref[pl.ds(q_blk * dil + res, bq * dil // DIL_BLK, stride=dil)]
                k = kd_ref[pl.ds(k_blk * dil + res, span * dil // DIL_BLK, stride=dil)]
                v = vd_ref[pl.ds(k_blk * dil + res, span * dil // DIL_BLK, stride=dil)]
                q = q.reshape(bq, HEAD_DIM).astype(BF16)
                k = k.reshape(span, HEAD_DIM).astype(BF16)
                v = v.reshape(span, HEAD_DIM).astype(BF16)
            s = lax.dot_general(q, k, (((1,), (1,)), ((), ())), preferred_element_type=F32)
            ktok = t0 + q_row - half * dil + dil * jj
            s = jnp.where(in_band & (ktok >= 0) & (ktok < seq_len), s, NEG_BIG)
            m = jnp.max(s, axis=1, keepdims=True)
            p = jnp.exp2(s - m)
            pv, psum = _weighted_values(p, v)
            og_ref[g, pl.ds(q_row, bq, stride=dil), :] = pv / psum
            lg_ref[g, pl.ds(q_row, bq, stride=dil), :] = m + jnp.log2(psum)
            return carry

        lax.fori_loop(0, tq // bq, body, 0, unroll=True)

    l0, l1, l2 = lg_ref[0], lg_ref[1], lg_ref[2]
    mx = jnp.maximum(jnp.maximum(l0, l1), l2)
    e0, e1, e2 = jnp.exp2(l0 - mx), jnp.exp2(l1 - mx), jnp.exp2(l2 - mx)
    o_ref[...] = ((e0 * og_ref[0] + e1 * og_ref[1] + e2 * og_ref[2]) / (e0 + e1 + e2)).astype(o_ref.dtype)


def _dilated_attention(r_arr, p_arr, *, seq_len, tq=2048):
    tq = min(tq, seq_len)
    halo = DIL_HALO
    ratio, n_halo = tq // halo, seq_len // halo
    qc, kc, vc = ROPE_OFF["cq"] // HEAD_DIM, ROPE_OFF["ck"] // HEAD_DIM, PLAIN_OFF["cv"] // HEAD_DIM
    n_heads = BRANCH_WIDTH // HEAD_DIM

    def window_specs(col):
        return [pl.BlockSpec((halo, HEAD_DIM), lambda h, i: (jnp.maximum(i * ratio - 1, 0), col + h)),
                pl.BlockSpec((tq, HEAD_DIM), lambda h, i: (i, col + h)),
                pl.BlockSpec((halo, HEAD_DIM), lambda h, i: (jnp.minimum((i + 1) * ratio, n_halo - 1), col + h))]

    q_specs = [pl.BlockSpec((tq, HEAD_DIM), lambda h, i, g=g: (i, qc + n_heads * g + h)) for g in range(3)]
    perm_spec = pl.BlockSpec((DIL_BLK, DIL_BLK), lambda h, i: (0, 0))
    dilations = [d for _, d in C_PATTERNS[1:]]
    assert C_PATTERNS[0][1] == 1 and all(DIL_BLK % d == 0 and (DIL_BLK // d) % 8 == 0 for d in dilations)
    window = tq + 2 * halo
    residue_scratch = []
    for d in dilations:
        residue_scratch += [pltpu.VMEM((tq // DIL_BLK * d, DIL_BLK // d, HEAD_DIM), F32),
                            pltpu.VMEM((window // DIL_BLK * d, DIL_BLK // d, HEAD_DIM), F32),
                            pltpu.VMEM((window // DIL_BLK * d, DIL_BLK // d, HEAD_DIM), F32)]
    return pl.pallas_call(
        functools.partial(_dilated_kernel, seq_len=seq_len, tq=tq),
        out_shape=jax.ShapeDtypeStruct((seq_len, BRANCH_WIDTH), BF16),
        grid=(n_heads, seq_len // tq),
        in_specs=q_specs + window_specs(kc) + window_specs(vc) + [perm_spec] * len(dilations),
        out_specs=pl.BlockSpec((tq, HEAD_DIM), lambda h, i: (i, h)),
        scratch_shapes=[pltpu.VMEM((window, HEAD_DIM), BF16), pltpu.VMEM((window, HEAD_DIM), BF16)]
        + residue_scratch + [pltpu.VMEM((3, tq, HEAD_DIM), F32), pltpu.VMEM((3, tq, HEAD_DIM), F32)],
        compiler_params=_params(("arbitrary", "arbitrary")),
        name="dilated_attention",
    )(r_arr, r_arr, r_arr, r_arr, r_arr, r_arr, p_arr, p_arr, p_arr,
      *[jnp.asarray(_residue_major(d), BF16) for d in dilations])


NBR_ROWS = 4
NBR_UNION = 12
NBR_SUB = 16


def _nbr_bias_kernel(rpb_ref, o_ref, t_ref):
    h = pl.program_id(0)
    n_dr, n_dc = 2 * B_WIN_ROWS - 1, 2 * B_WIN_COLS - 1
    qc = lax.broadcasted_iota(jnp.int32, (GRID_W, GRID_W), 0)
    kc = lax.broadcasted_iota(jnp.int32, (GRID_W, GRID_W), 1)
    col_start = jnp.clip(qc - B_WIN_COLS // 2, 0, GRID_W - B_WIN_COLS)
    col_ok = (kc >= col_start) & (kc < col_start + B_WIN_COLS)
    dc = kc - qc + (B_WIN_COLS - 1)
    for dr in range(n_dr):
        t = jnp.full((GRID_W, GRID_W), NEG_BIG, F32)
        for d in range(n_dc):
            t = jnp.where(dc == d, rpb_ref[(h * n_dr + dr) * n_dc + d] * LOG2E, t)
        t_ref[dr] = jnp.where(col_ok, t, NEG_BIG)
    o_ref[...] = jnp.full(o_ref.shape, NEG_BIG, F32)
    half = B_WIN_ROWS // 2
    for case in range(3):
        for a in range(NBR_ROWS):
            win, off = ((0, a), (a, half), (NBR_UNION - B_WIN_ROWS, half + a))[case]
            for j in range(B_WIN_ROWS):
                o_ref[case, a * GRID_W:(a + 1) * GRID_W, (win + j) * GRID_W:(win + j + 1) * GRID_W] = (
                    t_ref[j - off + B_WIN_ROWS - 1])


def _nbr_bias(rpb):
    n_heads = rpb.shape[0]
    shape = (3, NBR_ROWS * GRID_W, NBR_UNION * GRID_W)
    return pl.pallas_call(
        _nbr_bias_kernel,
        out_shape=jax.ShapeDtypeStruct((n_heads,) + shape, F32),
        grid=(n_heads,),
        in_specs=[pl.BlockSpec(memory_space=pltpu.SMEM)],
        out_specs=pl.BlockSpec((None,) + shape, lambda h: (h, 0, 0, 0)),
        scratch_shapes=[pltpu.VMEM((2 * B_WIN_ROWS - 1, GRID_W, GRID_W), F32)],
        compiler_params=_params(("arbitrary",)),
        name="nbr_bias",
    )(rpb.reshape(-1))


def _nbr_kernel(q_ref, k_ref, v_ref, b_first_ref, b_mid_ref, b_last_ref, o_ref, *, n_rows):
    step = pl.program_id(1)
    nq, nk = NBR_ROWS * GRID_W, NBR_UNION * GRID_W
    for sb in range(NBR_SUB):
        bias_ref = b_first_ref if sb == 0 else (b_last_ref if sb == NBR_SUB - 1 else b_mid_ref)
        r0 = (step * NBR_SUB + sb) * NBR_ROWS
        u0 = jnp.clip(r0 - B_WIN_ROWS // 2, 0, n_rows - NBR_UNION)
        kstart = pl.multiple_of(u0 * GRID_W, GRID_W)
        k = k_ref[pl.ds(kstart, nk), :]
        v = v_ref[pl.ds(kstart, nk), :]
        q = q_ref[sb * nq:(sb + 1) * nq, :]
        s = lax.dot_general(q, k, (((1,), (1,)), ((), ())), preferred_element_type=F32) + bias_ref[...]
        m = jnp.max(s, axis=1, keepdims=True)
        p = jnp.exp2(s - m)
        pv, psum = _weighted_values(p, v)
        o_ref[sb * nq:(sb + 1) * nq, :] = (pv / psum).astype(o_ref.dtype)


def _nbr_attention(p_arr, bias, *, seq_len):
    n_rows = seq_len // GRID_W
    tq = NBR_SUB * NBR_ROWS * GRID_W
    n_steps = seq_len // tq
    qc, kc, vc = (PLAIN_OFF[n] // HEAD_DIM for n in ("bq", "bk", "bv"))
    tile = (None, None, NBR_ROWS * GRID_W, NBR_UNION * GRID_W)
    return pl.pallas_call(
        functools.partial(_nbr_kernel, n_rows=n_rows),
        out_shape=jax.ShapeDtypeStruct((seq_len, BRANCH_WIDTH), BF16),
        grid=(BRANCH_WIDTH // HEAD_DIM, n_steps),
        in_specs=[pl.BlockSpec((tq, HEAD_DIM), lambda h, i: (i, qc + h)),
                  pl.BlockSpec((seq_len, HEAD_DIM), lambda h, i: (0, kc + h)),
                  pl.BlockSpec((seq_len, HEAD_DIM), lambda h, i: (0, vc + h)),
                  pl.BlockSpec(tile, lambda h, i: (h, jnp.where(i == 0, 0, 1), 0, 0)),
                  pl.BlockSpec(tile, lambda h, i: (h, 1, 0, 0)),
                  pl.BlockSpec(tile, lambda h, i: (h, jnp.where(i == n_steps - 1, 2, 1), 0, 0))],
        out_specs=pl.BlockSpec((tq, HEAD_DIM), lambda h, i: (i, h)),
        compiler_params=_params(("arbitrary", "arbitrary")),
        name="nbr_attention",
    )(p_arr, p_arr, p_arr, bias, bias, bias)


DENSE_RB = 256


def _dense_kernel(q_ref, k_ref, v_ref, o_ref, q2_ref, v2_ref, m_ref, acc_ref, s0_ref, *, seq_len, tq, tk):
    @pl.when(pl.program_id(1) == 0)
    def _():
        v2_ref[:, :HEAD_DIM] = v_ref[...]
        v2_ref[:, HEAD_DIM:] = jnp.ones((seq_len, HEAD_DIM), BF16)

    q2_ref[:tq] = q_ref[:, :HEAD_DIM]
    q2_ref[tq:] = q_ref[:, HEAD_DIM:]
    m_ref[...] = jnp.full(m_ref.shape, NEG_BIG, F32)
    acc_ref[...] = jnp.zeros(acc_ref.shape, F32)
    n_chunks = seq_len // tk

    def logits(r, k):
        return lax.dot_general(q2_ref[pl.ds(r * DENSE_RB, DENSE_RB), :], k, (((1,), (1,)), ((), ())),
                               preferred_element_type=F32)

    s0_ref[...] = logits(0, k_ref[pl.ds(0, tk), :])

    def body(j, carry):
        koff = pl.multiple_of(j * tk, tk)
        k = k_ref[pl.ds(koff, tk), :]
        v2 = v2_ref[pl.ds(koff, tk), :]
        for r in range((2 * tq) // DENSE_RB):
            rows = pl.ds(r * DENSE_RB, DENSE_RB)
            s = s0_ref[...] if r == 0 else logits(r, k)
            m_prev = m_ref[rows, :]
            m_new = jnp.maximum(m_prev, jnp.max(s, axis=1, keepdims=True))
            alpha = jnp.exp2(m_prev - m_new)
            p = jnp.exp2(s - m_new).astype(BF16)
            acc_ref[rows, :] = acc_ref[rows, :] * alpha + jnp.dot(p, v2, preferred_element_type=F32)
            m_ref[rows, :] = m_new
        next_off = pl.multiple_of(jnp.minimum(j + 1, n_chunks - 1) * tk, tk)
        s0_ref[...] = logits(0, k_ref[pl.ds(next_off, tk), :])
        return carry

    lax.fori_loop(0, n_chunks, body, 0)
    acc = acc_ref[...]
    o = acc[:, :HEAD_DIM] / acc[:, HEAD_DIM:]
    o_ref[:, :HEAD_DIM] = o[:tq].astype(o_ref.dtype)
    o_ref[:, HEAD_DIM:] = o[tq:].astype(o_ref.dtype)


def _dense_attention(n_arr, p_arr, *, seq_len, tq=2048, tk=2048):
    tq, tk = min(tq, seq_len), min(tk, seq_len)
    kc = NORM_OFF["dk"] // HEAD_DIM
    vc = PLAIN_OFF["dv"] // HEAD_DIM
    return pl.pallas_call(
        functools.partial(_dense_kernel, seq_len=seq_len, tq=tq, tk=tk),
        out_shape=jax.ShapeDtypeStruct((seq_len, BRANCH_WIDTH), BF16),
        grid=(2, seq_len // tq),
        in_specs=[pl.BlockSpec((tq, 2 * HEAD_DIM), lambda g, i: (i, g)),
                  pl.BlockSpec((seq_len, HEAD_DIM), lambda g, i: (0, kc + g), pipeline_mode=pl.Buffered(1)),
                  pl.BlockSpec((seq_len, HEAD_DIM), lambda g, i: (0, vc + g), pipeline_mode=pl.Buffered(1))],
        out_specs=pl.BlockSpec((tq, 2 * HEAD_DIM), lambda g, i: (i, g)),
        scratch_shapes=[pltpu.VMEM((2 * tq, HEAD_DIM), BF16), pltpu.VMEM((seq_len, 2 * HEAD_DIM), BF16),
                        pltpu.VMEM((2 * tq, 1), F32), pltpu.VMEM((2 * tq, 2 * HEAD_DIM), F32),
                        pltpu.VMEM((DENSE_RB, tk), F32)],
        compiler_params=_params(("arbitrary", "arbitrary")),
        name="dense_attention",
    )(n_arr, n_arr, p_arr)


def _merge_kernel(ya_ref, yb_ref, yc_ref, yd_ref, z_ref, gate_ref, x_ref, wb_ref, wo_ref, g_ref, *out_refs, final):
    merged = None
    for n, y_ref in enumerate((ya_ref, yb_ref, yc_ref, yd_ref)):
        z = z_ref[:, n * BRANCH_WIDTH:(n + 1) * BRANCH_WIDTH].astype(F32)
        t = (y_ref[...].astype(F32) * (z * _sigmoid(z))).astype(BF16)
        br = jnp.dot(t, wb_ref[n], preferred_element_type=F32)
        term = gate_ref[:, n * D_MODEL:(n + 1) * D_MODEL].astype(F32) * br
        merged = term if merged is None else merged + term
    x_new = x_ref[...] + jnp.dot(merged.astype(BF16), wo_ref[...], preferred_element_type=F32)
    normed = x_new * lax.rsqrt(jnp.mean(x_new * x_new, axis=-1, keepdims=True) + NORM_EPS) * g_ref[...]
    if final:
        out_refs[0][...] = normed
    else:
        out_refs[0][...] = x_new
        out_refs[1][...] = normed.astype(BF16)


def _merge(ys, p_arr, gates, x, wb, wo, g_next, *, final, tm=256):
    s = x.shape[0]
    row = lambda w: pl.BlockSpec((tm, w), lambda i: (i, 0))
    once = dict(pipeline_mode=pl.Buffered(1))
    in_specs = ([row(BRANCH_WIDTH)] * N_BRANCH
                + [row(N_BRANCH * BRANCH_WIDTH), row(N_BRANCH * D_MODEL), row(D_MODEL),
                   pl.BlockSpec((N_BRANCH, BRANCH_WIDTH, D_MODEL), lambda i: (0, 0, 0), **once),
                   pl.BlockSpec((D_MODEL, D_MODEL), lambda i: (0, 0), **once),
                   pl.BlockSpec((1, D_MODEL), lambda i: (0, 0))])
    if final:
        out_shape = [jax.ShapeDtypeStruct((s, D_MODEL), F32)]
    else:
        out_shape = [jax.ShapeDtypeStruct((s, D_MODEL), F32), jax.ShapeDtypeStruct((s, D_MODEL), BF16)]
    return pl.pallas_call(
        functools.partial(_merge_kernel, final=final),
        out_shape=out_shape,
        grid=(s // tm,),
        in_specs=in_specs,
        out_specs=[row(D_MODEL)] * len(out_shape),
        compiler_params=_params(("arbitrary",)),
        name="merge",
    )(*ys, p_arr, gates, x, wb, wo, g_next.reshape(1, D_MODEL))


@functools.lru_cache(maxsize=None)
def _rope_tables(seq_len):
    t = np.arange(seq_len)

    def angles(pos, dim):
        inv_freq = np.float32(ROPE_THETA) ** (-np.arange(dim // 2, dtype=np.float32) / np.float32(dim // 2))
        return pos.astype(np.float32)[:, None] * inv_freq.astype(np.float32)[None, :]

    a1 = angles(t, HEAD_DIM)
    cos1 = np.concatenate([np.cos(a1), np.cos(a1)], axis=1)
    sin1 = np.concatenate([-np.sin(a1), np.sin(a1)], axis=1)
    ar = angles(t // GRID_W, HEAD_DIM // 2)
    ac = angles(t % GRID_W, HEAD_DIM // 2)
    cos2 = np.concatenate([np.cos(ar), np.cos(ac), np.cos(ar), np.cos(ac)], axis=1)
    sin2 = np.concatenate([-np.sin(ar), -np.sin(ac), np.sin(ar), np.sin(ac)], axis=1)
    assert all(a.dtype == np.float32 for a in (cos1, sin1, cos2, sin2))
    return (cos1, sin1), (cos2, sin2)


def _axial_reorder(a):
    lead = a.shape[:-1]
    return a.reshape(*lead, -1, 2, 2, HEAD_DIM // 4).swapaxes(-3, -2).reshape(*lead, -1)


def _gather_cols(w, seg):
    cols = np.concatenate([np.arange(_OFF[n], _OFF[n] + _WID[n]) for n in seg])
    runs = np.split(cols, np.nonzero(np.diff(cols) != 1)[0] + 1)
    return jnp.concatenate([w[:, r[0]:r[-1] + 1] for r in runs], axis=1).astype(BF16)


def _col_scale(seg, scaled):
    return jnp.asarray(np.concatenate(
        [np.full((_WID[n],), scaled.get(n, 1.0), np.float32) for n in seg]).reshape(1, -1))


def _layer(x, h, w_in_all, layer, norm_gain, gate_b, a_sink, b_rpb, d_q_norm, d_k_norm, w_branch, w_out, g_next,
           tables, final):
    s = x.shape[0]
    (cos1, sin1), (cos2, sin2) = tables
    tm = 1024
    w_in = w_in_all[layer, :, :_OFF["gates"]]

    gains = _axial_reorder(jnp.concatenate([jnp.tile(d_q_norm, _WID["dq"] // HEAD_DIM),
                                            jnp.tile(d_k_norm, _WID["dk"] // HEAD_DIM)]))
    n_arr = _proj(x if h is None else h, _axial_reorder(_gather_cols(w_in, NORM_SEG)), "norm_rope",
                  [_col_scale(NORM_SEG, {"dq": SCALE}), gains.reshape(1, -1)],
                  [cos2, sin2], tm=tm, tn=NORM_W, name="proj_norm_rope", norm_gain=norm_gain if h is None else None)
    if h is None:
        n_arr, h = n_arr
    r_arr = _proj(h, _gather_cols(w_in, ROPE_SEG), "rope", [_col_scale(ROPE_SEG, {"aq": SCALE, "cq": SCALE})],
                  [cos1, sin1], tm=2 * tm, tn=ROPE_W // 2, name="proj_rope")
    p_arr = _proj(h, _gather_cols(w_in, PLAIN_SEG), "scale", [_col_scale(PLAIN_SEG, {"bq": SCALE})], [],
                  tm=2 * tm, tn=PLAIN_W // 3, name="proj_plain")
    gates = _proj_gates(h, w_in_all, layer, gate_b.reshape(1, -1))

    ya = _window_attention(r_arr, p_arr, a_sink, seq_len=s)
    yb = _nbr_attention(p_arr, _nbr_bias(b_rpb), seq_len=s)
    yc = _dilated_attention(r_arr, p_arr, seq_len=s)
    yd = _dense_attention(n_arr, p_arr, seq_len=s)
    return _merge((ya, yb, yc, yd), p_arr, gates, x, w_branch.astype(BF16), w_out.astype(BF16), g_next, final=final)


def kernel(x, norm_g, w_in, gate_b, a_sink, b_rpb, d_q_norm, d_k_norm, w_branch, w_out, final_norm_g):
    b, s, d = x.shape
    depth = w_in.shape[0]
    tables = tuple(tuple(jnp.asarray(a) for a in grp) for grp in _rope_tables(s))
    outs = []
    for bi in range(b):
        xb = x.reshape(s, d) if b == 1 else x[bi]
        h = None
        for layer in range(depth):
            final = layer == depth - 1
            g_next = final_norm_g if final else norm_g[layer + 1]
            res = _layer(xb, h, w_in, layer, norm_g[layer], gate_b[layer], a_sink[layer], b_rpb[layer], d_q_norm[layer],
                         d_k_norm[layer], w_branch[layer], w_out[layer], g_next, tables, final)
            if final:
                xb = res[0]
            else:
                xb, h = res
        outs.append(xb)
    return outs[0].reshape(1, s, d) if b == 1 else jnp.stack(outs, axis=0)
```

```python
import functools
import math

import numpy as np
import jax
import jax.numpy as jnp
from jax import lax
from jax.experimental import pallas as pl
from jax.experimental.pallas import tpu as pltpu

F32 = jnp.float32
BF16 = jnp.bfloat16

D_MODEL = 2048
HEAD_DIM = 128
BRANCH_WIDTH = 512
N_BRANCH = 4
A_HALF_WINDOW = 128
B_WIN_ROWS = 8
B_WIN_COLS = 16
C_PATTERNS = ((128, 1), (512, 4), (2048, 16))
GRID_W = 64
ROPE_THETA = 10000.0
NORM_EPS = 1e-6
LOG2E = math.log2(math.e)
SCALE = HEAD_DIM ** -0.5 * LOG2E
NEG_BIG = -1e30

VMEM_LIMIT = 56 * 1024 * 1024

_SPLITS = (512, 256, 256, 512, 512, 512, 512, 512, 1536, 512, 512, 512, 512, 256, 256, 512, 8192)
_NAMES = ("aq", "ak", "av", "az", "bq", "bk", "bv", "bz", "cq", "ck", "cv", "cz", "dq", "dk", "dv", "dz", "gates")
_OFF = dict(zip(_NAMES, np.concatenate([[0], np.cumsum(_SPLITS)[:-1]]).tolist()))
_WID = dict(zip(_NAMES, _SPLITS))

ROPE_SEG = ("aq", "ak", "cq", "ck")
NORM_SEG = ("dq", "dk")
PLAIN_SEG = ("az", "bz", "cz", "dz", "av", "dv", "bq", "bk", "bv", "cv")


def _seg_offsets(seg):
    offs, o = {}, 0
    for n in seg:
        offs[n] = o
        o += _WID[n]
    return offs, o


ROPE_OFF, ROPE_W = _seg_offsets(ROPE_SEG)
NORM_OFF, NORM_W = _seg_offsets(NORM_SEG)
PLAIN_OFF, PLAIN_W = _seg_offsets(PLAIN_SEG)


def _params(sem):
    return pltpu.CompilerParams(dimension_semantics=sem, vmem_limit_bytes=VMEM_LIMIT)


def _sigmoid(v):
    return 1.0 / (1.0 + jnp.exp(-v))


def _weighted_values(p, v):
    v1 = jnp.concatenate([v, jnp.ones_like(v)], axis=1)
    r = jnp.dot(p.astype(BF16), v1, preferred_element_type=F32)
    return r[:, :HEAD_DIM], r[:, HEAD_DIM:]


PROJ_RB = 256


def _rope_half(xh, cos, sin_signed):
    return xh * cos + pltpu.roll(xh, HEAD_DIM // 2, 1) * sin_signed


def _proj_kernel(*refs, mode, norm_input):
    if norm_input:
        g_in_ref, refs, h_out_ref = refs[0], refs[1:-1], refs[-1]
    h_ref, w_ref = refs[0], refs[1]
    o_ref = refs[-1]
    tm, tn = o_ref.shape
    for r in range(tm // PROJ_RB):
        rows = pl.ds(r * PROJ_RB, PROJ_RB)
        if norm_input:
            x = h_ref[rows, :]
            h = (x * lax.rsqrt(jnp.mean(x * x, axis=-1, keepdims=True) + NORM_EPS) * g_in_ref[...]).astype(BF16)
            h_out_ref[rows, :] = h
        else:
            h = h_ref[rows, :]
        acc = jnp.dot(h, w_ref[...], preferred_element_type=F32)
        if mode == "scale":
            o_ref[rows, :] = (acc * refs[2][...]).astype(o_ref.dtype)
        elif mode == "rope":
            cs_ref, cos_ref, sin_ref = refs[2:5]
            cos, sin = cos_ref[rows, :], sin_ref[rows, :]
            for c in range(tn // HEAD_DIM):
                sl = slice(c * HEAD_DIM, (c + 1) * HEAD_DIM)
                o_ref[rows, sl] = (_rope_half(acc[:, sl], cos, sin) * cs_ref[:, sl]).astype(o_ref.dtype)
        elif mode == "norm_rope":
            cs_ref, g_ref, cos_ref, sin_ref = refs[2:6]
            cos, sin = cos_ref[rows, :], sin_ref[rows, :]
            for c in range(tn // HEAD_DIM):
                sl = slice(c * HEAD_DIM, (c + 1) * HEAD_DIM)
                xh = acc[:, sl]
                y = xh * lax.rsqrt(jnp.mean(xh * xh, axis=-1, keepdims=True) + NORM_EPS) * g_ref[:, sl]
                o_ref[rows, sl] = (_rope_half(y, cos, sin) * cs_ref[:, sl]).astype(o_ref.dtype)
        else:
            raise ValueError(mode)


def _proj(h, w, mode, row_vecs, tables, tm, tn, name, norm_gain=None):
    s, d = h.shape
    n = w.shape[1]
    tm = min(tm, s)
    in_specs = [pl.BlockSpec((tm, d), lambda i, j: (i, 0)), pl.BlockSpec((d, tn), lambda i, j: (0, j))]
    in_specs += [pl.BlockSpec((1, tn), lambda i, j: (0, j)) for _ in row_vecs]
    in_specs += [pl.BlockSpec((tm, HEAD_DIM), lambda i, j: (i, 0)) for _ in tables]
    out_shape = [jax.ShapeDtypeStruct((s, n), BF16)]
    out_specs = [pl.BlockSpec((tm, tn), lambda i, j: (i, j))]
    args = [h, w, *row_vecs, *tables]
    if norm_gain is not None:
        assert n == tn
        in_specs = [pl.BlockSpec((1, d), lambda i, j: (0, 0))] + in_specs
        args = [norm_gain.reshape(1, d)] + args
        out_shape.append(jax.ShapeDtypeStruct((s, d), BF16))
        out_specs.append(pl.BlockSpec((tm, d), lambda i, j: (i, 0)))
    outs = pl.pallas_call(
        functools.partial(_proj_kernel, mode=mode, norm_input=norm_gain is not None),
        out_shape=out_shape,
        grid=(s // tm, n // tn),
        in_specs=in_specs,
        out_specs=out_specs,
        compiler_params=_params(("arbitrary", "arbitrary")),
        name=name,
    )(*args)
    return outs[0] if norm_gain is None else outs


def _gates_kernel(h_ref, w_ref, b_ref, o_ref, wb_ref):
    @pl.when(pl.program_id(1) == 0)
    def _():
        wb_ref[...] = w_ref[...].astype(BF16)

    for r in range(o_ref.shape[0] // PROJ_RB):
        rows = pl.ds(r * PROJ_RB, PROJ_RB)
        acc = jnp.dot(h_ref[rows, :], wb_ref[...], preferred_element_type=F32)
        o_ref[rows, :] = _sigmoid(acc + b_ref[...]).astype(o_ref.dtype)


def _proj_gates(h, w_in_all, layer, bias, tm=2048, tn=1024):
    s, d = h.shape
    tm = min(tm, s)
    n = _WID["gates"]
    first = _OFF["gates"] // tn
    return pl.pallas_call(
        _gates_kernel,
        out_shape=jax.ShapeDtypeStruct((s, n), BF16),
        grid=(n // tn, s // tm),
        in_specs=[pl.BlockSpec((tm, d), lambda j, i: (i, 0)),
                  pl.BlockSpec((None, d, tn), lambda j, i: (layer, 0, first + j)),
                  pl.BlockSpec((1, tn), lambda j, i: (0, j))],
        out_specs=pl.BlockSpec((tm, tn), lambda j, i: (i, j)),
        scratch_shapes=[pltpu.VMEM((d, tn), BF16)],
        compiler_params=_params(("arbitrary", "arbitrary")),
        name="proj_gates",
    )(h, w_in_all, bias)


WIN_BQ = 128


def _window_consts(sink_ref, kv_head):
    bq, groups = WIN_BQ, 2
    span, rows = bq + 2 * A_HALF_WINDOW, groups * bq
    row = lax.broadcasted_iota(jnp.int32, (rows, span), 0)
    col_minus_row = lax.broadcasted_iota(jnp.int32, (rows, span), 1) - jnp.where(row >= bq, row - bq, row)
    r1 = lax.broadcasted_iota(jnp.int32, (rows, 1), 0)
    sink = jnp.where(r1 >= bq, sink_ref[kv_head * groups + 1], sink_ref[kv_head * groups]) * LOG2E
    return col_minus_row, sink


def _window_block(consts, q_ref, k_ref, v_ref, o_ref, q0, qoff, seq_len):
    col_minus_row, sink = consts
    bq, hw = WIN_BQ, A_HALF_WINDOW
    span = bq + 2 * hw
    start = pl.multiple_of(jnp.clip(q0 - hw, 0, seq_len - span), bq)
    k = k_ref[pl.ds(start, span), :]
    v = v_ref[pl.ds(start, span), :]
    qb = q_ref[pl.ds(qoff, bq), :]
    q2 = jnp.concatenate([qb[:, :HEAD_DIM], qb[:, HEAD_DIM:]], axis=0)
    s = lax.dot_general(q2, k, (((1,), (1,)), ((), ())), preferred_element_type=F32)
    s = jnp.where(jnp.abs(col_minus_row + (start - q0)) <= hw, s, NEG_BIG)
    m = jnp.maximum(jnp.max(s, axis=1, keepdims=True), sink)
    p = jnp.exp2(s - m)
    pv, psum = _weighted_values(p, v)
    o = pv / (psum + jnp.exp2(sink - m))
    o_ref[pl.ds(qoff, bq), :HEAD_DIM] = o[:bq].astype(o_ref.dtype)
    o_ref[pl.ds(qoff, bq), HEAD_DIM:] = o[bq:].astype(o_ref.dtype)


def _window_kernel(sink_ref, q_ref, k_ref, v_ref, o_ref, *, seq_len, nsub):
    consts = _window_consts(sink_ref, pl.program_id(0))
    t0 = pl.program_id(1) * nsub * WIN_BQ

    def body(b, carry):
        qoff = pl.multiple_of(b * WIN_BQ, WIN_BQ)
        _window_block(consts, q_ref, k_ref, v_ref, o_ref, t0 + qoff, qoff, seq_len)
        return carry

    lax.fori_loop(0, nsub, body, 0, unroll=True)


def _window_attention(r_arr, p_arr, sink, *, seq_len, max_step=2048):
    nsub = min(max_step, seq_len) // WIN_BQ
    tq = nsub * WIN_BQ
    qc, kc, vc = ROPE_OFF["aq"] // (2 * HEAD_DIM), ROPE_OFF["ak"] // HEAD_DIM, PLAIN_OFF["av"] // HEAD_DIM
    return pl.pallas_call(
        functools.partial(_window_kernel, seq_len=seq_len, nsub=nsub),
        out_shape=jax.ShapeDtypeStruct((seq_len, BRANCH_WIDTH), BF16),
        grid=(2, seq_len // tq),
        in_specs=[pl.BlockSpec(memory_space=pltpu.SMEM),
                  pl.BlockSpec((tq, 2 * HEAD_DIM), lambda c, i: (i, qc + c)),
                  pl.BlockSpec((seq_len, HEAD_DIM), lambda c, i: (0, kc + c)),
                  pl.BlockSpec((seq_len, HEAD_DIM), lambda c, i: (0, vc + c))],
        out_specs=pl.BlockSpec((tq, 2 * HEAD_DIM), lambda c, i: (i, c)),
        compiler_params=_params(("arbitrary", "arbitrary")),
        name="window_attention",
    )(sink, r_arr, r_arr, p_arr)


DIL_BQ = 128
DIL_HALF = 64
DIL_HALO = DIL_HALF * max(d for _, d in C_PATTERNS)


DIL_BLK = 128


def _residue_major(dil):
    per = DIL_BLK // dil
    p = np.zeros((DIL_BLK, DIL_BLK), np.float32)
    for r in range(dil):
        for i in range(per):
            p[r * per + i, i * dil + r] = 1.0
    return p


def _to_residue_major(src_ref, dst_ref, perm, dil, first_blk, n_blk):
    per = DIL_BLK // dil
    for b in range(first_blk, first_blk + n_blk, 2):
        pair = jnp.concatenate([src_ref[b * DIL_BLK:(b + 1) * DIL_BLK, :],
                                src_ref[(b + 1) * DIL_BLK:(b + 2) * DIL_BLK, :]], axis=1)
        res = jnp.dot(perm, pair, preferred_element_type=F32)
        dst_ref[b * dil:(b + 1) * dil] = res[:, :HEAD_DIM].reshape(dil, per, HEAD_DIM)
        dst_ref[(b + 1) * dil:(b + 2) * dil] = res[:, HEAD_DIM:].reshape(dil, per, HEAD_DIM)


def _dilated_kernel(q0_ref, q1_ref, q2_ref, kp_ref, kc_ref, kn_ref, vp_ref, vc_ref, vn_ref, perm1_ref, perm2_ref,
                    o_ref, kw_ref, vw_ref, qd1_ref, kd1_ref, vd1_ref, qd2_ref, kd2_ref, vd2_ref, og_ref, lg_ref,
                    *, seq_len, tq):
    t0 = pl.program_id(1) * tq
    halo, bq, half = DIL_HALO, DIL_BQ, DIL_HALF
    span = bq + 2 * half
    for w_ref, prev, cur, nxt in ((kw_ref, kp_ref, kc_ref, kn_ref), (vw_ref, vp_ref, vc_ref, vn_ref)):
        w_ref[:halo] = prev[...]
        w_ref[halo:halo + tq] = cur[...]
        w_ref[halo + tq:] = nxt[...]
    residue_major = {}
    for (_, dil), q_ref, perm_ref, qd_ref, kd_ref, vd_ref in (
            (C_PATTERNS[1], q1_ref, perm1_ref, qd1_ref, kd1_ref, vd1_ref),
            (C_PATTERNS[2], q2_ref, perm2_ref, qd2_ref, kd2_ref, vd2_ref)):
        perm = perm_ref[...]
        first_blk = (halo - half * dil) // DIL_BLK
        n_blk = tq // DIL_BLK + dil
        _to_residue_major(q_ref, qd_ref, perm, dil, 0, tq // DIL_BLK)
        _to_residue_major(kw_ref, kd_ref, perm, dil, first_blk, n_blk)
        _to_residue_major(vw_ref, vd_ref, perm, dil, first_blk, n_blk)
        residue_major[dil] = (qd_ref, kd_ref, vd_ref)

    ii = lax.broadcasted_iota(jnp.int32, (bq, span), 0)
    jj = lax.broadcasted_iota(jnp.int32, (bq, span), 1)
    in_band = jnp.abs(jj - half - ii) <= half

    for g, (_, dil) in enumerate(C_PATTERNS):
        nb = tq // (bq * dil)

        def body(it, carry, g=g, dil=dil, nb=nb):
            res, sub = it // nb, it % nb
            q_row = res + dil * bq * sub
            w_row = halo + q_row - half * dil
            if dil == 1:
                q = q0_ref[pl.ds(pl.multiple_of(q_row, bq), bq), :]
                k = kw_ref[pl.ds(pl.multiple_of(w_row, half), span), :]
                v = vw_ref[pl.ds(pl.multiple_of(w_row, half), span), :]
            else:
                qd_ref, kd_ref, vd_ref = residue_major[dil]
                q_blk = sub * bq * dil // DIL_BLK
                k_blk = (halo - half * dil) // DIL_BLK + q_blk
                q = qd_ref[pl.ds(q_blk * dil + res, bq * dil // DIL_BLK, stride=dil)]
                k = kd_ref[pl.ds(k_blk * dil + res, span * dil // DIL_BLK, stride=dil)]
                v = vd_ref[pl.ds(k_blk * dil + res, span * dil // DIL_BLK, stride=dil)]
                q = q.reshape(bq, HEAD_DIM).astype(BF16)
                k = k.reshape(span, HEAD_DIM).astype(BF16)
                v = v.reshape(span, HEAD_DIM).astype(BF16)
            s = lax.dot_general(q, k, (((1,), (1,)), ((), ())), preferred_element_type=F32)
            ktok = t0 + q_row - half * dil + dil * jj
            s = jnp.where(in_band & (ktok >= 0) & (ktok < seq_len), s, NEG_BIG)
            m = jnp.max(s, axis=1, keepdims=True)
            p = jnp.exp2(s - m)
            pv, psum = _weighted_values(p, v)
            og_ref[g, pl.ds(q_row, bq, stride=dil), :] = pv / psum
            lg_ref[g, pl.ds(q_row, bq, stride=dil), :] = m + jnp.log2(psum)
            return carry

        lax.fori_loop(0, tq // bq, body, 0, unroll=True)

    l0, l1, l2 = lg_ref[0], lg_ref[1], lg_ref[2]
    mx = jnp.maximum(jnp.maximum(l0, l1), l2)
    e0, e1, e2 = jnp.exp2(l0 - mx), jnp.exp2(l1 - mx), jnp.exp2(l2 - mx)
    o_ref[...] = ((e0 * og_ref[0] + e1 * og_ref[1] + e2 * og_ref[2]) / (e0 + e1 + e2)).astype(o_ref.dtype)


def _dilated_attention(r_arr, p_arr, *, seq_len, tq=2048):
    tq = min(tq, seq_len)
    halo = DIL_HALO
    ratio, n_halo = tq // halo, seq_len // halo
    qc, kc, vc = ROPE_OFF["cq"] // HEAD_DIM, ROPE_OFF["ck"] // HEAD_DIM, PLAIN_OFF["cv"] // HEAD_DIM
    n_heads = BRANCH_WIDTH // HEAD_DIM

    def window_specs(col):
        return [pl.BlockSpec((halo, HEAD_DIM), lambda h, i: (jnp.maximum(i * ratio - 1, 0), col + h)),
                pl.BlockSpec((tq, HEAD_DIM), lambda h, i: (i, col + h)),
                pl.BlockSpec((halo, HEAD_DIM), lambda h, i: (jnp.minimum((i + 1) * ratio, n_halo - 1), col + h))]

    q_specs = [pl.BlockSpec((tq, HEAD_DIM), lambda h, i, g=g: (i, qc + n_heads * g + h)) for g in range(3)]
    perm_spec = pl.BlockSpec((DIL_BLK, DIL_BLK), lambda h, i: (0, 0))
    dilations = [d for _, d in C_PATTERNS[1:]]
    assert C_PATTERNS[0][1] == 1 and all(DIL_BLK % d == 0 and (DIL_BLK // d) % 8 == 0 for d in dilations)
    window = tq + 2 * halo
    residue_scratch = []
    for d in dilations:
        residue_scratch += [pltpu.VMEM((tq // DIL_BLK * d, DIL_BLK // d, HEAD_DIM), F32),
                            pltpu.VMEM((window // DIL_BLK * d, DIL_BLK // d, HEAD_DIM), F32),
                            pltpu.VMEM((window // DIL_BLK * d, DIL_BLK // d, HEAD_DIM), F32)]
    return pl.pallas_call(
        functools.partial(_dilated_kernel, seq_len=seq_len, tq=tq),
        out_shape=jax.ShapeDtypeStruct((seq_len, BRANCH_WIDTH), BF16),
        grid=(n_heads, seq_len // tq),
        in_specs=q_specs + window_specs(kc) + window_specs(vc) + [perm_spec] * len(dilations),
        out_specs=pl.BlockSpec((tq, HEAD_DIM), lambda h, i: (i, h)),
        scratch_shapes=[pltpu.VMEM((window, HEAD_DIM), BF16), pltpu.VMEM((window, HEAD_DIM), BF16)]
        + residue_scratch + [pltpu.VMEM((3, tq, HEAD_DIM), F32), pltpu.VMEM((3, tq, HEAD_DIM), F32)],
        compiler_params=_params(("arbitrary", "arbitrary")),
        name="dilated_attention",
    )(r_arr, r_arr, r_arr, r_arr, r_arr, r_arr, p_arr, p_arr, p_arr,
      *[jnp.asarray(_residue_major(d), BF16) for d in dilations])


NBR_ROWS = 4
NBR_UNION = 12
NBR_SUB = 32


def _nbr_bias_kernel(rpb_ref, o_ref, t_ref):
    h = pl.program_id(0)
    n_dr, n_dc = 2 * B_WIN_ROWS - 1, 2 * B_WIN_COLS - 1
    qc = lax.broadcasted_iota(jnp.int32, (GRID_W, GRID_W), 0)
    kc = lax.broadcasted_iota(jnp.int32, (GRID_W, GRID_W), 1)
    col_start = jnp.clip(qc - B_WIN_COLS // 2, 0, GRID_W - B_WIN_COLS)
    col_ok = (kc >= col_start) & (kc < col_start + B_WIN_COLS)
    dc = kc - qc + (B_WIN_COLS - 1)
    for dr in range(n_dr):
        t = jnp.full((GRID_W, GRID_W), NEG_BIG, F32)
        for d in range(n_dc):
            t = jnp.where(dc == d, rpb_ref[(h * n_dr + dr) * n_dc + d] * LOG2E, t)
        t_ref[dr] = jnp.where(col_ok, t, NEG_BIG)
    o_ref[...] = jnp.full(o_ref.shape, NEG_BIG, F32)
    half = B_WIN_ROWS // 2
    for case in range(3):
        for a in range(NBR_ROWS):
            win, off = ((0, a), (a, half), (NBR_UNION - B_WIN_ROWS, half + a))[case]
            for j in range(B_WIN_ROWS):
                o_ref[case, a * GRID_W:(a + 1) * GRID_W, (win + j) * GRID_W:(win + j + 1) * GRID_W] = (
                    t_ref[j - off + B_WIN_ROWS - 1])


def _nbr_bias(rpb):
    n_heads = rpb.shape[0]
    shape = (3, NBR_ROWS * GRID_W, NBR_UNION * GRID_W)
    return pl.pallas_call(
        _nbr_bias_kernel,
        out_shape=jax.ShapeDtypeStruct((n_heads,) + shape, F32),
        grid=(n_heads,),
        in_specs=[pl.BlockSpec(memory_space=pltpu.SMEM)],
        out_specs=pl.BlockSpec((None,) + shape, lambda h: (h, 0, 0, 0)),
        scratch_shapes=[pltpu.VMEM((2 * B_WIN_ROWS - 1, GRID_W, GRID_W), F32)],
        compiler_params=_params(("arbitrary",)),
        name="nbr_bias",
    )(rpb.reshape(-1))


def _nbr_kernel(q_ref, k_ref, v_ref, b_first_ref, b_mid_ref, b_last_ref, o_ref, *, n_rows, n_sub):
    step = pl.program_id(1)
    nq, nk = NBR_ROWS * GRID_W, NBR_UNION * GRID_W
    for sb in range(n_sub):
        bias_ref = b_first_ref if sb == 0 else (b_last_ref if sb == n_sub - 1 else b_mid_ref)
        r0 = (step * n_sub + sb) * NBR_ROWS
        u0 = jnp.clip(r0 - B_WIN_ROWS // 2, 0, n_rows - NBR_UNION)
        kstart = pl.multiple_of(u0 * GRID_W, GRID_W)
        k = k_ref[pl.ds(kstart, nk), :]
        v = v_ref[pl.ds(kstart, nk), :]
        q = q_ref[sb * nq:(sb + 1) * nq, :]
        s = lax.dot_general(q, k, (((1,), (1,)), ((), ())), preferred_element_type=F32) + bias_ref[...]
        m = jnp.max(s, axis=1, keepdims=True)
        p = jnp.exp2(s - m)
        pv, psum = _weighted_values(p, v)
        o_ref[sb * nq:(sb + 1) * nq, :] = (pv / psum).astype(o_ref.dtype)


def _nbr_attention(p_arr, bias, *, seq_len):
    n_rows = seq_len // GRID_W
    n_sub = min(NBR_SUB, n_rows // NBR_ROWS)
    tq = n_sub * NBR_ROWS * GRID_W
    n_steps = seq_len // tq
    qc, kc, vc = (PLAIN_OFF[n] // HEAD_DIM for n in ("bq", "bk", "bv"))
    tile = (None, None, NBR_ROWS * GRID_W, NBR_UNION * GRID_W)
    return pl.pallas_call(
        functools.partial(_nbr_kernel, n_rows=n_rows, n_sub=n_sub),
        out_shape=jax.ShapeDtypeStruct((seq_len, BRANCH_WIDTH), BF16),
        grid=(BRANCH_WIDTH // HEAD_DIM, n_steps),
        in_specs=[pl.BlockSpec((tq, HEAD_DIM), lambda h, i: (i, qc + h)),
                  pl.BlockSpec((seq_len, HEAD_DIM), lambda h, i: (0, kc + h)),
                  pl.BlockSpec((seq_len, HEAD_DIM), lambda h, i: (0, vc + h)),
                  pl.BlockSpec(tile, lambda h, i: (h, jnp.where(i == 0, 0, 1), 0, 0)),
                  pl.BlockSpec(tile, lambda h, i: (h, 1, 0, 0)),
                  pl.BlockSpec(tile, lambda h, i: (h, jnp.where(i == n_steps - 1, 2, 1), 0, 0))],
        out_specs=pl.BlockSpec((tq, HEAD_DIM), lambda h, i: (i, h)),
        compiler_params=_params(("arbitrary", "arbitrary")),
        name="nbr_attention",
    )(p_arr, p_arr, p_arr, bias, bias, bias)


DENSE_RB = 256


def _dense_kernel(q_ref, k_ref, v_ref, o_ref, q2_ref, v2_ref, m_ref, acc_ref, s0_ref, *, seq_len, tq, tk):
    @pl.when(pl.program_id(1) == 0)
    def _():
        v2_ref[:, :HEAD_DIM] = v_ref[...]
        v2_ref[:, HEAD_DIM:] = jnp.ones((seq_len, HEAD_DIM), BF16)

    q2_ref[:tq] = q_ref[:, :HEAD_DIM]
    q2_ref[tq:] = q_ref[:, HEAD_DIM:]
    m_ref[...] = jnp.full(m_ref.shape, NEG_BIG, F32)
    acc_ref[...] = jnp.zeros(acc_ref.shape, F32)
    n_chunks = seq_len // tk

    def logits(r, k):
        return lax.dot_general(q2_ref[pl.ds(r * DENSE_RB, DENSE_RB), :], k, (((1,), (1,)), ((), ())),
                               preferred_element_type=F32)

    s0_ref[...] = logits(0, k_ref[pl.ds(0, tk), :])

    def body(j, carry):
        koff = pl.multiple_of(j * tk, tk)
        k = k_ref[pl.ds(koff, tk), :]
        v2 = v2_ref[pl.ds(koff, tk), :]
        for r in range((2 * tq) // DENSE_RB):
            rows = pl.ds(r * DENSE_RB, DENSE_RB)
            s = s0_ref[...] if r == 0 else logits(r, k)
            m_prev = m_ref[rows, :]
            m_new = jnp.maximum(m_prev, jnp.max(s, axis=1, keepdims=True))
            alpha = jnp.exp2(m_prev - m_new)
            p = jnp.exp2(s - m_new).astype(BF16)
            acc_ref[rows, :] = acc_ref[rows, :] * alpha + jnp.dot(p, v2, preferred_element_type=F32)
            m_ref[rows, :] = m_new
        next_off = pl.multiple_of(jnp.minimum(j + 1, n_chunks - 1) * tk, tk)
        s0_ref[...] = logits(0, k_ref[pl.ds(next_off, tk), :])
        return carry

    lax.fori_loop(0, n_chunks, body, 0)
    acc = acc_ref[...]
    o = acc[:, :HEAD_DIM] / acc[:, HEAD_DIM:]
    o_ref[:, :HEAD_DIM] = o[:tq].astype(o_ref.dtype)
    o_ref[:, HEAD_DIM:] = o[tq:].astype(o_ref.dtype)


def _dense_attention(n_arr, p_arr, *, seq_len, tq=2048, tk=2048):
    tq, tk = min(tq, seq_len), min(tk, seq_len)
    kc = NORM_OFF["dk"] // HEAD_DIM
    vc = PLAIN_OFF["dv"] // HEAD_DIM
    return pl.pallas_call(
        functools.partial(_dense_kernel, seq_len=seq_len, tq=tq, tk=tk),
        out_shape=jax.ShapeDtypeStruct((seq_len, BRANCH_WIDTH), BF16),
        grid=(2, seq_len // tq),
        in_specs=[pl.BlockSpec((tq, 2 * HEAD_DIM), lambda g, i: (i, g)),
                  pl.BlockSpec((seq_len, HEAD_DIM), lambda g, i: (0, kc + g), pipeline_mode=pl.Buffered(1)),
                  pl.BlockSpec((seq_len, HEAD_DIM), lambda g, i: (0, vc + g), pipeline_mode=pl.Buffered(1))],
        out_specs=pl.BlockSpec((tq, 2 * HEAD_DIM), lambda g, i: (i, g)),
        scratch_shapes=[pltpu.VMEM((2 * tq, HEAD_DIM), BF16), pltpu.VMEM((seq_len, 2 * HEAD_DIM), BF16),
                        pltpu.VMEM((2 * tq, 1), F32), pltpu.VMEM((2 * tq, 2 * HEAD_DIM), F32),
                        pltpu.VMEM((DENSE_RB, tk), F32)],
        compiler_params=_params(("arbitrary", "arbitrary")),
        name="dense_attention",
    )(n_arr, n_arr, p_arr)


def _merge_kernel(ya_ref, yb_ref, yc_ref, yd_ref, z_ref, gate_ref, x_ref, wb_ref, wo_ref, g_ref, *out_refs, final):
    merged = None
    for n, y_ref in enumerate((ya_ref, yb_ref, yc_ref, yd_ref)):
        z = z_ref[:, n * BRANCH_WIDTH:(n + 1) * BRANCH_WIDTH].astype(F32)
        t = (y_ref[...].astype(F32) * (z * _sigmoid(z))).astype(BF16)
        br = jnp.dot(t, wb_ref[n], preferred_element_type=F32)
        term = gate_ref[:, n * D_MODEL:(n + 1) * D_MODEL].astype(F32) * br
        merged = term if merged is None else merged + term
    x_new = x_ref[...] + jnp.dot(merged.astype(BF16), wo_ref[...], preferred_element_type=F32)
    normed = x_new * lax.rsqrt(jnp.mean(x_new * x_new, axis=-1, keepdims=True) + NORM_EPS) * g_ref[...]
    if final:
        out_refs[0][...] = normed
    else:
        out_refs[0][...] = x_new
        out_refs[1][...] = normed.astype(BF16)


def _merge(ys, p_arr, gates, x, wb, wo, g_next, *, final, tm=256):
    s = x.shape[0]
    row = lambda w: pl.BlockSpec((tm, w), lambda i: (i, 0))
    once = dict(pipeline_mode=pl.Buffered(1))
    in_specs = ([row(BRANCH_WIDTH)] * N_BRANCH
                + [row(N_BRANCH * BRANCH_WIDTH), row(N_BRANCH * D_MODEL), row(D_MODEL),
                   pl.BlockSpec((N_BRANCH, BRANCH_WIDTH, D_MODEL), lambda i: (0, 0, 0), **once),
                   pl.BlockSpec((D_MODEL, D_MODEL), lambda i: (0, 0), **once),
                   pl.BlockSpec((1, D_MODEL), lambda i: (0, 0))])
    if final:
        out_shape = [jax.ShapeDtypeStruct((s, D_MODEL), F32)]
    else:
        out_shape = [jax.ShapeDtypeStruct((s, D_MODEL), F32), jax.ShapeDtypeStruct((s, D_MODEL), BF16)]
    return pl.pallas_call(
        functools.partial(_merge_kernel, final=final),
        out_shape=out_shape,
        grid=(s // tm,),
        in_specs=in_specs,
        out_specs=[row(D_MODEL)] * len(out_shape),
        compiler_params=_params(("arbitrary",)),
        name="merge",
    )(*ys, p_arr, gates, x, wb, wo, g_next.reshape(1, D_MODEL))


@functools.lru_cache(maxsize=None)
def _rope_tables(seq_len):
    t = np.arange(seq_len)

    def angles(pos, dim):
        inv_freq = np.float32(ROPE_THETA) ** (-np.arange(dim // 2, dtype=np.float32) / np.float32(dim // 2))
        return pos.astype(np.float32)[:, None] * inv_freq.astype(np.float32)[None, :]

    a1 = angles(t, HEAD_DIM)
    cos1 = np.concatenate([np.cos(a1), np.cos(a1)], axis=1)
    sin1 = np.concatenate([-np.sin(a1), np.sin(a1)], axis=1)
    ar = angles(t // GRID_W, HEAD_DIM // 2)
    ac = angles(t % GRID_W, HEAD_DIM // 2)
    cos2 = np.concatenate([np.cos(ar), np.cos(ac), np.cos(ar), np.cos(ac)], axis=1)
    sin2 = np.concatenate([-np.sin(ar), -np.sin(ac), np.sin(ar), np.sin(ac)], axis=1)
    assert all(a.dtype == np.float32 for a in (cos1, sin1, cos2, sin2))
    return (cos1, sin1), (cos2, sin2)


def _axial_reorder(a):
    lead = a.shape[:-1]
    return a.reshape(*lead, -1, 2, 2, HEAD_DIM // 4).swapaxes(-3, -2).reshape(*lead, -1)


def _gather_cols(w, seg):
    cols = np.concatenate([np.arange(_OFF[n], _OFF[n] + _WID[n]) for n in seg])
    runs = np.split(cols, np.nonzero(np.diff(cols) != 1)[0] + 1)
    return jnp.concatenate([w[:, r[0]:r[-1] + 1] for r in runs], axis=1).astype(BF16)


def _col_scale(seg, scaled):
    return jnp.asarray(np.concatenate(
        [np.full((_WID[n],), scaled.get(n, 1.0), np.float32) for n in seg]).reshape(1, -1))


def _layer(x, h, w_in_all, layer, norm_gain, gate_b, a_sink, b_rpb, d_q_norm, d_k_norm, w_branch, w_out, g_next,
           tables, final):
    s = x.shape[0]
    (cos1, sin1), (cos2, sin2) = tables
    tm = 1024
    w_in = w_in_all[layer, :, :_OFF["gates"]]

    gains = _axial_reorder(jnp.concatenate([jnp.tile(d_q_norm, _WID["dq"] // HEAD_DIM),
                                            jnp.tile(d_k_norm, _WID["dk"] // HEAD_DIM)]))
    n_arr = _proj(x if h is None else h, _axial_reorder(_gather_cols(w_in, NORM_SEG)), "norm_rope",
                  [_col_scale(NORM_SEG, {"dq": SCALE}), gains.reshape(1, -1)],
                  [cos2, sin2], tm=tm, tn=NORM_W, name="proj_norm_rope", norm_gain=norm_gain if h is None else None)
    if h is None:
        n_arr, h = n_arr
    r_arr = _proj(h, _gather_cols(w_in, ROPE_SEG), "rope", [_col_scale(ROPE_SEG, {"aq": SCALE, "cq": SCALE})],
                  [cos1, sin1], tm=2 * tm, tn=ROPE_W // 2, name="proj_rope")
    p_arr = _proj(h, _gather_cols(w_in, PLAIN_SEG), "scale", [_col_scale(PLAIN_SEG, {"bq": SCALE})], [],
                  tm=2 * tm, tn=PLAIN_W // 3, name="proj_plain")
    gates = _proj_gates(h, w_in_all, layer, gate_b.reshape(1, -1))

    ya = _window_attention(r_arr, p_arr, a_sink, seq_len=s)
    yb = _nbr_attention(p_arr, _nbr_bias(b_rpb), seq_len=s)
    yc = _dilated_attention(r_arr, p_arr, seq_len=s)
    yd = _dense_attention(n_arr, p_arr, seq_len=s)
    return _merge((ya, yb, yc, yd), p_arr, gates, x, w_branch.astype(BF16), w_out.astype(BF16), g_next, final=final)


def kernel(x, norm_g, w_in, gate_b, a_sink, b_rpb, d_q_norm, d_k_norm, w_branch, w_out, final_norm_g):
    b, s, d = x.shape
    depth = w_in.shape[0]
    tables = tuple(tuple(jnp.asarray(a) for a in grp) for grp in _rope_tables(s))
    outs = []
    for bi in range(b):
        xb = x.reshape(s, d) if b == 1 else x[bi]
        h = None
        for layer in range(depth):
            final = layer == depth - 1
            g_next = final_norm_g if final else norm_g[layer + 1]
            res = _layer(xb, h, w_in, layer, norm_g[layer], gate_b[layer], a_sink[layer], b_rpb[layer], d_q_norm[layer],
                         d_k_norm[layer], w_branch[layer], w_out[layer], g_next, tables, final)
            if final:
                xb = res[0]
            else:
                xb, h = res
        outs.append(xb)
    return outs[0].reshape(1, s, d) if b == 1 else jnp.stack(outs, axis=0)
```

```python
import functools
import math

import numpy as np
import jax
import jax.numpy as jnp
from jax import lax
from jax.experimental import pallas as pl
from jax.experimental.pallas import tpu as pltpu

F32 = jnp.float32
BF16 = jnp.bfloat16

D_MODEL = 2048
HEAD_DIM = 128
BRANCH_WIDTH = 512
N_BRANCH = 4
A_HALF_WINDOW = 128
B_WIN_ROWS = 8
B_WIN_COLS = 16
C_PATTERNS = ((128, 1), (512, 4), (2048, 16))
GRID_W = 64
ROPE_THETA = 10000.0
NORM_EPS = 1e-6
LOG2E = math.log2(math.e)
SCALE = HEAD_DIM ** -0.5 * LOG2E
NEG_BIG = -1e30

VMEM_LIMIT = 56 * 1024 * 1024

_SPLITS = (512, 256, 256, 512, 512, 512, 512, 512, 1536, 512, 512, 512, 512, 256, 256, 512, 8192)
_NAMES = ("aq", "ak", "av", "az", "bq", "bk", "bv", "bz", "cq", "ck", "cv", "cz", "dq", "dk", "dv", "dz", "gates")
_OFF = dict(zip(_NAMES, np.concatenate([[0], np.cumsum(_SPLITS)[:-1]]).tolist()))
_WID = dict(zip(_NAMES, _SPLITS))

ROPE_SEG = ("aq", "ak", "cq", "ck")
NORM_SEG = ("dq", "dk")
PLAIN_SEG = ("az", "bz", "cz", "dz", "av", "dv", "bq", "bk", "bv", "cv")


def _seg_offsets(seg):
    offs, o = {}, 0
    for n in seg:
        offs[n] = o
        o += _WID[n]
    return offs, o


ROPE_OFF, ROPE_W = _seg_offsets(ROPE_SEG)
NORM_OFF, NORM_W = _seg_offsets(NORM_SEG)
PLAIN_OFF, PLAIN_W = _seg_offsets(PLAIN_SEG)


def _params(sem):
    return pltpu.CompilerParams(dimension_semantics=sem, vmem_limit_bytes=VMEM_LIMIT)


def _sigmoid(v):
    return 1.0 / (1.0 + jnp.exp(-v))


def _weighted_values(p, v):
    v1 = jnp.concatenate([v, jnp.ones_like(v)], axis=1)
    r = jnp.dot(p.astype(BF16), v1, preferred_element_type=F32)
    return r[:, :HEAD_DIM], r[:, HEAD_DIM:]


PROJ_RB = 256


def _rope_half(xh, cos, sin_signed):
    return xh * cos + pltpu.roll(xh, HEAD_DIM // 2, 1) * sin_signed


def _proj_kernel(*refs, mode, norm_input):
    if norm_input:
        g_in_ref, refs, h_out_ref = refs[0], refs[1:-1], refs[-1]
    h_ref, w_ref = refs[0], refs[1]
    o_ref = refs[-1]
    tm, tn = o_ref.shape
    for r in range(tm // PROJ_RB):
        rows = pl.ds(r * PROJ_RB, PROJ_RB)
        if norm_input:
            x = h_ref[rows, :]
            h = (x * lax.rsqrt(jnp.mean(x * x, axis=-1, keepdims=True) + NORM_EPS) * g_in_ref[...]).astype(BF16)
            h_out_ref[rows, :] = h
        else:
            h = h_ref[rows, :]
        acc = jnp.dot(h, w_ref[...], preferred_element_type=F32)
        if mode == "scale":
            o_ref[rows, :] = (acc * refs[2][...]).astype(o_ref.dtype)
        elif mode == "rope":
            cs_ref, cos_ref, sin_ref = refs[2:5]
            cos, sin = cos_ref[rows, :], sin_ref[rows, :]
            for c in range(tn // HEAD_DIM):
                sl = slice(c * HEAD_DIM, (c + 1) * HEAD_DIM)
                o_ref[rows, sl] = (_rope_half(acc[:, sl], cos, sin) * cs_ref[:, sl]).astype(o_ref.dtype)
        elif mode == "norm_rope":
            cs_ref, g_ref, cos_ref, sin_ref = refs[2:6]
            cos, sin = cos_ref[rows, :], sin_ref[rows, :]
            for c in range(tn // HEAD_DIM):
                sl = slice(c * HEAD_DIM, (c + 1) * HEAD_DIM)
                xh = acc[:, sl]
                y = xh * lax.rsqrt(jnp.mean(xh * xh, axis=-1, keepdims=True) + NORM_EPS) * g_ref[:, sl]
                o_ref[rows, sl] = (_rope_half(y, cos, sin) * cs_ref[:, sl]).astype(o_ref.dtype)
        else:
            raise ValueError(mode)


def _proj(h, w, mode, row_vecs, tables, tm, tn, name, norm_gain=None):
    s, d = h.shape
    n = w.shape[1]
    tm = min(tm, s)
    in_specs = [pl.BlockSpec((tm, d), lambda i, j: (i, 0)), pl.BlockSpec((d, tn), lambda i, j: (0, j))]
    in_specs += [pl.BlockSpec((1, tn), lambda i, j: (0, j)) for _ in row_vecs]
    in_specs += [pl.BlockSpec((tm, HEAD_DIM), lambda i, j: (i, 0)) for _ in tables]
    out_shape = [jax.ShapeDtypeStruct((s, n), BF16)]
    out_specs = [pl.BlockSpec((tm, tn), lambda i, j: (i, j))]
    args = [h, w, *row_vecs, *tables]
    if norm_gain is not None:
        assert n == tn
        in_specs = [pl.BlockSpec((1, d), lambda i, j: (0, 0))] + in_specs
        args = [norm_gain.reshape(1, d)] + args
        out_shape.append(jax.ShapeDtypeStruct((s, d), BF16))
        out_specs.append(pl.BlockSpec((tm, d), lambda i, j: (i, 0)))
    outs = pl.pallas_call(
        functools.partial(_proj_kernel, mode=mode, norm_input=norm_gain is not None),
        out_shape=out_shape,
        grid=(s // tm, n // tn),
        in_specs=in_specs,
        out_specs=out_specs,
        compiler_params=_params(("arbitrary", "arbitrary")),
        name=name,
    )(*args)
    return outs[0] if norm_gain is None else outs


def _gates_kernel(h_ref, w_ref, b_ref, o_ref, wb_ref):
    @pl.when(pl.program_id(1) == 0)
    def _():
        wb_ref[...] = w_ref[...].astype(BF16)

    for r in range(o_ref.shape[0] // PROJ_RB):
        rows = pl.ds(r * PROJ_RB, PROJ_RB)
        acc = jnp.dot(h_ref[rows, :], wb_ref[...], preferred_element_type=F32)
        o_ref[rows, :] = _sigmoid(acc + b_ref[...]).astype(o_ref.dtype)


def _proj_gates(h, w_in_all, layer, bias, tm=2048, tn=1024):
    s, d = h.shape
    tm = min(tm, s)
    n = _WID["gates"]
    first = _OFF["gates"] // tn
    return pl.pallas_call(
        _gates_kernel,
        out_shape=jax.ShapeDtypeStruct((s, n), BF16),
        grid=(n // tn, s // tm),
        in_specs=[pl.BlockSpec((tm, d), lambda j, i: (i, 0)),
                  pl.BlockSpec((None, d, tn), lambda j, i: (layer, 0, first + j)),
                  pl.BlockSpec((1, tn), lambda j, i: (0, j))],
        out_specs=pl.BlockSpec((tm, tn), lambda j, i: (i, j)),
        scratch_shapes=[pltpu.VMEM((d, tn), BF16)],
        compiler_params=_params(("arbitrary", "arbitrary")),
        name="proj_gates",
    )(h, w_in_all, bias)


WIN_BQ = 128


def _window_consts(sink_ref, kv_head):
    bq, groups = WIN_BQ, 2
    span, rows = bq + 2 * A_HALF_WINDOW, groups * bq
    row = lax.broadcasted_iota(jnp.int32, (rows, span), 0)
    col_minus_row = lax.broadcasted_iota(jnp.int32, (rows, span), 1) - jnp.where(row >= bq, row - bq, row)
    r1 = lax.broadcasted_iota(jnp.int32, (rows, 1), 0)
    sink = jnp.where(r1 >= bq, sink_ref[kv_head * groups + 1], sink_ref[kv_head * groups]) * LOG2E
    return col_minus_row, sink


def _window_block(consts, q_ref, k_ref, v_ref, o_ref, q0, qoff, seq_len):
    col_minus_row, sink = consts
    bq, hw = WIN_BQ, A_HALF_WINDOW
    span = bq + 2 * hw
    start = pl.multiple_of(jnp.clip(q0 - hw, 0, seq_len - span), bq)
    k = k_ref[pl.ds(start, span), :]
    v = v_ref[pl.ds(start, span), :]
    qb = q_ref[pl.ds(qoff, bq), :]
    q2 = jnp.concatenate([qb[:, :HEAD_DIM], qb[:, HEAD_DIM:]], axis=0)
    s = lax.dot_general(q2, k, (((1,), (1,)), ((), ())), preferred_element_type=F32)
    s = jnp.where(jnp.abs(col_minus_row + (start - q0)) <= hw, s, NEG_BIG)
    m = jnp.maximum(jnp.max(s, axis=1, keepdims=True), sink)
    p = jnp.exp2(s - m)
    pv, psum = _weighted_values(p, v)
    o = pv / (psum + jnp.exp2(sink - m))
    o_ref[pl.ds(qoff, bq), :HEAD_DIM] = o[:bq].astype(o_ref.dtype)
    o_ref[pl.ds(qoff, bq), HEAD_DIM:] = o[bq:].astype(o_ref.dtype)


def _window_kernel(sink_ref, q_ref, k_ref, v_ref, o_ref, *, seq_len, nsub):
    consts = _window_consts(sink_ref, pl.program_id(0))
    t0 = pl.program_id(1) * nsub * WIN_BQ

    def body(b, carry):
        qoff = pl.multiple_of(b * WIN_BQ, WIN_BQ)
        _window_block(consts, q_ref, k_ref, v_ref, o_ref, t0 + qoff, qoff, seq_len)
        return carry

    lax.fori_loop(0, nsub, body, 0, unroll=True)


def _window_attention(r_arr, p_arr, sink, *, seq_len, max_step=2048):
    nsub = min(max_step, seq_len) // WIN_BQ
    tq = nsub * WIN_BQ
    qc, kc, vc = ROPE_OFF["aq"] // (2 * HEAD_DIM), ROPE_OFF["ak"] // HEAD_DIM, PLAIN_OFF["av"] // HEAD_DIM
    return pl.pallas_call(
        functools.partial(_window_kernel, seq_len=seq_len, nsub=nsub),
        out_shape=jax.ShapeDtypeStruct((seq_len, BRANCH_WIDTH), BF16),
        grid=(2, seq_len // tq),
        in_specs=[pl.BlockSpec(memory_space=pltpu.SMEM),
                  pl.BlockSpec((tq, 2 * HEAD_DIM), lambda c, i: (i, qc + c)),
                  pl.BlockSpec((seq_len, HEAD_DIM), lambda c, i: (0, kc + c)),
                  pl.BlockSpec((seq_len, HEAD_DIM), lambda c, i: (0, vc + c))],
        out_specs=pl.BlockSpec((tq, 2 * HEAD_DIM), lambda c, i: (i, c)),
        compiler_params=_params(("arbitrary", "arbitrary")),
        name="window_attention",
    )(sink, r_arr, r_arr, p_arr)


DIL_BQ = 128
DIL_HALF = 64
DIL_HALO = DIL_HALF * max(d for _, d in C_PATTERNS)


DIL_BLK = 128


def _residue_major(dil):
    per = DIL_BLK // dil
    p = np.zeros((DIL_BLK, DIL_BLK), np.float32)
    for r in range(dil):
        for i in range(per):
            p[r * per + i, i * dil + r] = 1.0
    return p


def _to_residue_major(src_ref, dst_ref, perm, dil, first_blk, n_blk):
    per = DIL_BLK // dil
    for b in range(first_blk, first_blk + n_blk, 2):
        pair = jnp.concatenate([src_ref[b * DIL_BLK:(b + 1) * DIL_BLK, :],
                                src_ref[(b + 1) * DIL_BLK:(b + 2) * DIL_BLK, :]], axis=1)
        res = jnp.dot(perm, pair, preferred_element_type=F32)
        dst_ref[b * dil:(b + 1) * dil] = res[:, :HEAD_DIM].reshape(dil, per, HEAD_DIM)
        dst_ref[(b + 1) * dil:(b + 2) * dil] = res[:, HEAD_DIM:].reshape(dil, per, HEAD_DIM)


def _dilated_kernel(q0_ref, q1_ref, q2_ref, kp_ref, kc_ref, kn_ref, vp_ref, vc_ref, vn_ref, perm1_ref, perm2_ref,
                    o_ref, kw_ref, vw_ref, qd1_ref, kd1_ref, vd1_ref, qd2_ref, kd2_ref, vd2_ref, og_ref, lg_ref,
                    *, seq_len, tq):
    t0 = pl.program_id(1) * tq
    halo, bq, half = DIL_HALO, DIL_BQ, DIL_HALF
    span = bq + 2 * half
    for w_ref, prev, cur, nxt in ((kw_ref, kp_ref, kc_ref, kn_ref), (vw_ref, vp_ref, vc_ref, vn_ref)):
        w_ref[:halo] = prev[...]
        w_ref[halo:halo + tq] = cur[...]
        w_ref[halo + tq:] = nxt[...]
    residue_major = {}
    for (_, dil), q_ref, perm_ref, qd_ref, kd_ref, vd_ref in (
            (C_PATTERNS[1], q1_ref, perm1_ref, qd1_ref, kd1_ref, vd1_ref),
            (C_PATTERNS[2], q2_ref, perm2_ref, qd2_ref, kd2_ref, vd2_ref)):
        perm = perm_ref[...]
        first_blk = (halo - half * dil) // DIL_BLK
        n_blk = tq // DIL_BLK + dil
        _to_residue_major(q_ref, qd_ref, perm, dil, 0, tq // DIL_BLK)
        _to_residue_major(kw_ref, kd_ref, perm, dil, first_blk, n_blk)
        _to_residue_major(vw_ref, vd_ref, perm, dil, first_blk, n_blk)
        residue_major[dil] = (qd_ref, kd_ref, vd_ref)

    ii = lax.broadcasted_iota(jnp.int32, (bq, span), 0)
    jj = lax.broadcasted_iota(jnp.int32, (bq, span), 1)
    in_band = jnp.abs(jj - half - ii) <= half

    for g, (_, dil) in enumerate(C_PATTERNS):
        nb = tq // (bq * dil)

        def body(it, carry, g=g, dil=dil, nb=nb):
            res, sub = it // nb, it % nb
            q_row = res + dil * bq * sub
            w_row = halo + q_row - half * dil
            if dil == 1:
                q = q0_ref[pl.ds(pl.multiple_of(q_row, bq), bq), :]
                k = kw_ref[pl.ds(pl.multiple_of(w_row, half), span), :]
                v = vw_ref[pl.ds(pl.multiple_of(w_row, half), span), :]
            else:
                qd_ref, kd_ref, vd_ref = residue_major[dil]
                q_blk = sub * bq * dil // DIL_BLK
                k_blk = (halo - half * dil) // DIL_BLK + q_blk
                q = qd_ref[pl.ds(q_blk * dil + res, bq * dil // DIL_BLK, stride=dil)]
                k = kd_ref[pl.ds(k_blk * dil + res, span * dil // DIL_BLK, stride=dil)]
                v = vd_ref[pl.ds(k_blk * dil + res, span * dil // DIL_BLK, stride=dil)]
                q = q.reshape(bq, HEAD_DIM).astype(BF16)
                k = k.reshape(span, HEAD_DIM).astype(BF16)
                v = v.reshape(span, HEAD_DIM).astype(BF16)
            s = lax.dot_general(q, k, (((1,), (1,)), ((), ())), preferred_element_type=F32)
            ktok = t0 + q_row - half * dil + dil * jj
            s = jnp.where(in_band & (ktok >= 0) & (ktok < seq_len), s, NEG_BIG)
            m = jnp.max(s, axis=1, keepdims=True)
            p = jnp.exp2(s - m)
            pv, psum = _weighted_values(p, v)
            og_ref[g, pl.ds(q_row, bq, stride=dil), :] = pv / psum
            lg_ref[g, pl.ds(q_row, bq, stride=dil), :] = m + jnp.log2(psum)
            return carry

        lax.fori_loop(0, tq // bq, body, 0, unroll=True)

    l0, l1, l2 = lg_ref[0], lg_ref[1], lg_ref[2]
    mx = jnp.maximum(jnp.maximum(l0, l1), l2)
    e0, e1, e2 = jnp.exp2(l0 - mx), jnp.exp2(l1 - mx), jnp.exp2(l2 - mx)
    o_ref[...] = ((e0 * og_ref[0] + e1 * og_ref[1] + e2 * og_ref[2]) / (e0 + e1 + e2)).astype(o_ref.dtype)


def _dilated_attention(r_arr, p_arr, *, seq_len, tq=4096):
    tq = min(tq, seq_len)
    halo = DIL_HALO
    ratio, n_halo = tq // halo, seq_len // halo
    qc, kc, vc = ROPE_OFF["cq"] // HEAD_DIM, ROPE_OFF["ck"] // HEAD_DIM, PLAIN_OFF["cv"] // HEAD_DIM
    n_heads = BRANCH_WIDTH // HEAD_DIM

    def window_specs(col):
        return [pl.BlockSpec((halo, HEAD_DIM), lambda h, i: (jnp.maximum(i * ratio - 1, 0), col + h)),
                pl.BlockSpec((tq, HEAD_DIM), lambda h, i: (i, col + h)),
                pl.BlockSpec((halo, HEAD_DIM), lambda h, i: (jnp.minimum((i + 1) * ratio, n_halo - 1), col + h))]

    q_specs = [pl.BlockSpec((tq, HEAD_DIM), lambda h, i, g=g: (i, qc + n_heads * g + h)) for g in range(3)]
    perm_spec = pl.BlockSpec((DIL_BLK, DIL_BLK), lambda h, i: (0, 0))
    dilations = [d for _, d in C_PATTERNS[1:]]
    assert C_PATTERNS[0][1] == 1 and all(DIL_BLK % d == 0 and (DIL_BLK // d) % 8 == 0 for d in dilations)
    window = tq + 2 * halo
    residue_scratch = []
    for d in dilations:
        residue_scratch += [pltpu.VMEM((tq // DIL_BLK * d, DIL_BLK // d, HEAD_DIM), F32),
                            pltpu.VMEM((window // DIL_BLK * d, DIL_BLK // d, HEAD_DIM), F32),
                            pltpu.VMEM((window // DIL_BLK * d, DIL_BLK // d, HEAD_DIM), F32)]
    return pl.pallas_call(
        functools.partial(_dilated_kernel, seq_len=seq_len, tq=tq),
        out_shape=jax.ShapeDtypeStruct((seq_len, BRANCH_WIDTH), BF16),
        grid=(n_heads, seq_len // tq),
        in_specs=q_specs + window_specs(kc) + window_specs(vc) + [perm_spec] * len(dilations),
        out_specs=pl.BlockSpec((tq, HEAD_DIM), lambda h, i: (i, h)),
        scratch_shapes=[pltpu.VMEM((window, HEAD_DIM), BF16), pltpu.VMEM((window, HEAD_DIM), BF16)]
        + residue_scratch + [pltpu.VMEM((3, tq, HEAD_DIM), F32), pltpu.VMEM((3, tq, HEAD_DIM), F32)],
        compiler_params=_params(("arbitrary", "arbitrary")),
        name="dilated_attention",
    )(r_arr, r_arr, r_arr, r_arr, r_arr, r_arr, p_arr, p_arr, p_arr,
      *[jnp.asarray(_residue_major(d), BF16) for d in dilations])


NBR_ROWS = 4
NBR_UNION = 12
NBR_SUB = 32


def _nbr_bias_kernel(rpb_ref, o_ref, t_ref):
    h = pl.program_id(0)
    n_dr, n_dc = 2 * B_WIN_ROWS - 1, 2 * B_WIN_COLS - 1
    qc = lax.broadcasted_iota(jnp.int32, (GRID_W, GRID_W), 0)
    kc = lax.broadcasted_iota(jnp.int32, (GRID_W, GRID_W), 1)
    col_start = jnp.clip(qc - B_WIN_COLS // 2, 0, GRID_W - B_WIN_COLS)
    col_ok = (kc >= col_start) & (kc < col_start + B_WIN_COLS)
    dc = kc - qc + (B_WIN_COLS - 1)
    for dr in range(n_dr):
        t = jnp.full((GRID_W, GRID_W), NEG_BIG, F32)
        for d in range(n_dc):
            t = jnp.where(dc == d, rpb_ref[(h * n_dr + dr) * n_dc + d] * LOG2E, t)
        t_ref[dr] = jnp.where(col_ok, t, NEG_BIG)
    o_ref[...] = jnp.full(o_ref.shape, NEG_BIG, F32)
    half = B_WIN_ROWS // 2
    for case in range(3):
        for a in range(NBR_ROWS):
            win, off = ((0, a), (a, half), (NBR_UNION - B_WIN_ROWS, half + a))[case]
            for j in range(B_WIN_ROWS):
                o_ref[case, a * GRID_W:(a + 1) * GRID_W, (win + j) * GRID_W:(win + j + 1) * GRID_W] = (
                    t_ref[j - off + B_WIN_ROWS - 1])


def _nbr_bias(rpb):
    n_heads = rpb.shape[0]
    shape = (3, NBR_ROWS * GRID_W, NBR_UNION * GRID_W)
    return pl.pallas_call(
        _nbr_bias_kernel,
        out_shape=jax.ShapeDtypeStruct((n_heads,) + shape, F32),
        grid=(n_heads,),
        in_specs=[pl.BlockSpec(memory_space=pltpu.SMEM)],
        out_specs=pl.BlockSpec((None,) + shape, lambda h: (h, 0, 0, 0)),
        scratch_shapes=[pltpu.VMEM((2 * B_WIN_ROWS - 1, GRID_W, GRID_W), F32)],
        compiler_params=_params(("arbitrary",)),
        name="nbr_bias",
    )(rpb.reshape(-1))


def _nbr_kernel(q_ref, k_ref, v_ref, b_first_ref, b_mid_ref, b_last_ref, o_ref, *, n_rows, n_sub):
    step = pl.program_id(1)
    nq, nk = NBR_ROWS * GRID_W, NBR_UNION * GRID_W
    for sb in range(n_sub):
        bias_ref = b_first_ref if sb == 0 else (b_last_ref if sb == n_sub - 1 else b_mid_ref)
        r0 = (step * n_sub + sb) * NBR_ROWS
        u0 = jnp.clip(r0 - B_WIN_ROWS // 2, 0, n_rows - NBR_UNION)
        kstart = pl.multiple_of(u0 * GRID_W, GRID_W)
        k = k_ref[pl.ds(kstart, nk), :]
        v = v_ref[pl.ds(kstart, nk), :]
        q = q_ref[sb * nq:(sb + 1) * nq, :]
        s = lax.dot_general(q, k, (((1,), (1,)), ((), ())), preferred_element_type=F32) + bias_ref[...]
        m = jnp.max(s, axis=1, keepdims=True)
        p = jnp.exp2(s - m)
        pv, psum = _weighted_values(p, v)
        o_ref[sb * nq:(sb + 1) * nq, :] = (pv / psum).astype(o_ref.dtype)


def _nbr_attention(p_arr, bias, *, seq_len):
    n_rows = seq_len // GRID_W
    n_sub = min(NBR_SUB, n_rows // NBR_ROWS)
    tq = n_sub * NBR_ROWS * GRID_W
    n_steps = seq_len // tq
    qc, kc, vc = (PLAIN_OFF[n] // HEAD_DIM for n in ("bq", "bk", "bv"))
    tile = (None, None, NBR_ROWS * GRID_W, NBR_UNION * GRID_W)
    return pl.pallas_call(
        functools.partial(_nbr_kernel, n_rows=n_rows, n_sub=n_sub),
        out_shape=jax.ShapeDtypeStruct((seq_len, BRANCH_WIDTH), BF16),
        grid=(BRANCH_WIDTH // HEAD_DIM, n_steps),
        in_specs=[pl.BlockSpec((tq, HEAD_DIM), lambda h, i: (i, qc + h)),
                  pl.BlockSpec((seq_len, HEAD_DIM), lambda h, i: (0, kc + h)),
                  pl.BlockSpec((seq_len, HEAD_DIM), lambda h, i: (0, vc + h)),
                  pl.BlockSpec(tile, lambda h, i: (h, jnp.where(i == 0, 0, 1), 0, 0)),
                  pl.BlockSpec(tile, lambda h, i: (h, 1, 0, 0)),
                  pl.BlockSpec(tile, lambda h, i: (h, jnp.where(i == n_steps - 1, 2, 1), 0, 0))],
        out_specs=pl.BlockSpec((tq, HEAD_DIM), lambda h, i: (i, h)),
        compiler_params=_params(("arbitrary", "arbitrary")),
        name="nbr_attention",
    )(p_arr, p_arr, p_arr, bias, bias, bias)


DENSE_RB = 256


def _dense_kernel(q_ref, k_ref, v_ref, o_ref, q2_ref, v2_ref, m_ref, acc_ref, s0_ref, *, seq_len, tq, tk):
    @pl.when(pl.program_id(1) == 0)
    def _():
        v2_ref[:, :HEAD_DIM] = v_ref[...]
        v2_ref[:, HEAD_DIM:] = jnp.ones((seq_len, HEAD_DIM), BF16)

    q2_ref[:tq] = q_ref[:, :HEAD_DIM]
    q2_ref[tq:] = q_ref[:, HEAD_DIM:]
    m_ref[...] = jnp.full(m_ref.shape, NEG_BIG, F32)
    acc_ref[...] = jnp.zeros(acc_ref.shape, F32)
    n_chunks = seq_len // tk

    def logits(r, k):
        return lax.dot_general(q2_ref[pl.ds(r * DENSE_RB, DENSE_RB), :], k, (((1,), (1,)), ((), ())),
                               preferred_element_type=F32)

    s0_ref[...] = logits(0, k_ref[pl.ds(0, tk), :])

    def body(j, carry):
        koff = pl.multiple_of(j * tk, tk)
        k = k_ref[pl.ds(koff, tk), :]
        v2 = v2_ref[pl.ds(koff, tk), :]
        for r in range((2 * tq) // DENSE_RB):
            rows = pl.ds(r * DENSE_RB, DENSE_RB)
            s = s0_ref[...] if r == 0 else logits(r, k)
            m_prev = m_ref[rows, :]
            m_new = jnp.maximum(m_prev, jnp.max(s, axis=1, keepdims=True))
            alpha = jnp.exp2(m_prev - m_new)
            p = jnp.exp2(s - m_new).astype(BF16)
            acc_ref[rows, :] = acc_ref[rows, :] * alpha + jnp.dot(p, v2, preferred_element_type=F32)
            m_ref[rows, :] = m_new
        next_off = pl.multiple_of(jnp.minimum(j + 1, n_chunks - 1) * tk, tk)
        s0_ref[...] = logits(0, k_ref[pl.ds(next_off, tk), :])
        return carry

    lax.fori_loop(0, n_chunks, body, 0)
    acc = acc_ref[...]
    o = acc[:, :HEAD_DIM] / acc[:, HEAD_DIM:]
    o_ref[:, :HEAD_DIM] = o[:tq].astype(o_ref.dtype)
    o_ref[:, HEAD_DIM:] = o[tq:].astype(o_ref.dtype)


def _dense_attention(n_arr, p_arr, *, seq_len, tq=2048, tk=2048):
    tq, tk = min(tq, seq_len), min(tk, seq_len)
    kc = NORM_OFF["dk"] // HEAD_DIM
    vc = PLAIN_OFF["dv"] // HEAD_DIM
    return pl.pallas_call(
        functools.partial(_dense_kernel, seq_len=seq_len, tq=tq, tk=tk),
        out_shape=jax.ShapeDtypeStruct((seq_len, BRANCH_WIDTH), BF16),
        grid=(2, seq_len // tq),
        in_specs=[pl.BlockSpec((tq, 2 * HEAD_DIM), lambda g, i: (i, g)),
                  pl.BlockSpec((seq_len, HEAD_DIM), lambda g, i: (0, kc + g), pipeline_mode=pl.Buffered(1)),
                  pl.BlockSpec((seq_len, HEAD_DIM), lambda g, i: (0, vc + g), pipeline_mode=pl.Buffered(1))],
        out_specs=pl.BlockSpec((tq, 2 * HEAD_DIM), lambda g, i: (i, g)),
        scratch_shapes=[pltpu.VMEM((2 * tq, HEAD_DIM), BF16), pltpu.VMEM((seq_len, 2 * HEAD_DIM), BF16),
                        pltpu.VMEM((2 * tq, 1), F32), pltpu.VMEM((2 * tq, 2 * HEAD_DIM), F32),
                        pltpu.VMEM((DENSE_RB, tk), F32)],
        compiler_params=_params(("arbitrary", "arbitrary")),
        name="dense_attention",
    )(n_arr, n_arr, p_arr)


def _merge_kernel(ya_ref, yb_ref, yc_ref, yd_ref, z_ref, gate_ref, x_ref, wb_ref, wo_ref, g_ref, *out_refs, final):
    merged = None
    for n, y_ref in enumerate((ya_ref, yb_ref, yc_ref, yd_ref)):
        z = z_ref[:, n * BRANCH_WIDTH:(n + 1) * BRANCH_WIDTH].astype(F32)
        t = (y_ref[...].astype(F32) * (z * _sigmoid(z))).astype(BF16)
        br = jnp.dot(t, wb_ref[n], preferred_element_type=F32)
        term = gate_ref[:, n * D_MODEL:(n + 1) * D_MODEL].astype(F32) * br
        merged = term if merged is None else merged + term
    x_new = x_ref[...] + jnp.dot(merged.astype(BF16), wo_ref[...], preferred_element_type=F32)
    normed = x_new * lax.rsqrt(jnp.mean(x_new * x_new, axis=-1, keepdims=True) + NORM_EPS) * g_ref[...]
    if final:
        out_refs[0][...] = normed
    else:
        out_refs[0][...] = x_new
        out_refs[1][...] = normed.astype(BF16)


def _merge(ys, p_arr, gates, x, wb, wo, g_next, *, final, tm=256):
    s = x.shape[0]
    row = lambda w: pl.BlockSpec((tm, w), lambda i: (i, 0))
    once = dict(pipeline_mode=pl.Buffered(1))
    in_specs = ([row(BRANCH_WIDTH)] * N_BRANCH
                + [row(N_BRANCH * BRANCH_WIDTH), row(N_BRANCH * D_MODEL), row(D_MODEL),
                   pl.BlockSpec((N_BRANCH, BRANCH_WIDTH, D_MODEL), lambda i: (0, 0, 0), **once),
                   pl.BlockSpec((D_MODEL, D_MODEL), lambda i: (0, 0), **once),
                   pl.BlockSpec((1, D_MODEL), lambda i: (0, 0))])
    if final:
        out_shape = [jax.ShapeDtypeStruct((s, D_MODEL), F32)]
    else:
        out_shape = [jax.ShapeDtypeStruct((s, D_MODEL), F32), jax.ShapeDtypeStruct((s, D_MODEL), BF16)]
    return pl.pallas_call(
        functools.partial(_merge_kernel, final=final),
        out_shape=out_shape,
        grid=(s // tm,),
        in_specs=in_specs,
        out_specs=[row(D_MODEL)] * len(out_shape),
        compiler_params=_params(("arbitrary",)),
        name="merge",
    )(*ys, p_arr, gates, x, wb, wo, g_next.reshape(1, D_MODEL))


@functools.lru_cache(maxsize=None)
def _rope_tables(seq_len):
    t = np.arange(seq_len)

    def angles(pos, dim):
        inv_freq = np.float32(ROPE_THETA) ** (-np.arange(dim // 2, dtype=np.float32) / np.float32(dim // 2))
        return pos.astype(np.float32)[:, None] * inv_freq.astype(np.float32)[None, :]

    a1 = angles(t, HEAD_DIM)
    cos1 = np.concatenate([np.cos(a1), np.cos(a1)], axis=1)
    sin1 = np.concatenate([-np.sin(a1), np.sin(a1)], axis=1)
    ar = angles(t // GRID_W, HEAD_DIM // 2)
    ac = angles(t % GRID_W, HEAD_DIM // 2)
    cos2 = np.concatenate([np.cos(ar), np.cos(ac), np.cos(ar), np.cos(ac)], axis=1)
    sin2 = np.concatenate([-np.sin(ar), -np.sin(ac), np.sin(ar), np.sin(ac)], axis=1)
    assert all(a.dtype == np.float32 for a in (cos1, sin1, cos2, sin2))
    return (cos1, sin1), (cos2, sin2)


def _axial_reorder(a):
    lead = a.shape[:-1]
    return a.reshape(*lead, -1, 2, 2, HEAD_DIM // 4).swapaxes(-3, -2).reshape(*lead, -1)


def _gather_cols(w, seg):
    cols = np.concatenate([np.arange(_OFF[n], _OFF[n] + _WID[n]) for n in seg])
    runs = np.split(cols, np.nonzero(np.diff(cols) != 1)[0] + 1)
    return jnp.concatenate([w[:, r[0]:r[-1] + 1] for r in runs], axis=1).astype(BF16)


def _col_scale(seg, scaled):
    return jnp.asarray(np.concatenate(
        [np.full((_WID[n],), scaled.get(n, 1.0), np.float32) for n in seg]).reshape(1, -1))


def _layer(x, h, w_in_all, layer, norm_gain, gate_b, a_sink, b_rpb, d_q_norm, d_k_norm, w_branch, w_out, g_next,
           tables, final):
    s = x.shape[0]
    (cos1, sin1), (cos2, sin2) = tables
    tm = 1024
    w_in = w_in_all[layer, :, :_OFF["gates"]]

    gains = _axial_reorder(jnp.concatenate([jnp.tile(d_q_norm, _WID["dq"] // HEAD_DIM),
                                            jnp.tile(d_k_norm, _WID["dk"] // HEAD_DIM)]))
    n_arr = _proj(x if h is None else h, _axial_reorder(_gather_cols(w_in, NORM_SEG)), "norm_rope",
                  [_col_scale(NORM_SEG, {"dq": SCALE}), gains.reshape(1, -1)],
                  [cos2, sin2], tm=tm, tn=NORM_W, name="proj_norm_rope", norm_gain=norm_gain if h is None else None)
    if h is None:
        n_arr, h = n_arr
    r_arr = _proj(h, _gather_cols(w_in, ROPE_SEG), "rope", [_col_scale(ROPE_SEG, {"aq": SCALE, "cq": SCALE})],
                  [cos1, sin1], tm=2 * tm, tn=ROPE_W // 2, name="proj_rope")
    p_arr = _proj(h, _gather_cols(w_in, PLAIN_SEG), "scale", [_col_scale(PLAIN_SEG, {"bq": SCALE})], [],
                  tm=2 * tm, tn=PLAIN_W // 3, name="proj_plain")
    gates = _proj_gates(h, w_in_all, layer, gate_b.reshape(1, -1))

    ya = _window_attention(r_arr, p_arr, a_sink, seq_len=s)
    yb = _nbr_attention(p_arr, _nbr_bias(b_rpb), seq_len=s)
    yc = _dilated_attention(r_arr, p_arr, seq_len=s)
    yd = _dense_attention(n_arr, p_arr, seq_len=s)
    return _merge((ya, yb, yc, yd), p_arr, gates, x, w_branch.astype(BF16), w_out.astype(BF16), g_next, final=final)


def kernel(x, norm_g, w_in, gate_b, a_sink, b_rpb, d_q_norm, d_k_norm, w_branch, w_out, final_norm_g):
    b, s, d = x.shape
    depth = w_in.shape[0]
    tables = tuple(tuple(jnp.asarray(a) for a in grp) for grp in _rope_tables(s))
    outs = []
    for bi in range(b):
        xb = x.reshape(s, d) if b == 1 else x[bi]
        h = None
        for layer in range(depth):
            final = layer == depth - 1
            g_next = final_norm_g if final else norm_g[layer + 1]
            res = _layer(xb, h, w_in, layer, norm_g[layer], gate_b[layer], a_sink[layer], b_rpb[layer], d_q_norm[layer],
                         d_k_norm[layer], w_branch[layer], w_out[layer], g_next, tables, final)
            if final:
                xb = res[0]
            else:
                xb, h = res
        outs.append(xb)
    return outs[0].reshape(1, s, d) if b == 1 else jnp.stack(outs, axis=0)
```

```python
import functools
import math

import numpy as np
import jax
import jax.numpy as jnp
from jax import lax
from jax.experimental import pallas as pl
from jax.experimental.pallas import tpu as pltpu

F32 = jnp.float32
BF16 = jnp.bfloat16

D_MODEL = 2048
HEAD_DIM = 128
BRANCH_WIDTH = 512
N_BRANCH = 4
A_HALF_WINDOW = 128
B_WIN_ROWS = 8
B_WIN_COLS = 16
C_PATTERNS = ((128, 1), (512, 4), (2048, 16))
GRID_W = 64
ROPE_THETA = 10000.0
NORM_EPS = 1e-6
LOG2E = math.log2(math.e)
SCALE = HEAD_DIM ** -0.5 * LOG2E
NEG_BIG = -1e30

VMEM_LIMIT = 56 * 1024 * 1024

_SPLITS = (512, 256, 256, 512, 512, 512, 512, 512, 1536, 512, 512, 512, 512, 256, 256, 512, 8192)
_NAMES = ("aq", "ak", "av", "az", "bq", "bk", "bv", "bz", "cq", "ck", "cv", "cz", "dq", "dk", "dv", "dz", "gates")
_OFF = dict(zip(_NAMES, np.concatenate([[0], np.cumsum(_SPLITS)[:-1]]).tolist()))
_WID = dict(zip(_NAMES, _SPLITS))

ROPE_SEG = ("aq", "ak", "cq", "ck")
NORM_SEG = ("dq", "dk")
PLAIN_SEG = ("az", "bz", "cz", "dz", "av", "dv", "bq", "bk", "bv", "cv")


def _seg_offsets(seg):
    offs, o = {}, 0
    for n in seg:
        offs[n] = o
        o += _WID[n]
    return offs, o


ROPE_OFF, ROPE_W = _seg_offsets(ROPE_SEG)
NORM_OFF, NORM_W = _seg_offsets(NORM_SEG)
PLAIN_OFF, PLAIN_W = _seg_offsets(PLAIN_SEG)


def _params(sem):
    return pltpu.CompilerParams(dimension_semantics=sem, vmem_limit_bytes=VMEM_LIMIT)


def _sigmoid(v):
    return 1.0 / (1.0 + jnp.exp(-v))


def _weighted_values(p, v):
    v1 = jnp.concatenate([v, jnp.ones_like(v)], axis=1)
    r = jnp.dot(p.astype(BF16), v1, preferred_element_type=F32)
    return r[:, :HEAD_DIM], r[:, HEAD_DIM:]


PROJ_RB = 256


def _rope_half(xh, cos, sin_signed):
    return xh * cos + pltpu.roll(xh, HEAD_DIM // 2, 1) * sin_signed


def _proj_kernel(*refs, mode, norm_input):
    if norm_input:
        g_in_ref, refs, h_out_ref = refs[0], refs[1:-1], refs[-1]
    h_ref, w_ref = refs[0], refs[1]
    o_ref = refs[-1]
    tm, tn = o_ref.shape
    for r in range(tm // PROJ_RB):
        rows = pl.ds(r * PROJ_RB, PROJ_RB)
        if norm_input:
            x = h_ref[rows, :]
            h = (x * lax.rsqrt(jnp.mean(x * x, axis=-1, keepdims=True) + NORM_EPS) * g_in_ref[...]).astype(BF16)
            h_out_ref[rows, :] = h
        else:
            h = h_ref[rows, :]
        acc = jnp.dot(h, w_ref[...], preferred_element_type=F32)
        if mode == "scale":
            o_ref[rows, :] = (acc * refs[2][...]).astype(o_ref.dtype)
        elif mode == "rope":
            cs_ref, cos_ref, sin_ref = refs[2:5]
            cos, sin = cos_ref[rows, :], sin_ref[rows, :]
            for c in range(tn // HEAD_DIM):
                sl = slice(c * HEAD_DIM, (c + 1) * HEAD_DIM)
                o_ref[rows, sl] = (_rope_half(acc[:, sl], cos, sin) * cs_ref[:, sl]).astype(o_ref.dtype)
        elif mode == "norm_rope":
            cs_ref, g_ref, cos_ref, sin_ref = refs[2:6]
            cos, sin = cos_ref[rows, :], sin_ref[rows, :]
            for c in range(tn // HEAD_DIM):
                sl = slice(c * HEAD_DIM, (c + 1) * HEAD_DIM)
                xh = acc[:, sl]
                y = xh * lax.rsqrt(jnp.mean(xh * xh, axis=-1, keepdims=True) + NORM_EPS) * g_ref[:, sl]
                o_ref[rows, sl] = (_rope_half(y, cos, sin) * cs_ref[:, sl]).astype(o_ref.dtype)
        else:
            raise ValueError(mode)


def _proj(h, w, mode, row_vecs, tables, tm, tn, name, norm_gain=None):
    s, d = h.shape
    n = w.shape[1]
    tm = min(tm, s)
    in_specs = [pl.BlockSpec((tm, d), lambda i, j: (i, 0)), pl.BlockSpec((d, tn), lambda i, j: (0, j))]
    in_specs += [pl.BlockSpec((1, tn), lambda i, j: (0, j)) for _ in row_vecs]
    in_specs += [pl.BlockSpec((tm, HEAD_DIM), lambda i, j: (i, 0)) for _ in tables]
    out_shape = [jax.ShapeDtypeStruct((s, n), BF16)]
    out_specs = [pl.BlockSpec((tm, tn), lambda i, j: (i, j))]
    args = [h, w, *row_vecs, *tables]
    if norm_gain is not None:
        assert n == tn
        in_specs = [pl.BlockSpec((1, d), lambda i, j: (0, 0))] + in_specs
        args = [norm_gain.reshape(1, d)] + args
        out_shape.append(jax.ShapeDtypeStruct((s, d), BF16))
        out_specs.append(pl.BlockSpec((tm, d), lambda i, j: (i, 0)))
    outs = pl.pallas_call(
        functools.partial(_proj_kernel, mode=mode, norm_input=norm_gain is not None),
        out_shape=out_shape,
        grid=(s // tm, n // tn),
        in_specs=in_specs,
        out_specs=out_specs,
        compiler_params=_params(("arbitrary", "arbitrary")),
        name=name,
    )(*args)
    return outs[0] if norm_gain is None else outs


def _gates_kernel(h_ref, w_ref, b_ref, o_ref, wb_ref):
    @pl.when(pl.program_id(1) == 0)
    def _():
        wb_ref[...] = w_ref[...].astype(BF16)

    for r in range(o_ref.shape[0] // PROJ_RB):
        rows = pl.ds(r * PROJ_RB, PROJ_RB)
        acc = jnp.dot(h_ref[rows, :], wb_ref[...], preferred_element_type=F32)
        o_ref[rows, :] = _sigmoid(acc + b_ref[...]).astype(o_ref.dtype)


def _proj_gates(h, w_in_all, layer, bias, tm=2048, tn=1024):
    s, d = h.shape
    tm = min(tm, s)
    n = _WID["gates"]
    first = _OFF["gates"] // tn
    return pl.pallas_call(
        _gates_kernel,
        out_shape=jax.ShapeDtypeStruct((s, n), BF16),
        grid=(n // tn, s // tm),
        in_specs=[pl.BlockSpec((tm, d), lambda j, i: (i, 0)),
                  pl.BlockSpec((None, d, tn), lambda j, i: (layer, 0, first + j)),
                  pl.BlockSpec((1, tn), lambda j, i: (0, j))],
        out_specs=pl.BlockSpec((tm, tn), lambda j, i: (i, j)),
        scratch_shapes=[pltpu.VMEM((d, tn), BF16)],
        compiler_params=_params(("arbitrary", "arbitrary")),
        name="proj_gates",
    )(h, w_in_all, bias)


WIN_BQ = 128


def _window_consts(sink_ref, kv_head):
    bq, groups = WIN_BQ, 2
    span, rows = bq + 2 * A_HALF_WINDOW, groups * bq
    row = lax.broadcasted_iota(jnp.int32, (rows, span), 0)
    col_minus_row = lax.broadcasted_iota(jnp.int32, (rows, span), 1) - jnp.where(row >= bq, row - bq, row)
    r1 = lax.broadcasted_iota(jnp.int32, (rows, 1), 0)
    sink = jnp.where(r1 >= bq, sink_ref[kv_head * groups + 1], sink_ref[kv_head * groups]) * LOG2E
    return col_minus_row, sink


def _window_block(consts, q_ref, k_ref, v_ref, o_ref, q0, qoff, seq_len):
    col_minus_row, sink = consts
    bq, hw = WIN_BQ, A_HALF_WINDOW
    span = bq + 2 * hw
    start = pl.multiple_of(jnp.clip(q0 - hw, 0, seq_len - span), bq)
    k = k_ref[pl.ds(start, span), :]
    v = v_ref[pl.ds(start, span), :]
    qb = q_ref[pl.ds(qoff, bq), :]
    q2 = jnp.concatenate([qb[:, :HEAD_DIM], qb[:, HEAD_DIM:]], axis=0)
    s = lax.dot_general(q2, k, (((1,), (1,)), ((), ())), preferred_element_type=F32)
    s = jnp.where(jnp.abs(col_minus_row + (start - q0)) <= hw, s, NEG_BIG)
    m = jnp.maximum(jnp.max(s, axis=1, keepdims=True), sink)
    p = jnp.exp2(s - m)
    pv, psum = _weighted_values(p, v)
    o = pv / (psum + jnp.exp2(sink - m))
    o_ref[pl.ds(qoff, bq), :HEAD_DIM] = o[:bq].astype(o_ref.dtype)
    o_ref[pl.ds(qoff, bq), HEAD_DIM:] = o[bq:].astype(o_ref.dtype)


def _window_kernel(sink_ref, q_ref, k_ref, v_ref, o_ref, *, seq_len, nsub):
    consts = _window_consts(sink_ref, pl.program_id(0))
    t0 = pl.program_id(1) * nsub * WIN_BQ

    def body(b, carry):
        qoff = pl.multiple_of(b * WIN_BQ, WIN_BQ)
        _window_block(consts, q_ref, k_ref, v_ref, o_ref, t0 + qoff, qoff, seq_len)
        return carry

    lax.fori_loop(0, nsub, body, 0, unroll=True)


def _window_attention(r_arr, p_arr, sink, *, seq_len, max_step=4096):
    nsub = min(max_step, seq_len) // WIN_BQ
    tq = nsub * WIN_BQ
    qc, kc, vc = ROPE_OFF["aq"] // (2 * HEAD_DIM), ROPE_OFF["ak"] // HEAD_DIM, PLAIN_OFF["av"] // HEAD_DIM
    return pl.pallas_call(
        functools.partial(_window_kernel, seq_len=seq_len, nsub=nsub),
        out_shape=jax.ShapeDtypeStruct((seq_len, BRANCH_WIDTH), BF16),
        grid=(2, seq_len // tq),
        in_specs=[pl.BlockSpec(memory_space=pltpu.SMEM),
                  pl.BlockSpec((tq, 2 * HEAD_DIM), lambda c, i: (i, qc + c)),
                  pl.BlockSpec((seq_len, HEAD_DIM), lambda c, i: (0, kc + c)),
                  pl.BlockSpec((seq_len, HEAD_DIM), lambda c, i: (0, vc + c))],
        out_specs=pl.BlockSpec((tq, 2 * HEAD_DIM), lambda c, i: (i, c)),
        compiler_params=_params(("arbitrary", "arbitrary")),
        name="window_attention",
    )(sink, r_arr, r_arr, p_arr)


DIL_BQ = 128
DIL_HALF = 64
DIL_HALO = DIL_HALF * max(d for _, d in C_PATTERNS)


DIL_BLK = 128


def _residue_major(dil):
    per = DIL_BLK // dil
    p = np.zeros((DIL_BLK, DIL_BLK), np.float32)
    for r in range(dil):
        for i in range(per):
            p[r * per + i, i * dil + r] = 1.0
    return p


def _to_residue_major(src_ref, dst_ref, perm, dil, first_blk, n_blk):
    per = DIL_BLK // dil
    for b in range(first_blk, first_blk + n_blk, 2):
        pair = jnp.concatenate([src_ref[b * DIL_BLK:(b + 1) * DIL_BLK, :],
                                src_ref[(b + 1) * DIL_BLK:(b + 2) * DIL_BLK, :]], axis=1)
        res = jnp.dot(perm, pair, preferred_element_type=F32)
        dst_ref[b * dil:(b + 1) * dil] = res[:, :HEAD_DIM].reshape(dil, per, HEAD_DIM)
        dst_ref[(b + 1) * dil:(b + 2) * dil] = res[:, HEAD_DIM:].reshape(dil, per, HEAD_DIM)


def _dilated_kernel(q0_ref, q1_ref, q2_ref, kp_ref, kc_ref, kn_ref, vp_ref, vc_ref, vn_ref, perm1_ref, perm2_ref,
                    o_ref, kw_ref, vw_ref, qd1_ref, kd1_ref, vd1_ref, qd2_ref, kd2_ref, vd2_ref, og_ref, lg_ref,
                    *, seq_len, tq):
    t0 = pl.program_id(1) * tq
    halo, bq, half = DIL_HALO, DIL_BQ, DIL_HALF
    span = bq + 2 * half
    for w_ref, prev, cur, nxt in ((kw_ref, kp_ref, kc_ref, kn_ref), (vw_ref, vp_ref, vc_ref, vn_ref)):
        w_ref[:halo] = prev[...]
        w_ref[halo:halo + tq] = cur[...]
        w_ref[halo + tq:] = nxt[...]
    residue_major = {}
    for (_, dil), q_ref, perm_ref, qd_ref, kd_ref, vd_ref in (
            (C_PATTERNS[1], q1_ref, perm1_ref, qd1_ref, kd1_ref, vd1_ref),
            (C_PATTERNS[2], q2_ref, perm2_ref, qd2_ref, kd2_ref, vd2_ref)):
        perm = perm_ref[...]
        first_blk = (halo - half * dil) // DIL_BLK
        n_blk = tq // DIL_BLK + dil
        _to_residue_major(q_ref, qd_ref, perm, dil, 0, tq // DIL_BLK)
        _to_residue_major(kw_ref, kd_ref, perm, dil, first_blk, n_blk)
        _to_residue_major(vw_ref, vd_ref, perm, dil, first_blk, n_blk)
        residue_major[dil] = (qd_ref, kd_ref, vd_ref)

    ii = lax.broadcasted_iota(jnp.int32, (bq, span), 0)
    jj = lax.broadcasted_iota(jnp.int32, (bq, span), 1)
    in_band = jnp.abs(jj - half - ii) <= half

    for g, (_, dil) in enumerate(C_PATTERNS):
        nb = tq // (bq * dil)

        def body(it, carry, g=g, dil=dil, nb=nb):
            res, sub = it // nb, it % nb
            q_row = res + dil * bq * sub
            w_row = halo + q_row - half * dil
            if dil == 1:
                q = q0_ref[pl.ds(pl.multiple_of(q_row, bq), bq), :]
                k = kw_ref[pl.ds(pl.multiple_of(w_row, half), span), :]
                v = vw_ref[pl.ds(pl.multiple_of(w_row, half), span), :]
            else:
                qd_ref, kd_ref, vd_ref = residue_major[dil]
                q_blk = sub * bq * dil // DIL_BLK
                k_blk = (halo - half * dil) // DIL_BLK + q_blk
                q = qd_ref[pl.ds(q_blk * dil + res, bq * dil // DIL_BLK, stride=dil)]
                k = kd_ref[pl.ds(k_blk * dil + res, span * dil // DIL_BLK, stride=dil)]
                v = vd_ref[pl.ds(k_blk * dil + res, span * dil // DIL_BLK, stride=dil)]
                q = q.reshape(bq, HEAD_DIM).astype(BF16)
                k = k.reshape(span, HEAD_DIM).astype(BF16)
                v = v.reshape(span, HEAD_DIM).astype(BF16)
            s = lax.dot_general(q, k, (((1,), (1,)), ((), ())), preferred_element_type=F32)
            ktok = t0 + q_row - half * dil + dil * jj
            s = jnp.where(in_band & (ktok >= 0) & (ktok < seq_len), s, NEG_BIG)
            m = jnp.max(s, axis=1, keepdims=True)
            p = jnp.exp2(s - m)
            pv, psum = _weighted_values(p, v)
            og_ref[g, pl.ds(q_row, bq, stride=dil), :] = pv / psum
            lg_ref[g, pl.ds(q_row, bq, stride=dil), :] = m + jnp.log2(psum)
            return carry

        lax.fori_loop(0, tq // bq, body, 0, unroll=True)

    l0, l1, l2 = lg_ref[0], lg_ref[1], lg_ref[2]
    mx = jnp.maximum(jnp.maximum(l0, l1), l2)
    e0, e1, e2 = jnp.exp2(l0 - mx), jnp.exp2(l1 - mx), jnp.exp2(l2 - mx)
    o_ref[...] = ((e0 * og_ref[0] + e1 * og_ref[1] + e2 * og_ref[2]) / (e0 + e1 + e2)).astype(o_ref.dtype)


def _dilated_attention(r_arr, p_arr, *, seq_len, tq=4096):
    tq = min(tq, seq_len)
    halo = DIL_HALO
    ratio, n_halo = tq // halo, seq_len // halo
    qc, kc, vc = ROPE_OFF["cq"] // HEAD_DIM, ROPE_OFF["ck"] // HEAD_DIM, PLAIN_OFF["cv"] // HEAD_DIM
    n_heads = BRANCH_WIDTH // HEAD_DIM

    def window_specs(col):
        return [pl.BlockSpec((halo, HEAD_DIM), lambda h, i: (jnp.maximum(i * ratio - 1, 0), col + h)),
                pl.BlockSpec((tq, HEAD_DIM), lambda h, i: (i, col + h)),
                pl.BlockSpec((halo, HEAD_DIM), lambda h, i: (jnp.minimum((i + 1) * ratio, n_halo - 1), col + h))]

    q_specs = [pl.BlockSpec((tq, HEAD_DIM), lambda h, i, g=g: (i, qc + n_heads * g + h)) for g in range(3)]
    perm_spec = pl.BlockSpec((DIL_BLK, DIL_BLK), lambda h, i: (0, 0))
    dilations = [d for _, d in C_PATTERNS[1:]]
    assert C_PATTERNS[0][1] == 1 and all(DIL_BLK % d == 0 and (DIL_BLK // d) % 8 == 0 for d in dilations)
    window = tq + 2 * halo
    residue_scratch = []
    for d in dilations:
        residue_scratch += [pltpu.VMEM((tq // DIL_BLK * d, DIL_BLK // d, HEAD_DIM), F32),
                            pltpu.VMEM((window // DIL_BLK * d, DIL_BLK // d, HEAD_DIM), F32),
                            pltpu.VMEM((window // DIL_BLK * d, DIL_BLK // d, HEAD_DIM), F32)]
    return pl.pallas_call(
        functools.partial(_dilated_kernel, seq_len=seq_len, tq=tq),
        out_shape=jax.ShapeDtypeStruct((seq_len, BRANCH_WIDTH), BF16),
        grid=(n_heads, seq_len // tq),
        in_specs=q_specs + window_specs(kc) + window_specs(vc) + [perm_spec] * len(dilations),
        out_specs=pl.BlockSpec((tq, HEAD_DIM), lambda h, i: (i, h)),
        scratch_shapes=[pltpu.VMEM((window, HEAD_DIM), BF16), pltpu.VMEM((window, HEAD_DIM), BF16)]
        + residue_scratch + [pltpu.VMEM((3, tq, HEAD_DIM), F32), pltpu.VMEM((3, tq, HEAD_DIM), F32)],
        compiler_params=_params(("arbitrary", "arbitrary")),
        name="dilated_attention",
    )(r_arr, r_arr, r_arr, r_arr, r_arr, r_arr, p_arr, p_arr, p_arr,
      *[jnp.asarray(_residue_major(d), BF16) for d in dilations])


NBR_ROWS = 4
NBR_UNION = 12
NBR_SUB = 64


def _nbr_bias_kernel(rpb_ref, o_ref, t_ref):
    h = pl.program_id(0)
    n_dr, n_dc = 2 * B_WIN_ROWS - 1, 2 * B_WIN_COLS - 1
    qc = lax.broadcasted_iota(jnp.int32, (GRID_W, GRID_W), 0)
    kc = lax.broadcasted_iota(jnp.int32, (GRID_W, GRID_W), 1)
    col_start = jnp.clip(qc - B_WIN_COLS // 2, 0, GRID_W - B_WIN_COLS)
    col_ok = (kc >= col_start) & (kc < col_start + B_WIN_COLS)
    dc = kc - qc + (B_WIN_COLS - 1)
    for dr in range(n_dr):
        t = jnp.full((GRID_W, GRID_W), NEG_BIG, F32)
        for d in range(n_dc):
            t = jnp.where(dc == d, rpb_ref[(h * n_dr + dr) * n_dc + d] * LOG2E, t)
        t_ref[dr] = jnp.where(col_ok, t, NEG_BIG)
    o_ref[...] = jnp.full(o_ref.shape, NEG_BIG, F32)
    half = B_WIN_ROWS // 2
    for case in range(3):
        for a in range(NBR_ROWS):
            win, off = ((0, a), (a, half), (NBR_UNION - B_WIN_ROWS, half + a))[case]
            for j in range(B_WIN_ROWS):
                o_ref[case, a * GRID_W:(a + 1) * GRID_W, (win + j) * GRID_W:(win + j + 1) * GRID_W] = (
                    t_ref[j - off + B_WIN_ROWS - 1])


def _nbr_bias(rpb):
    n_heads = rpb.shape[0]
    shape = (3, NBR_ROWS * GRID_W, NBR_UNION * GRID_W)
    return pl.pallas_call(
        _nbr_bias_kernel,
        out_shape=jax.ShapeDtypeStruct((n_heads,) + shape, F32),
        grid=(n_heads,),
        in_specs=[pl.BlockSpec(memory_space=pltpu.SMEM)],
        out_specs=pl.BlockSpec((None,) + shape, lambda h: (h, 0, 0, 0)),
        scratch_shapes=[pltpu.VMEM((2 * B_WIN_ROWS - 1, GRID_W, GRID_W), F32)],
        compiler_params=_params(("arbitrary",)),
        name="nbr_bias",
    )(rpb.reshape(-1))


def _nbr_kernel(q_ref, k_ref, v_ref, b_first_ref, b_mid_ref, b_last_ref, o_ref, *, n_rows, n_sub):
    step = pl.program_id(1)
    nq, nk = NBR_ROWS * GRID_W, NBR_UNION * GRID_W
    for sb in range(n_sub):
        bias_ref = b_first_ref if sb == 0 else (b_last_ref if sb == n_sub - 1 else b_mid_ref)
        r0 = (step * n_sub + sb) * NBR_ROWS
        u0 = jnp.clip(r0 - B_WIN_ROWS // 2, 0, n_rows - NBR_UNION)
        kstart = pl.multiple_of(u0 * GRID_W, GRID_W)
        k = k_ref[pl.ds(kstart, nk), :]
        v = v_ref[pl.ds(kstart, nk), :]
        q = q_ref[sb * nq:(sb + 1) * nq, :]
        s = lax.dot_general(q, k, (((1,), (1,)), ((), ())), preferred_element_type=F32) + bias_ref[...]
        m = jnp.max(s, axis=1, keepdims=True)
        p = jnp.exp2(s - m)
        pv, psum = _weighted_values(p, v)
        o_ref[sb * nq:(sb + 1) * nq, :] = (pv / psum).astype(o_ref.dtype)


def _nbr_attention(p_arr, bias, *, seq_len):
    n_rows = seq_len // GRID_W
    n_sub = min(NBR_SUB, n_rows // NBR_ROWS)
    tq = n_sub * NBR_ROWS * GRID_W
    n_steps = seq_len // tq
    qc, kc, vc = (PLAIN_OFF[n] // HEAD_DIM for n in ("bq", "bk", "bv"))
    tile = (None, None, NBR_ROWS * GRID_W, NBR_UNION * GRID_W)
    return pl.pallas_call(
        functools.partial(_nbr_kernel, n_rows=n_rows, n_sub=n_sub),
        out_shape=jax.ShapeDtypeStruct((seq_len, BRANCH_WIDTH), BF16),
        grid=(BRANCH_WIDTH // HEAD_DIM, n_steps),
        in_specs=[pl.BlockSpec((tq, HEAD_DIM), lambda h, i: (i, qc + h)),
                  pl.BlockSpec((seq_len, HEAD_DIM), lambda h, i: (0, kc + h)),
                  pl.BlockSpec((seq_len, HEAD_DIM), lambda h, i: (0, vc + h)),
                  pl.BlockSpec(tile, lambda h, i: (h, jnp.where(i == 0, 0, 1), 0, 0)),
                  pl.BlockSpec(tile, lambda h, i: (h, 1, 0, 0)),
                  pl.BlockSpec(tile, lambda h, i: (h, jnp.where(i == n_steps - 1, 2, 1), 0, 0))],
        out_specs=pl.BlockSpec((tq, HEAD_DIM), lambda h, i: (i, h)),
        compiler_params=_params(("arbitrary", "arbitrary")),
        name="nbr_attention",
    )(p_arr, p_arr, p_arr, bias, bias, bias)


DENSE_RB = 256


def _dense_kernel(q_ref, k_ref, v_ref, o_ref, q2_ref, v2_ref, m_ref, acc_ref, s0_ref, *, seq_len, tq, tk):
    @pl.when(pl.program_id(1) == 0)
    def _():
        v2_ref[:, :HEAD_DIM] = v_ref[...]
        v2_ref[:, HEAD_DIM:] = jnp.ones((seq_len, HEAD_DIM), BF16)

    q2_ref[:tq] = q_ref[:, :HEAD_DIM]
    q2_ref[tq:] = q_ref[:, HEAD_DIM:]
    m_ref[...] = jnp.full(m_ref.shape, NEG_BIG, F32)
    acc_ref[...] = jnp.zeros(acc_ref.shape, F32)
    n_chunks = seq_len // tk

    def logits(r, k):
        return lax.dot_general(q2_ref[pl.ds(r * DENSE_RB, DENSE_RB), :], k, (((1,), (1,)), ((), ())),
                               preferred_element_type=F32)

    s0_ref[...] = logits(0, k_ref[pl.ds(0, tk), :])

    def body(j, carry):
        koff = pl.multiple_of(j * tk, tk)
        k = k_ref[pl.ds(koff, tk), :]
        v2 = v2_ref[pl.ds(koff, tk), :]
        for r in range((2 * tq) // DENSE_RB):
            rows = pl.ds(r * DENSE_RB, DENSE_RB)
            s = s0_ref[...] if r == 0 else logits(r, k)
            m_prev = m_ref[rows, :]
            m_new = jnp.maximum(m_prev, jnp.max(s, axis=1, keepdims=True))
            alpha = jnp.exp2(m_prev - m_new)
            p = jnp.exp2(s - m_new).astype(BF16)
            acc_ref[rows, :] = acc_ref[rows, :] * alpha + jnp.dot(p, v2, preferred_element_type=F32)
            m_ref[rows, :] = m_new
        next_off = pl.multiple_of(jnp.minimum(j + 1, n_chunks - 1) * tk, tk)
        s0_ref[...] = logits(0, k_ref[pl.ds(next_off, tk), :])
        return carry

    lax.fori_loop(0, n_chunks, body, 0)
    acc = acc_ref[...]
    o = acc[:, :HEAD_DIM] / acc[:, HEAD_DIM:]
    o_ref[:, :HEAD_DIM] = o[:tq].astype(o_ref.dtype)
    o_ref[:, HEAD_DIM:] = o[tq:].astype(o_ref.dtype)


def _dense_attention(n_arr, p_arr, *, seq_len, tq=2048, tk=2048):
    tq, tk = min(tq, seq_len), min(tk, seq_len)
    kc = NORM_OFF["dk"] // HEAD_DIM
    vc = PLAIN_OFF["dv"] // HEAD_DIM
    return pl.pallas_call(
        functools.partial(_dense_kernel, seq_len=seq_len, tq=tq, tk=tk),
        out_shape=jax.ShapeDtypeStruct((seq_len, BRANCH_WIDTH), BF16),
        grid=(2, seq_len // tq),
        in_specs=[pl.BlockSpec((tq, 2 * HEAD_DIM), lambda g, i: (i, g)),
                  pl.BlockSpec((seq_len, HEAD_DIM), lambda g, i: (0, kc + g), pipeline_mode=pl.Buffered(1)),
                  pl.BlockSpec((seq_len, HEAD_DIM), lambda g, i: (0, vc + g), pipeline_mode=pl.Buffered(1))],
        out_specs=pl.BlockSpec((tq, 2 * HEAD_DIM), lambda g, i: (i, g)),
        scratch_shapes=[pltpu.VMEM((2 * tq, HEAD_DIM), BF16), pltpu.VMEM((seq_len, 2 * HEAD_DIM), BF16),
                        pltpu.VMEM((2 * tq, 1), F32), pltpu.VMEM((2 * tq, 2 * HEAD_DIM), F32),
                        pltpu.VMEM((DENSE_RB, tk), F32)],
        compiler_params=_params(("arbitrary", "arbitrary")),
        name="dense_attention",
    )(n_arr, n_arr, p_arr)


def _merge_kernel(ya_ref, yb_ref, yc_ref, yd_ref, z_ref, gate_ref, x_ref, wb_ref, wo_ref, g_ref, *out_refs, final):
    merged = None
    for n, y_ref in enumerate((ya_ref, yb_ref, yc_ref, yd_ref)):
        z = z_ref[:, n * BRANCH_WIDTH:(n + 1) * BRANCH_WIDTH].astype(F32)
        t = (y_ref[...].astype(F32) * (z * _sigmoid(z))).astype(BF16)
        br = jnp.dot(t, wb_ref[n], preferred_element_type=F32)
        term = gate_ref[:, n * D_MODEL:(n + 1) * D_MODEL].astype(F32) * br
        merged = term if merged is None else merged + term
    x_new = x_ref[...] + jnp.dot(merged.astype(BF16), wo_ref[...], preferred_element_type=F32)
    normed = x_new * lax.rsqrt(jnp.mean(x_new * x_new, axis=-1, keepdims=True) + NORM_EPS) * g_ref[...]
    if final:
        out_refs[0][...] = normed
    else:
        out_refs[0][...] = x_new
        out_refs[1][...] = normed.astype(BF16)


def _merge(ys, p_arr, gates, x, wb, wo, g_next, *, final, tm=256):
    s = x.shape[0]
    row = lambda w: pl.BlockSpec((tm, w), lambda i: (i, 0))
    once = dict(pipeline_mode=pl.Buffered(1))
    in_specs = ([row(BRANCH_WIDTH)] * N_BRANCH
                + [row(N_BRANCH * BRANCH_WIDTH), row(N_BRANCH * D_MODEL), row(D_MODEL),
                   pl.BlockSpec((N_BRANCH, BRANCH_WIDTH, D_MODEL), lambda i: (0, 0, 0), **once),
                   pl.BlockSpec((D_MODEL, D_MODEL), lambda i: (0, 0), **once),
                   pl.BlockSpec((1, D_MODEL), lambda i: (0, 0))])
    if final:
        out_shape = [jax.ShapeDtypeStruct((s, D_MODEL), F32)]
    else:
        out_shape = [jax.ShapeDtypeStruct((s, D_MODEL), F32), jax.ShapeDtypeStruct((s, D_MODEL), BF16)]
    return pl.pallas_call(
        functools.partial(_merge_kernel, final=final),
        out_shape=out_shape,
        grid=(s // tm,),
        in_specs=in_specs,
        out_specs=[row(D_MODEL)] * len(out_shape),
        compiler_params=_params(("arbitrary",)),
        name="merge",
    )(*ys, p_arr, gates, x, wb, wo, g_next.reshape(1, D_MODEL))


@functools.lru_cache(maxsize=None)
def _rope_tables(seq_len):
    t = np.arange(seq_len)

    def angles(pos, dim):
        inv_freq = np.float32(ROPE_THETA) ** (-np.arange(dim // 2, dtype=np.float32) / np.float32(dim // 2))
        return pos.astype(np.float32)[:, None] * inv_freq.astype(np.float32)[None, :]

    a1 = angles(t, HEAD_DIM)
    cos1 = np.concatenate([np.cos(a1), np.cos(a1)], axis=1)
    sin1 = np.concatenate([-np.sin(a1), np.sin(a1)], axis=1)
    ar = angles(t // GRID_W, HEAD_DIM // 2)
    ac = angles(t % GRID_W, HEAD_DIM // 2)
    cos2 = np.concatenate([np.cos(ar), np.cos(ac), np.cos(ar), np.cos(ac)], axis=1)
    sin2 = np.concatenate([-np.sin(ar), -np.sin(ac), np.sin(ar), np.sin(ac)], axis=1)
    assert all(a.dtype == np.float32 for a in (cos1, sin1, cos2, sin2))
    return (cos1, sin1), (cos2, sin2)


def _axial_reorder(a):
    lead = a.shape[:-1]
    return a.reshape(*lead, -1, 2, 2, HEAD_DIM // 4).swapaxes(-3, -2).reshape(*lead, -1)


def _gather_cols(w, seg):
    cols = np.concatenate([np.arange(_OFF[n], _OFF[n] + _WID[n]) for n in seg])
    runs = np.split(cols, np.nonzero(np.diff(cols) != 1)[0] + 1)
    return jnp.concatenate([w[:, r[0]:r[-1] + 1] for r in runs], axis=1).astype(BF16)


def _col_scale(seg, scaled):
    return jnp.asarray(np.concatenate(
        [np.full((_WID[n],), scaled.get(n, 1.0), np.float32) for n in seg]).reshape(1, -1))


def _layer(x, h, w_in_all, layer, norm_gain, gate_b, a_sink, b_rpb, d_q_norm, d_k_norm, w_branch, w_out, g_next,
           tables, final):
    s = x.shape[0]
    (cos1, sin1), (cos2, sin2) = tables
    tm = 1024
    w_in = w_in_all[layer, :, :_OFF["gates"]]

    gains = _axial_reorder(jnp.concatenate([jnp.tile(d_q_norm, _WID["dq"] // HEAD_DIM),
                                            jnp.tile(d_k_norm, _WID["dk"] // HEAD_DIM)]))
    n_arr = _proj(x if h is None else h, _axial_reorder(_gather_cols(w_in, NORM_SEG)), "norm_rope",
                  [_col_scale(NORM_SEG, {"dq": SCALE}), gains.reshape(1, -1)],
                  [cos2, sin2], tm=tm, tn=NORM_W, name="proj_norm_rope", norm_gain=norm_gain if h is None else None)
    if h is None:
        n_arr, h = n_arr
    r_arr = _proj(h, _gather_cols(w_in, ROPE_SEG), "rope", [_col_scale(ROPE_SEG, {"aq": SCALE, "cq": SCALE})],
                  [cos1, sin1], tm=2 * tm, tn=ROPE_W // 2, name="proj_rope")
    p_arr = _proj(h, _gather_cols(w_in, PLAIN_SEG), "scale", [_col_scale(PLAIN_SEG, {"bq": SCALE})], [],
                  tm=2 * tm, tn=PLAIN_W // 3, name="proj_plain")
    gates = _proj_gates(h, w_in_all, layer, gate_b.reshape(1, -1))

    ya = _window_attention(r_arr, p_arr, a_sink, seq_len=s)
    yb = _nbr_attention(p_arr, _nbr_bias(b_rpb), seq_len=s)
    yc = _dilated_attention(r_arr, p_arr, seq_len=s)
    yd = _dense_attention(n_arr, p_arr, seq_len=s)
    return _merge((ya, yb, yc, yd), p_arr, gates, x, w_branch.astype(BF16), w_out.astype(BF16), g_next, final=final)


def kernel(x, norm_g, w_in, gate_b, a_sink, b_rpb, d_q_norm, d_k_norm, w_branch, w_out, final_norm_g):
    b, s, d = x.shape
    depth = w_in.shape[0]
    tables = tuple(tuple(jnp.asarray(a) for a in grp) for grp in _rope_tables(s))
    outs = []
    for bi in range(b):
        xb = x.reshape(s, d) if b == 1 else x[bi]
        h = None
        for layer in range(depth):
            final = layer == depth - 1
            g_next = final_norm_g if final else norm_g[layer + 1]
            res = _layer(xb, h, w_in, layer, norm_g[layer], gate_b[layer], a_sink[layer], b_rpb[layer], d_q_norm[layer],
                         d_k_norm[layer], w_branch[layer], w_out[layer], g_next, tables, final)
            if final:
                xb = res[0]
            else:
                xb, h = res
        outs.append(xb)
    return outs[0].reshape(1, s, d) if b == 1 else jnp.stack(outs, axis=0)
```

```python
import functools
import math

import numpy as np
import jax
import jax.numpy as jnp
from jax import lax
from jax.experimental import pallas as pl
from jax.experimental.pallas import tpu as pltpu

F32 = jnp.float32
BF16 = jnp.bfloat16

D_MODEL = 2048
HEAD_DIM = 128
BRANCH_WIDTH = 512
N_BRANCH = 4
A_HALF_WINDOW = 128
B_WIN_ROWS = 8
B_WIN_COLS = 16
C_PATTERNS = ((128, 1), (512, 4), (2048, 16))
GRID_W = 64
ROPE_THETA = 10000.0
NORM_EPS = 1e-6
LOG2E = math.log2(math.e)
SCALE = HEAD_DIM ** -0.5 * LOG2E
NEG_BIG = -1e30

VMEM_LIMIT = 56 * 1024 * 1024

_SPLITS = (512, 256, 256, 512, 512, 512, 512, 512, 1536, 512, 512, 512, 512, 256, 256, 512, 8192)
_NAMES = ("aq", "ak", "av", "az", "bq", "bk", "bv", "bz", "cq", "ck", "cv", "cz", "dq", "dk", "dv", "dz", "gates")
_OFF = dict(zip(_NAMES, np.concatenate([[0], np.cumsum(_SPLITS)[:-1]]).tolist()))
_WID = dict(zip(_NAMES, _SPLITS))

ROPE_SEG = ("aq", "ak", "cq", "ck")
NORM_SEG = ("dq", "dk")
PLAIN_SEG = ("az", "bz", "cz", "dz", "av", "dv", "bq", "bk", "bv", "cv")


def _seg_offsets(seg):
    offs, o = {}, 0
    for n in seg:
        offs[n] = o
        o += _WID[n]
    return offs, o


ROPE_OFF, ROPE_W = _seg_offsets(ROPE_SEG)
NORM_OFF, NORM_W = _seg_offsets(NORM_SEG)
PLAIN_OFF, PLAIN_W = _seg_offsets(PLAIN_SEG)


def _params(sem):
    return pltpu.CompilerParams(dimension_semantics=sem, vmem_limit_bytes=VMEM_LIMIT)


def _sigmoid(v):
    return 1.0 / (1.0 + jnp.exp(-v))


def _weighted_values(p, v):
    v1 = jnp.concatenate([v, jnp.ones_like(v)], axis=1)
    r = jnp.dot(p.astype(BF16), v1, preferred_element_type=F32)
    return r[:, :HEAD_DIM], r[:, HEAD_DIM:]


PROJ_RB = 256


def _rope_half(xh, cos, sin_signed):
    return xh * cos + pltpu.roll(xh, HEAD_DIM // 2, 1) * sin_signed


def _proj_kernel(*refs, mode, norm_input):
    if norm_input:
        g_in_ref, refs, h_out_ref = refs[0], refs[1:-1], refs[-1]
    h_ref, w_ref = refs[0], refs[1]
    o_ref = refs[-1]
    tm, tn = o_ref.shape
    for r in range(tm // PROJ_RB):
        rows = pl.ds(r * PROJ_RB, PROJ_RB)
        if norm_input:
            x = h_ref[rows, :]
            h = (x * lax.rsqrt(jnp.mean(x * x, axis=-1, keepdims=True) + NORM_EPS) * g_in_ref[...]).astype(BF16)
            h_out_ref[rows, :] = h
        else:
            h = h_ref[rows, :]
        acc = jnp.dot(h, w_ref[...], preferred_element_type=F32)
        if mode == "scale":
            o_ref[rows, :] = (acc * refs[2][...]).astype(o_ref.dtype)
        elif mode == "rope":
            cs_ref, cos_ref, sin_ref = refs[2:5]
            cos, sin = cos_ref[rows, :], sin_ref[rows, :]
            for c in range(tn // HEAD_DIM):
                sl = slice(c * HEAD_DIM, (c + 1) * HEAD_DIM)
                o_ref[rows, sl] = (_rope_half(acc[:, sl], cos, sin) * cs_ref[:, sl]).astype(o_ref.dtype)
        elif mode == "norm_rope":
            cs_ref, g_ref, cos_ref, sin_ref = refs[2:6]
            cos, sin = cos_ref[rows, :], sin_ref[rows, :]
            for c in range(tn // HEAD_DIM):
                sl = slice(c * HEAD_DIM, (c + 1) * HEAD_DIM)
                xh = acc[:, sl]
                y = xh * lax.rsqrt(jnp.mean(xh * xh, axis=-1, keepdims=True) + NORM_EPS) * g_ref[:, sl]
                o_ref[rows, sl] = (_rope_half(y, cos, sin) * cs_ref[:, sl]).astype(o_ref.dtype)
        else:
            raise ValueError(mode)


def _proj(h, w, mode, row_vecs, tables, tm, tn, name, norm_gain=None):
    s, d = h.shape
    n = w.shape[1]
    tm = min(tm, s)
    in_specs = [pl.BlockSpec((tm, d), lambda i, j: (i, 0)), pl.BlockSpec((d, tn), lambda i, j: (0, j))]
    in_specs += [pl.BlockSpec((1, tn), lambda i, j: (0, j)) for _ in row_vecs]
    in_specs += [pl.BlockSpec((tm, HEAD_DIM), lambda i, j: (i, 0)) for _ in tables]
    out_shape = [jax.ShapeDtypeStruct((s, n), BF16)]
    out_specs = [pl.BlockSpec((tm, tn), lambda i, j: (i, j))]
    args = [h, w, *row_vecs, *tables]
    if norm_gain is not None:
        assert n == tn
        in_specs = [pl.BlockSpec((1, d), lambda i, j: (0, 0))] + in_specs
        args = [norm_gain.reshape(1, d)] + args
        out_shape.append(jax.ShapeDtypeStruct((s, d), BF16))
        out_specs.append(pl.BlockSpec((tm, d), lambda i, j: (i, 0)))
    outs = pl.pallas_call(
        functools.partial(_proj_kernel, mode=mode, norm_input=norm_gain is not None),
        out_shape=out_shape,
        grid=(s // tm, n // tn),
        in_specs=in_specs,
        out_specs=out_specs,
        compiler_params=_params(("arbitrary", "arbitrary")),
        name=name,
    )(*args)
    return outs[0] if norm_gain is None else outs


def _gates_kernel(h_ref, w_ref, b_ref, o_ref, wb_ref):
    @pl.when(pl.program_id(1) == 0)
    def _():
        wb_ref[...] = w_ref[...].astype(BF16)

    for r in range(o_ref.shape[0] // PROJ_RB):
        rows = pl.ds(r * PROJ_RB, PROJ_RB)
        acc = jnp.dot(h_ref[rows, :], wb_ref[...], preferred_element_type=F32)
        o_ref[rows, :] = _sigmoid(acc + b_ref[...]).astype(o_ref.dtype)


def _proj_gates(h, w_in_all, layer, bias, tm=2048, tn=1024):
    s, d = h.shape
    tm = min(tm, s)
    n = _WID["gates"]
    first = _OFF["gates"] // tn
    return pl.pallas_call(
        _gates_kernel,
        out_shape=jax.ShapeDtypeStruct((s, n), BF16),
        grid=(n // tn, s // tm),
        in_specs=[pl.BlockSpec((tm, d), lambda j, i: (i, 0)),
                  pl.BlockSpec((None, d, tn), lambda j, i: (layer, 0, first + j)),
                  pl.BlockSpec((1, tn), lambda j, i: (0, j))],
        out_specs=pl.BlockSpec((tm, tn), lambda j, i: (i, j)),
        scratch_shapes=[pltpu.VMEM((d, tn), BF16)],
        compiler_params=_params(("arbitrary", "arbitrary")),
        name="proj_gates",
    )(h, w_in_all, bias)


WIN_BQ = 128


def _window_consts(sink_ref, kv_head):
    bq, groups = WIN_BQ, 2
    span, rows = bq + 2 * A_HALF_WINDOW, groups * bq
    row = lax.broadcasted_iota(jnp.int32, (rows, span), 0)
    col_minus_row = lax.broadcasted_iota(jnp.int32, (rows, span), 1) - jnp.where(row >= bq, row - bq, row)
    r1 = lax.broadcasted_iota(jnp.int32, (rows, 1), 0)
    sink = jnp.where(r1 >= bq, sink_ref[kv_head * groups + 1], sink_ref[kv_head * groups]) * LOG2E
    return col_minus_row, sink


def _window_block(consts, q_ref, k_ref, v_ref, o_ref, q0, qoff, seq_len):
    col_minus_row, sink = consts
    bq, hw = WIN_BQ, A_HALF_WINDOW
    span = bq + 2 * hw
    start = pl.multiple_of(jnp.clip(q0 - hw, 0, seq_len - span), bq)
    k = k_ref[pl.ds(start, span), :]
    v = v_ref[pl.ds(start, span), :]
    qb = q_ref[pl.ds(qoff, bq), :]
    q2 = jnp.concatenate([qb[:, :HEAD_DIM], qb[:, HEAD_DIM:]], axis=0)
    s = lax.dot_general(q2, k, (((1,), (1,)), ((), ())), preferred_element_type=F32)
    s = jnp.where(jnp.abs(col_minus_row + (start - q0)) <= hw, s, NEG_BIG)
    m = jnp.maximum(jnp.max(s, axis=1, keepdims=True), sink)
    p = jnp.exp2(s - m)
    pv, psum = _weighted_values(p, v)
    o = pv / (psum + jnp.exp2(sink - m))
    o_ref[pl.ds(qoff, bq), :HEAD_DIM] = o[:bq].astype(o_ref.dtype)
    o_ref[pl.ds(qoff, bq), HEAD_DIM:] = o[bq:].astype(o_ref.dtype)


def _window_kernel(sink_ref, q_ref, k_ref, v_ref, o_ref, *, seq_len, nsub):
    consts = _window_consts(sink_ref, pl.program_id(0))
    t0 = pl.program_id(1) * nsub * WIN_BQ

    def body(b, carry):
        qoff = pl.multiple_of(b * WIN_BQ, WIN_BQ)
        _window_block(consts, q_ref, k_ref, v_ref, o_ref, t0 + qoff, qoff, seq_len)
        return carry

    lax.fori_loop(0, nsub, body, 0, unroll=True)


def _window_attention(r_arr, p_arr, sink, *, seq_len, max_step=2048):
    nsub = min(max_step, seq_len) // WIN_BQ
    tq = nsub * WIN_BQ
    qc, kc, vc = ROPE_OFF["aq"] // (2 * HEAD_DIM), ROPE_OFF["ak"] // HEAD_DIM, PLAIN_OFF["av"] // HEAD_DIM
    return pl.pallas_call(
        functools.partial(_window_kernel, seq_len=seq_len, nsub=nsub),
        out_shape=jax.ShapeDtypeStruct((seq_len, BRANCH_WIDTH), BF16),
        grid=(2, seq_len // tq),
        in_specs=[pl.BlockSpec(memory_space=pltpu.SMEM),
                  pl.BlockSpec((tq, 2 * HEAD_DIM), lambda c, i: (i, qc + c)),
                  pl.BlockSpec((seq_len, HEAD_DIM), lambda c, i: (0, kc + c)),
                  pl.BlockSpec((seq_len, HEAD_DIM), lambda c, i: (0, vc + c))],
        out_specs=pl.BlockSpec((tq, 2 * HEAD_DIM), lambda c, i: (i, c)),
        compiler_params=_params(("arbitrary", "arbitrary")),
        name="window_attention",
    )(sink, r_arr, r_arr, p_arr)


DIL_BQ = 128
DIL_HALF = 64
DIL_HALO = DIL_HALF * max(d for _, d in C_PATTERNS)


DIL_BLK = 128


def _residue_major(dil):
    per = DIL_BLK // dil
    p = np.zeros((DIL_BLK, DIL_BLK), np.float32)
    for r in range(dil):
        for i in range(per):
            p[r * per + i, i * dil + r] = 1.0
    return p


def _to_residue_major(src_ref, dst_ref, perm, dil, first_blk, n_blk):
    per = DIL_BLK // dil
    for b in range(first_blk, first_blk + n_blk, 2):
        pair = jnp.concatenate([src_ref[b * DIL_BLK:(b + 1) * DIL_BLK, :],
                                src_ref[(b + 1) * DIL_BLK:(b + 2) * DIL_BLK, :]], axis=1)
        res = jnp.dot(perm, pair, preferred_element_type=F32)
        dst_ref[b * dil:(b + 1) * dil] = res[:, :HEAD_DIM].reshape(dil, per, HEAD_DIM)
        dst_ref[(b + 1) * dil:(b + 2) * dil] = res[:, HEAD_DIM:].reshape(dil, per, HEAD_DIM)


def _dilated_kernel(q0_ref, q1_ref, q2_ref, kp_ref, kc_ref, kn_ref, vp_ref, vc_ref, vn_ref, perm1_ref, perm2_ref,
                    o_ref, kw_ref, vw_ref, qd1_ref, kd1_ref, vd1_ref, qd2_ref, kd2_ref, vd2_ref, og_ref, lg_ref,
                    *, seq_len, tq):
    t0 = pl.program_id(1) * tq
    halo, bq, half = DIL_HALO, DIL_BQ, DIL_HALF
    span = bq + 2 * half
    for w_ref, prev, cur, nxt in ((kw_ref, kp_ref, kc_ref, kn_ref), (vw_ref, vp_ref, vc_ref, vn_ref)):
        w_ref[:halo] = prev[...]
        w_ref[halo:halo + tq] = cur[...]
        w_ref[halo + tq:] = nxt[...]
    residue_major = {}
    for (_, dil), q_ref, perm_ref, qd_ref, kd_ref, vd_ref in (
            (C_PATTERNS[1], q1_ref, perm1_ref, qd1_ref, kd1_ref, vd1_ref),
            (C_PATTERNS[2], q2_ref, perm2_ref, qd2_ref, kd2_ref, vd2_ref)):
        perm = perm_ref[...]
        first_blk = (halo - half * dil) // DIL_BLK
        n_blk = tq // DIL_BLK + dil
        _to_residue_major(q_ref, qd_ref, perm, dil, 0, tq // DIL_BLK)
        _to_residue_major(kw_ref, kd_ref, perm, dil, first_blk, n_blk)
        _to_residue_major(vw_ref, vd_ref, perm, dil, first_blk, n_blk)
        residue_major[dil] = (qd_ref, kd_ref, vd_ref)

    ii = lax.broadcasted_iota(jnp.int32, (bq, span), 0)
    jj = lax.broadcasted_iota(jnp.int32, (bq, span), 1)
    in_band = jnp.abs(jj - half - ii) <= half

    for g, (_, dil) in enumerate(C_PATTERNS):
        nb = tq // (bq * dil)

        def body(it, carry, g=g, dil=dil, nb=nb):
            res, sub = it // nb, it % nb
            q_row = res + dil * bq * sub
            w_row = halo + q_row - half * dil
            if dil == 1:
                q = q0_ref[pl.ds(pl.multiple_of(q_row, bq), bq), :]
                k = kw_ref[pl.ds(pl.multiple_of(w_row, half), span), :]
                v = vw_ref[pl.ds(pl.multiple_of(w_row, half), span), :]
            else:
                qd_ref, kd_ref, vd_ref = residue_major[dil]
                q_blk = sub * bq * dil // DIL_BLK
                k_blk = (halo - half * dil) // DIL_BLK + q_blk
                q = qd_ref[pl.ds(q_blk * dil + res, bq * dil // DIL_BLK, stride=dil)]
                k = kd_ref[pl.ds(k_blk * dil + res, span * dil // DIL_BLK, stride=dil)]
                v = vd_ref[pl.ds(k_blk * dil + res, span * dil // DIL_BLK, stride=dil)]
                q = q.reshape(bq, HEAD_DIM).astype(BF16)
                k = k.reshape(span, HEAD_DIM).astype(BF16)
                v = v.reshape(span, HEAD_DIM).astype(BF16)
            s = lax.dot_general(q, k, (((1,), (1,)), ((), ())), preferred_element_type=F32)
            ktok = t0 + q_row - half * dil + dil * jj
            s = jnp.where(in_band & (ktok >= 0) & (ktok < seq_len), s, NEG_BIG)
            m = jnp.max(s, axis=1, keepdims=True)
            p = jnp.exp2(s - m)
            pv, psum = _weighted_values(p, v)
            og_ref[g, pl.ds(q_row, bq, stride=dil), :] = pv / psum
            lg_ref[g, pl.ds(q_row, bq, stride=dil), :] = m + jnp.log2(psum)
            return carry

        lax.fori_loop(0, tq // bq, body, 0, unroll=True)

    l0, l1, l2 = lg_ref[0], lg_ref[1], lg_ref[2]
    mx = jnp.maximum(jnp.maximum(l0, l1), l2)
    e0, e1, e2 = jnp.exp2(l0 - mx), jnp.exp2(l1 - mx), jnp.exp2(l2 - mx)
    o_ref[...] = ((e0 * og_ref[0] + e1 * og_ref[1] + e2 * og_ref[2]) / (e0 + e1 + e2)).astype(o_ref.dtype)


def _dilated_attention(r_arr, p_arr, *, seq_len, tq=4096):
    tq = min(tq, seq_len)
    halo = DIL_HALO
    ratio, n_halo = tq // halo, seq_len // halo
    qc, kc, vc = ROPE_OFF["cq"] // HEAD_DIM, ROPE_OFF["ck"] // HEAD_DIM, PLAIN_OFF["cv"] // HEAD_DIM
    n_heads = BRANCH_WIDTH // HEAD_DIM

    def window_specs(col):
        return [pl.BlockSpec((halo, HEAD_DIM), lambda h, i: (jnp.maximum(i * ratio - 1, 0), col + h)),
                pl.BlockSpec((tq, HEAD_DIM), lambda h, i: (i, col + h)),
                pl.BlockSpec((halo, HEAD_DIM), lambda h, i: (jnp.minimum((i + 1) * ratio, n_halo - 1), col + h))]

    q_specs = [pl.BlockSpec((tq, HEAD_DIM), lambda h, i, g=g: (i, qc + n_heads * g + h)) for g in range(3)]
    perm_spec = pl.BlockSpec((DIL_BLK, DIL_BLK), lambda h, i: (0, 0))
    dilations = [d for _, d in C_PATTERNS[1:]]
    assert C_PATTERNS[0][1] == 1 and all(DIL_BLK % d == 0 and (DIL_BLK // d) % 8 == 0 for d in dilations)
    window = tq + 2 * halo
    residue_scratch = []
    for d in dilations:
        residue_scratch += [pltpu.VMEM((tq // DIL_BLK * d, DIL_BLK // d, HEAD_DIM), F32),
                            pltpu.VMEM((window // DIL_BLK * d, DIL_BLK // d, HEAD_DIM), F32),
                            pltpu.VMEM((window // DIL_BLK * d, DIL_BLK // d, HEAD_DIM), F32)]
    return pl.pallas_call(
        functools.partial(_dilated_kernel, seq_len=seq_len, tq=tq),
        out_shape=jax.ShapeDtypeStruct((seq_len, BRANCH_WIDTH), BF16),
        grid=(n_heads, seq_len // tq),
        in_specs=q_specs + window_specs(kc) + window_specs(vc) + [perm_spec] * len(dilations),
        out_specs=pl.BlockSpec((tq, HEAD_DIM), lambda h, i: (i, h)),
        scratch_shapes=[pltpu.VMEM((window, HEAD_DIM), BF16), pltpu.VMEM((window, HEAD_DIM), BF16)]
        + residue_scratch + [pltpu.VMEM((3, tq, HEAD_DIM), F32), pltpu.VMEM((3, tq, HEAD_DIM), F32)],
        compiler_params=_params(("arbitrary", "arbitrary")),
        name="dilated_attention",
    )(r_arr, r_arr, r_arr, r_arr, r_arr, r_arr, p_arr, p_arr, p_arr,
      *[jnp.asarray(_residue_major(d), BF16) for d in dilations])


NBR_ROWS = 4
NBR_UNION = 12
NBR_SUB = 32


def _nbr_bias_kernel(rpb_ref, o_ref, t_ref):
    h = pl.program_id(0)
    n_dr, n_dc = 2 * B_WIN_ROWS - 1, 2 * B_WIN_COLS - 1
    qc = lax.broadcasted_iota(jnp.int32, (GRID_W, GRID_W), 0)
    kc = lax.broadcasted_iota(jnp.int32, (GRID_W, GRID_W), 1)
    col_start = jnp.clip(qc - B_WIN_COLS // 2, 0, GRID_W - B_WIN_COLS)
    col_ok = (kc >= col_start) & (kc < col_start + B_WIN_COLS)
    dc = kc - qc + (B_WIN_COLS - 1)
    for dr in range(n_dr):
        t = jnp.full((GRID_W, GRID_W), NEG_BIG, F32)
        for d in range(n_dc):
            t = jnp.where(dc == d, rpb_ref[(h * n_dr + dr) * n_dc + d] * LOG2E, t)
        t_ref[dr] = jnp.where(col_ok, t, NEG_BIG)
    o_ref[...] = jnp.full(o_ref.shape, NEG_BIG, F32)
    half = B_WIN_ROWS // 2
    for case in range(3):
        for a in range(NBR_ROWS):
            win, off = ((0, a), (a, half), (NBR_UNION - B_WIN_ROWS, half + a))[case]
            for j in range(B_WIN_ROWS):
                o_ref[case, a * GRID_W:(a + 1) * GRID_W, (win + j) * GRID_W:(win + j + 1) * GRID_W] = (
                    t_ref[j - off + B_WIN_ROWS - 1])


def _nbr_bias(rpb):
    n_heads = rpb.shape[0]
    shape = (3, NBR_ROWS * GRID_W, NBR_UNION * GRID_W)
    return pl.pallas_call(
        _nbr_bias_kernel,
        out_shape=jax.ShapeDtypeStruct((n_heads,) + shape, F32),
        grid=(n_heads,),
        in_specs=[pl.BlockSpec(memory_space=pltpu.SMEM)],
        out_specs=pl.BlockSpec((None,) + shape, lambda h: (h, 0, 0, 0)),
        scratch_shapes=[pltpu.VMEM((2 * B_WIN_ROWS - 1, GRID_W, GRID_W), F32)],
        compiler_params=_params(("arbitrary",)),
        name="nbr_bias",
    )(rpb.reshape(-1))


def _nbr_kernel(q_ref, k_ref, v_ref, b_first_ref, b_mid_ref, b_last_ref, o_ref, *, n_rows, n_sub):
    step = pl.program_id(1)
    nq, nk = NBR_ROWS * GRID_W, NBR_UNION * GRID_W
    for sb in range(n_sub):
        bias_ref = b_first_ref if sb == 0 else (b_last_ref if sb == n_sub - 1 else b_mid_ref)
        r0 = (step * n_sub + sb) * NBR_ROWS
        u0 = jnp.clip(r0 - B_WIN_ROWS // 2, 0, n_rows - NBR_UNION)
        kstart = pl.multiple_of(u0 * GRID_W, GRID_W)
        k = k_ref[pl.ds(kstart, nk), :]
        v = v_ref[pl.ds(kstart, nk), :]
        q = q_ref[sb * nq:(sb + 1) * nq, :]
        s = lax.dot_general(q, k, (((1,), (1,)), ((), ())), preferred_element_type=F32) + bias_ref[...]
        m = jnp.max(s, axis=1, keepdims=True)
        p = jnp.exp2(s - m)
        pv, psum = _weighted_values(p, v)
        o_ref[sb * nq:(sb + 1) * nq, :] = (pv / psum).astype(o_ref.dtype)


def _nbr_attention(p_arr, bias, *, seq_len):
    n_rows = seq_len // GRID_W
    n_sub = min(NBR_SUB, n_rows // NBR_ROWS)
    tq = n_sub * NBR_ROWS * GRID_W
    n_steps = seq_len // tq
    qc, kc, vc = (PLAIN_OFF[n] // HEAD_DIM for n in ("bq", "bk", "bv"))
    tile = (None, None, NBR_ROWS * GRID_W, NBR_UNION * GRID_W)
    return pl.pallas_call(
        functools.partial(_nbr_kernel, n_rows=n_rows, n_sub=n_sub),
        out_shape=jax.ShapeDtypeStruct((seq_len, BRANCH_WIDTH), BF16),
        grid=(BRANCH_WIDTH // HEAD_DIM, n_steps),
        in_specs=[pl.BlockSpec((tq, HEAD_DIM), lambda h, i: (i, qc + h)),
                  pl.BlockSpec((seq_len, HEAD_DIM), lambda h, i: (0, kc + h)),
                  pl.BlockSpec((seq_len, HEAD_DIM), lambda h, i: (0, vc + h)),
                  pl.BlockSpec(tile, lambda h, i: (h, jnp.where(i == 0, 0, 1), 0, 0)),
                  pl.BlockSpec(tile, lambda h, i: (h, 1, 0, 0)),
                  pl.BlockSpec(tile, lambda h, i: (h, jnp.where(i == n_steps - 1, 2, 1), 0, 0))],
        out_specs=pl.BlockSpec((tq, HEAD_DIM), lambda h, i: (i, h)),
        compiler_params=_params(("arbitrary", "arbitrary")),
        name="nbr_attention",
    )(p_arr, p_arr, p_arr, bias, bias, bias)


DENSE_RB = 256


def _dense_kernel(q_ref, k_ref, v_ref, o_ref, q2_ref, v2_ref, m_ref, acc_ref, s0_ref, *, seq_len, tq, tk):
    @pl.when(pl.program_id(1) == 0)
    def _():
        v2_ref[:, :HEAD_DIM] = v_ref[...]
        v2_ref[:, HEAD_DIM:] = jnp.ones((seq_len, HEAD_DIM), BF16)

    q2_ref[:tq] = q_ref[:, :HEAD_DIM]
    q2_ref[tq:] = q_ref[:, HEAD_DIM:]
    m_ref[...] = jnp.full(m_ref.shape, NEG_BIG, F32)
    acc_ref[...] = jnp.zeros(acc_ref.shape, F32)
    n_chunks = seq_len // tk

    def logits(r, k):
        return lax.dot_general(q2_ref[pl.ds(r * DENSE_RB, DENSE_RB), :], k, (((1,), (1,)), ((), ())),
                               preferred_element_type=F32)

    s0_ref[...] = logits(0, k_ref[pl.ds(0, tk), :])

    def body(j, carry):
        koff = pl.multiple_of(j * tk, tk)
        k = k_ref[pl.ds(koff, tk), :]
        v2 = v2_ref[pl.ds(koff, tk), :]
        for r in range((2 * tq) // DENSE_RB):
            rows = pl.ds(r * DENSE_RB, DENSE_RB)
            s = s0_ref[...] if r == 0 else logits(r, k)
            m_prev = m_ref[rows, :]
            m_new = jnp.maximum(m_prev, jnp.max(s, axis=1, keepdims=True))
            alpha = jnp.exp2(m_prev - m_new)
            p = jnp.exp2(s - m_new).astype(BF16)
            acc_ref[rows, :] = acc_ref[rows, :] * alpha + jnp.dot(p, v2, preferred_element_type=F32)
            m_ref[rows, :] = m_new
        next_off = pl.multiple_of(jnp.minimum(j + 1, n_chunks - 1) * tk, tk)
        s0_ref[...] = logits(0, k_ref[pl.ds(next_off, tk), :])
        return carry

    lax.fori_loop(0, n_chunks, body, 0)
    acc = acc_ref[...]
    o = acc[:, :HEAD_DIM] / acc[:, HEAD_DIM:]
    o_ref[:, :HEAD_DIM] = o[:tq].astype(o_ref.dtype)
    o_ref[:, HEAD_DIM:] = o[tq:].astype(o_ref.dtype)


def _dense_attention(n_arr, p_arr, *, seq_len, tq=2048, tk=4096):
    tq, tk = min(tq, seq_len), min(tk, seq_len)
    kc = NORM_OFF["dk"] // HEAD_DIM
    vc = PLAIN_OFF["dv"] // HEAD_DIM
    return pl.pallas_call(
        functools.partial(_dense_kernel, seq_len=seq_len, tq=tq, tk=tk),
        out_shape=jax.ShapeDtypeStruct((seq_len, BRANCH_WIDTH), BF16),
        grid=(2, seq_len // tq),
        in_specs=[pl.BlockSpec((tq, 2 * HEAD_DIM), lambda g, i: (i, g)),
                  pl.BlockSpec((seq_len, HEAD_DIM), lambda g, i: (0, kc + g), pipeline_mode=pl.Buffered(1)),
                  pl.BlockSpec((seq_len, HEAD_DIM), lambda g, i: (0, vc + g), pipeline_mode=pl.Buffered(1))],
        out_specs=pl.BlockSpec((tq, 2 * HEAD_DIM), lambda g, i: (i, g)),
        scratch_shapes=[pltpu.VMEM((2 * tq, HEAD_DIM), BF16), pltpu.VMEM((seq_len, 2 * HEAD_DIM), BF16),
                        pltpu.VMEM((2 * tq, 1), F32), pltpu.VMEM((2 * tq, 2 * HEAD_DIM), F32),
                        pltpu.VMEM((DENSE_RB, tk), F32)],
        compiler_params=_params(("arbitrary", "arbitrary")),
        name="dense_attention",
    )(n_arr, n_arr, p_arr)


def _merge_kernel(ya_ref, yb_ref, yc_ref, yd_ref, z_ref, gate_ref, x_ref, wb_ref, wo_ref, g_ref, *out_refs, final):
    merged = None
    for n, y_ref in enumerate((ya_ref, yb_ref, yc_ref, yd_ref)):
        z = z_ref[:, n * BRANCH_WIDTH:(n + 1) * BRANCH_WIDTH].astype(F32)
        t = (y_ref[...].astype(F32) * (z * _sigmoid(z))).astype(BF16)
        br = jnp.dot(t, wb_ref[n], preferred_element_type=F32)
        term = gate_ref[:, n * D_MODEL:(n + 1) * D_MODEL].astype(F32) * br
        merged = term if merged is None else merged + term
    x_new = x_ref[...] + jnp.dot(merged.astype(BF16), wo_ref[...], preferred_element_type=F32)
    normed = x_new * lax.rsqrt(jnp.mean(x_new * x_new, axis=-1, keepdims=True) + NORM_EPS) * g_ref[...]
    if final:
        out_refs[0][...] = normed
    else:
        out_refs[0][...] = x_new
        out_refs[1][...] = normed.astype(BF16)


def _merge(ys, p_arr, gates, x, wb, wo, g_next, *, final, tm=256):
    s = x.shape[0]
    row = lambda w: pl.BlockSpec((tm, w), lambda i: (i, 0))
    once = dict(pipeline_mode=pl.Buffered(1))
    in_specs = ([row(BRANCH_WIDTH)] * N_BRANCH
                + [row(N_BRANCH * BRANCH_WIDTH), row(N_BRANCH * D_MODEL), row(D_MODEL),
                   pl.BlockSpec((N_BRANCH, BRANCH_WIDTH, D_MODEL), lambda i: (0, 0, 0), **once),
                   pl.BlockSpec((D_MODEL, D_MODEL), lambda i: (0, 0), **once),
                   pl.BlockSpec((1, D_MODEL), lambda i: (0, 0))])
    if final:
        out_shape = [jax.ShapeDtypeStruct((s, D_MODEL), F32)]
    else:
        out_shape = [jax.ShapeDtypeStruct((s, D_MODEL), F32), jax.ShapeDtypeStruct((s, D_MODEL), BF16)]
    return pl.pallas_call(
        functools.partial(_merge_kernel, final=final),
        out_shape=out_shape,
        grid=(s // tm,),
        in_specs=in_specs,
        out_specs=[row(D_MODEL)] * len(out_shape),
        compiler_params=_params(("arbitrary",)),
        name="merge",
    )(*ys, p_arr, gates, x, wb, wo, g_next.reshape(1, D_MODEL))


@functools.lru_cache(maxsize=None)
def _rope_tables(seq_len):
    t = np.arange(seq_len)

    def angles(pos, dim):
        inv_freq = np.float32(ROPE_THETA) ** (-np.arange(dim // 2, dtype=np.float32) / np.float32(dim // 2))
        return pos.astype(np.float32)[:, None] * inv_freq.astype(np.float32)[None, :]

    a1 = angles(t, HEAD_DIM)
    cos1 = np.concatenate([np.cos(a1), np.cos(a1)], axis=1)
    sin1 = np.concatenate([-np.sin(a1), np.sin(a1)], axis=1)
    ar = angles(t // GRID_W, HEAD_DIM // 2)
    ac = angles(t % GRID_W, HEAD_DIM // 2)
    cos2 = np.concatenate([np.cos(ar), np.cos(ac), np.cos(ar), np.cos(ac)], axis=1)
    sin2 = np.concatenate([-np.sin(ar), -np.sin(ac), np.sin(ar), np.sin(ac)], axis=1)
    assert all(a.dtype == np.float32 for a in (cos1, sin1, cos2, sin2))
    return (cos1, sin1), (cos2, sin2)


def _axial_reorder(a):
    lead = a.shape[:-1]
    return a.reshape(*lead, -1, 2, 2, HEAD_DIM // 4).swapaxes(-3, -2).reshape(*lead, -1)


def _gather_cols(w, seg):
    cols = np.concatenate([np.arange(_OFF[n], _OFF[n] + _WID[n]) for n in seg])
    runs = np.split(cols, np.nonzero(np.diff(cols) != 1)[0] + 1)
    return jnp.concatenate([w[:, r[0]:r[-1] + 1] for r in runs], axis=1).astype(BF16)


def _col_scale(seg, scaled):
    return jnp.asarray(np.concatenate(
        [np.full((_WID[n],), scaled.get(n, 1.0), np.float32) for n in seg]).reshape(1, -1))


def _layer(x, h, w_in_all, layer, norm_gain, gate_b, a_sink, b_rpb, d_q_norm, d_k_norm, w_branch, w_out, g_next,
           tables, final):
    s = x.shape[0]
    (cos1, sin1), (cos2, sin2) = tables
    tm = 1024
    w_in = w_in_all[layer, :, :_OFF["gates"]]

    gains = _axial_reorder(jnp.concatenate([jnp.tile(d_q_norm, _WID["dq"] // HEAD_DIM),
                                            jnp.tile(d_k_norm, _WID["dk"] // HEAD_DIM)]))
    n_arr = _proj(x if h is None else h, _axial_reorder(_gather_cols(w_in, NORM_SEG)), "norm_rope",
                  [_col_scale(NORM_SEG, {"dq": SCALE}), gains.reshape(1, -1)],
                  [cos2, sin2], tm=tm, tn=NORM_W, name="proj_norm_rope", norm_gain=norm_gain if h is None else None)
    if h is None:
        n_arr, h = n_arr
    r_arr = _proj(h, _gather_cols(w_in, ROPE_SEG), "rope", [_col_scale(ROPE_SEG, {"aq": SCALE, "cq": SCALE})],
                  [cos1, sin1], tm=2 * tm, tn=ROPE_W // 2, name="proj_rope")
    p_arr = _proj(h, _gather_cols(w_in, PLAIN_SEG), "scale", [_col_scale(PLAIN_SEG, {"bq": SCALE})], [],
                  tm=2 * tm, tn=PLAIN_W // 3, name="proj_plain")
    gates = _proj_gates(h, w_in_all, layer, gate_b.reshape(1, -1))

    ya = _window_attention(r_arr, p_arr, a_sink, seq_len=s)
    yb = _nbr_attention(p_arr, _nbr_bias(b_rpb), seq_len=s)
    yc = _dilated_attention(r_arr, p_arr, seq_len=s)
    yd = _dense_attention(n_arr, p_arr, seq_len=s)
    return _merge((ya, yb, yc, yd), p_arr, gates, x, w_branch.astype(BF16), w_out.astype(BF16), g_next, final=final)


def kernel(x, norm_g, w_in, gate_b, a_sink, b_rpb, d_q_norm, d_k_norm, w_branch, w_out, final_norm_g):
    b, s, d = x.shape
    depth = w_in.shape[0]
    tables = tuple(tuple(jnp.asarray(a) for a in grp) for grp in _rope_tables(s))
    outs = []
    for bi in range(b):
        xb = x.reshape(s, d) if b == 1 else x[bi]
        h = None
        for layer in range(depth):
            final = layer == depth - 1
            g_next = final_norm_g if final else norm_g[layer + 1]
            res = _layer(xb, h, w_in, layer, norm_g[layer], gate_b[layer], a_sink[layer], b_rpb[layer], d_q_norm[layer],
                         d_k_norm[layer], w_branch[layer], w_out[layer], g_next, tables, final)
            if final:
                xb = res[0]
            else:
                xb, h = res
        outs.append(xb)
    return outs[0].reshape(1, s, d) if b == 1 else jnp.stack(outs, axis=0)
```
